```python
import jax, jax.numpy as jnp
from jax import lax
import numpy as np

D_MODEL = 2048
BATCH = 8
SEQ = 4096
DEPTH = 1
DEC_BATCH = 16
DEC_SEQ = 16
PAST_LEN = 4096

CHUNK = 64
MIX_DIM = D_MODEL
GDN_DK = 128
GDN_DV = 128
GDN_HEADS = (MIX_DIM // 2) // GDN_DV
GDN_QK_DIM = GDN_HEADS * GDN_DK
GDN_V_DIM = GDN_HEADS * GDN_DV
CONV_W = 4
CONV_DIM = 2 * GDN_QK_DIM + GDN_V_DIM
MLP_DIM = MIX_DIM - GDN_V_DIM
MLP_GROUP_DIM = 128
MLP_GROUPS = MLP_DIM // MLP_GROUP_DIM
MLP_CHUNK = 128
IN_DIM = CONV_DIM + 2 * GDN_HEADS + GDN_V_DIM + 2 * MLP_DIM
PEER_HEADS = 8
PEER_QDIM = 256
PEER_QHALF = PEER_QDIM // 2
N_KEYS = 128
N_EXPERTS = N_KEYS * N_KEYS
PEER_TOPK = 16
PEER_BLOCK = 256
PLE_DIM = 256
EPS = 1e-6

kernel_name = 'hymba_gdn_gmlp_peer_stream_step'


def _rmsnorm(x, w):
    xf = x.astype(jnp.float32)
    y = xf * lax.rsqrt(jnp.mean(xf * xf, axis=-1, keepdims=True) + EPS)
    return (y * w.astype(jnp.float32)).astype(x.dtype)


def _layernorm(x, g, b):
    xf = x.astype(jnp.float32)
    mu = jnp.mean(xf, axis=-1, keepdims=True)
    xc = xf - mu
    y = xc * lax.rsqrt(jnp.mean(xc * xc, axis=-1, keepdims=True) + EPS)
    return (y * g.astype(jnp.float32) + b.astype(jnp.float32)).astype(x.dtype)


def _l2norm(x):
    return x * lax.rsqrt(jnp.sum(x * x, axis=-1, keepdims=True) + EPS)


def _causal_conv(x, buf, w):
    T = x.shape[1]
    xp = jnp.concatenate([buf.astype(x.dtype), x], axis=1)
    y = xp[:, 0:T] * w[0]
    for j in range(1, CONV_W):
        y = y + xp[:, j:j + T] * w[j]
    return jax.nn.silu(y), xp[:, T:]


def _gdn_core(q, k, v, g, beta, s0):
    f32 = jnp.float32
    B, T, H, _ = q.shape
    q = _l2norm(q.astype(f32)) * (GDN_DK ** -0.5)
    k = _l2norm(k.astype(f32))
    v = v.astype(f32)
    g = g.astype(f32)
    beta = beta.astype(f32)
    pad = (-T) % CHUNK
    Tp = T + pad
    N = Tp // CHUNK

    def blocks(a):
        a = jnp.pad(a, [(0, 0), (0, pad)] + [(0, 0)] * (a.ndim - 2))
        a = a.reshape((B, N, CHUNK) + a.shape[2:])
        return jnp.moveaxis(a, 3, 1)

    q, k, v, g, beta = (blocks(a) for a in (q, k, v, g, beta))
    gc = jnp.cumsum(g, axis=-1)
    idx = jnp.arange(CHUNK)
    incl = idx[:, None] >= idx[None, :]
    strict = idx[:, None] > idx[None, :]
    decay = jnp.exp(jnp.where(incl, gc[..., :, None] - gc[..., None, :], -jnp.inf))
    kb = k * beta[..., None]
    lmat = jnp.where(strict, jnp.einsum('bhnid,bhnjd->bhnij', kb, k) * decay, 0.0)
    amat = lmat + jnp.eye(CHUNK, dtype=f32)
    rhs = jnp.concatenate([v * beta[..., None], kb * jnp.exp(gc)[..., None]], axis=-1)
    sol = lax.linalg.triangular_solve(amat, rhs, left_side=True, lower=True, unit_diagonal=True)
    u_blk, w_blk = sol[..., :GDN_DV], sol[..., GDN_DV:]
    attn = jnp.einsum('bhnid,bhnjd->bhnij', q, k) * decay
    q_dec = q * jnp.exp(gc)[..., None]
    k_dec = k * jnp.exp(gc[..., -1:] - gc)[..., None]
    blk_decay = jnp.exp(gc[..., -1])

    def step(s, inp):
        u_n, w_n, qd_n, kd_n, at_n, bd_n = inp
        v_new = u_n - jnp.einsum('bhck,bhkv->bhcv', w_n, s)
        o_n = jnp.einsum('bhck,bhkv->bhcv', qd_n, s) + jnp.einsum('bhij,bhjv->bhiv', at_n, v_new)
        s = s * bd_n[..., None, None] + jnp.einsum('bhck,bhcv->bhkv', kd_n, v_new)
        return s, o_n

    xs = tuple(jnp.moveaxis(a, 2, 0) for a in (u_blk, w_blk, q_dec, k_dec, attn, blk_decay))
    s_fin, o = lax.scan(step, s0.astype(f32), xs)
    o = jnp.transpose(o, (1, 0, 3, 2, 4)).reshape(B, Tp, H, GDN_DV)[:, :T]
    return o, s_fin


def _chunk_mix(vn, w_s, b_s):
    B, T = vn.shape[0], vn.shape[1]
    pad = (-T) % MLP_CHUNK
    Tp = T + pad
    vp = jnp.pad(vn, [(0, 0), (0, pad), (0, 0), (0, 0)])
    vp = vp.reshape(B, Tp // MLP_CHUNK, MLP_CHUNK, MLP_GROUPS, MLP_GROUP_DIM)
    wm = jnp.tril(w_s)
    s = jnp.einsum('gts,bnsgc->bntgc', wm, vp) + b_s.T[None, None, :, :, None]
    return s.reshape(B, Tp, MLP_GROUPS, MLP_GROUP_DIM)[:, :T]


def _peer(x, w_q, sub_keys, peer_u, peer_v):
    B, T, D = x.shape
    nt = B * T
    pad = (-nt) % PEER_BLOCK
    xf = jnp.pad(x.reshape(nt, D), [(0, pad), (0, 0)]).reshape(-1, PEER_BLOCK, D)

    def blk(xb):
        q = (xb @ w_q).reshape(PEER_BLOCK, PEER_HEADS, 2, PEER_QHALF)
        sc = jnp.einsum('phsd,shnd->phsn', q, sub_keys).astype(jnp.float32)
        sv, si = lax.top_k(sc, PEER_TOPK)
        cand = (sv[:, :, 0, :, None] + sv[:, :, 1, None, :]).reshape(PEER_BLOCK, PEER_HEADS, -1)
        cidx = (si[:, :, 0, :, None] * N_KEYS + si[:, :, 1, None, :]).reshape(PEER_BLOCK, PEER_HEADS, -1)
        fv, fi = lax.top_k(cand, PEER_TOPK)
        eidx = jnp.take_along_axis(cidx, fi, axis=-1)
        gate = jax.nn.softmax(fv, axis=-1)
        act = jnp.einsum('pd,phkd->phk', xb, peer_u[eidx])
        coef = (gate * jax.nn.gelu(act.astype(jnp.float32), approximate=False)).astype(xb.dtype)
        return jnp.einsum('phk,phkd->pd', coef, peer_v[eidx])

    out = lax.map(blk, xf)
    return out.reshape(-1, D)[:nt].reshape(B, T, D)


def _layer(h, p, s0, buf, norm1, w_in, conv_w, a_log, dt_bias, gdn_norm, ln_g, ln_b,
           w_s, b_s, w_out, norm2, w_q, sub_keys, peer_u, peer_v, norm3, w_ple_gate, w_ple_proj):
    B, T, _ = h.shape
    n = _rmsnorm(h, norm1)
    proj = n @ w_in
    c0 = CONV_DIM
    c1 = c0 + GDN_HEADS
    c2 = c1 + GDN_HEADS
    c3 = c2 + GDN_V_DIM
    qkv, a, b, z, uv = jnp.split(proj, [c0, c1, c2, c3], axis=-1)
    qkv, new_buf = _causal_conv(qkv, buf, conv_w)
    q, k, v = jnp.split(qkv, [GDN_QK_DIM, 2 * GDN_QK_DIM], axis=-1)
    q = q.reshape(B, T, GDN_HEADS, GDN_DK)
    k = k.reshape(B, T, GDN_HEADS, GDN_DK)
    v = v.reshape(B, T, GDN_HEADS, GDN_DV)
    g = -jnp.exp(a_log.astype(jnp.float32)) * jax.nn.softplus(a.astype(jnp.float32) + dt_bias.astype(jnp.float32))
    beta = jax.nn.sigmoid(b.astype(jnp.float32))
    o, s_new = _gdn_core(q, k, v, g, beta, s0)
    o = _rmsnorm(o.astype(h.dtype), gdn_norm) * jax.nn.silu(z.reshape(B, T, GDN_HEADS, GDN_DV))
    o = o.reshape(B, T, GDN_V_DIM)
    uv = jax.nn.gelu(uv, approximate=False)
    u, vv = jnp.split(uv, 2, axis=-1)
    vn = _layernorm(vv, ln_g, ln_b)
    s = _chunk_mix(vn.reshape(B, T, MLP_GROUPS, MLP_GROUP_DIM), w_s, b_s).reshape(B, T, MLP_DIM)
    y_b = u * s
    h = h + jnp.concatenate([o, y_b], axis=-1) @ w_out
    h = h + _peer(_rmsnorm(h, norm2), w_q, sub_keys, peer_u, peer_v)
    gate = jax.nn.sigmoid(_rmsnorm(h, norm3) @ w_ple_gate)
    h = h + gate * (p @ w_ple_proj)
    return h, s_new, new_buf, vn


def setup_inputs(seed: int = 0) -> dict:
    key = jax.random.key(seed)
    ks = jax.random.split(key, 32)
    f32 = jnp.float32

    def nrm(k, shape, scale):
        return jax.random.normal(k, shape, f32) * scale

    L = DEPTH
    dt = jnp.exp(jax.random.uniform(ks[9], (L, GDN_HEADS), f32, float(np.log(1e-3)), float(np.log(0.1))))
    return {
        'x_prompt': nrm(ks[0], (BATCH, SEQ, D_MODEL), 1.0),
        'x_sample': nrm(ks[1], (DEC_BATCH, DEC_SEQ, D_MODEL), 1.0),
        'state_gdn': nrm(ks[2], (L, DEC_BATCH, GDN_HEADS, GDN_DK, GDN_DV), 0.1),
        'cache_conv': nrm(ks[3], (L, DEC_BATCH, CONV_W - 1, CONV_DIM), 1.0),
        'p_prompt': nrm(ks[4], (L, BATCH, SEQ, PLE_DIM), 1.0),
        'p_sample': nrm(ks[5], (L, DEC_BATCH, DEC_SEQ, PLE_DIM), 1.0),
        'norm1': 1.0 + nrm(ks[6], (L, D_MODEL), 0.02),
        'w_in': nrm(ks[7], (L, D_MODEL, IN_DIM), D_MODEL ** -0.5),
        'conv_w': nrm(ks[8], (L, CONV_W, CONV_DIM), CONV_W ** -0.5),
        'a_log': jnp.log(jax.random.uniform(ks[10], (L, GDN_HEADS), f32, 1.0, 16.0)),
        'dt_bias': dt + jnp.log(-jnp.expm1(-dt)),
        'gdn_norm': 1.0 + nrm(ks[11], (L, GDN_DV), 0.02),
        'ln_g': 1.0 + nrm(ks[12], (L, MLP_DIM), 0.02),
        'ln_b': nrm(ks[13], (L, MLP_DIM), 0.02),
        'w_s': nrm(ks[14], (L, MLP_GROUPS, MLP_CHUNK, MLP_CHUNK), MLP_CHUNK ** -0.5),
        'b_s': 1.0 + nrm(ks[15], (L, MLP_GROUPS, MLP_CHUNK), 0.02),
        'w_out': nrm(ks[16], (L, MIX_DIM, D_MODEL), MIX_DIM ** -0.5),
        'norm2': 1.0 + nrm(ks[17], (L, D_MODEL), 0.02),
        'w_q': nrm(ks[18], (L, D_MODEL, PEER_HEADS * PEER_QDIM), D_MODEL ** -0.5),
        'sub_keys': nrm(ks[19], (L, 2, PEER_HEADS, N_KEYS, PEER_QHALF), PEER_QHALF ** -0.5),
        'peer_u': nrm(ks[20], (L, N_EXPERTS, D_MODEL), D_MODEL ** -0.5),
        'peer_v': nrm(ks[21], (L, N_EXPERTS, D_MODEL), PEER_HEADS ** -0.5),
        'norm3': 1.0 + nrm(ks[22], (L, D_MODEL), 0.02),
        'w_ple_gate': nrm(ks[23], (L, D_MODEL, D_MODEL), D_MODEL ** -0.5),
        'w_ple_proj': nrm(ks[24], (L, PLE_DIM, D_MODEL), PLE_DIM ** -0.5),
        'final_norm': 1.0 + nrm(ks[25], (D_MODEL,), 0.02),
    }


def reference(x_prompt, x_sample, state_gdn, cache_conv, p_prompt, p_sample,
              norm1, w_in, conv_w, a_log, dt_bias, gdn_norm, ln_g, ln_b, w_s, b_s, w_out,
              norm2, w_q, sub_keys, peer_u, peer_v, norm3, w_ple_gate, w_ple_proj, final_norm):
    hp, hs = x_prompt, x_sample
    sp_l, bp_l, ss_l, bs_l, vs_l = [], [], [], [], []
    for i in range(DEPTH):
        prm = (norm1[i], w_in[i], conv_w[i], a_log[i], dt_bias[i], gdn_norm[i], ln_g[i], ln_b[i],
               w_s[i], b_s[i], w_out[i], norm2[i], w_q[i], sub_keys[i], peer_u[i], peer_v[i],
               norm3[i], w_ple_gate[i], w_ple_proj[i])
        s0p = jnp.zeros((hp.shape[0], GDN_HEADS, GDN_DK, GDN_DV), jnp.float32)
        b0p = jnp.zeros((hp.shape[0], CONV_W - 1, CONV_DIM), hp.dtype)
        hp, s_p, b_p, _ = _layer(hp, p_prompt[i], s0p, b0p, *prm)
        hs, s_s, b_s_new, v_s = _layer(hs, p_sample[i], state_gdn[i], cache_conv[i], *prm)
        sp_l.append(s_p)
        bp_l.append(b_p)
        ss_l.append(s_s)
        bs_l.append(b_s_new)
        vs_l.append(v_s)
    y_prompt = _rmsnorm(hp, final_norm)
    y_sample = _rmsnorm(hs, final_norm)
    return (y_prompt, y_sample, jnp.stack(sp_l), jnp.stack(bp_l), jnp.stack(ss_l), jnp.stack(bs_l), jnp.stack(vs_l))
```

```python
import functools
import math

import jax
import jax.numpy as jnp
from jax import lax
from jax.experimental import pallas as pl
from jax.experimental.pallas import tpu as pltpu

F32 = jnp.float32
BF16 = jnp.bfloat16

EPS = 1e-6
GDN_CHUNK = 64
GDN_HEADS = 8
GDN_DK = 128
GDN_DV = 128
CONV_W = 4
MLP_CHUNK = 128
MLP_GROUPS = 8
MLP_GROUP_DIM = 128
PEER_HEADS = 8
PEER_QHALF = 128
N_KEYS = 128
PEER_TOPK = 16

LANES = 128
SUBLANES = 8
VMEM_LIMIT_BYTES = 56 * 1024 * 1024


def _cparams(*sem):
    return pltpu.CompilerParams(dimension_semantics=sem, vmem_limit_bytes=VMEM_LIMIT_BYTES)


def _split3(x):
    hi = x.astype(BF16)
    r1 = x - hi.astype(F32)
    mid = r1.astype(BF16)
    lo = (r1 - mid.astype(F32)).astype(BF16)
    return hi, mid, lo


def _dg(a, b, dims):
    return lax.dot_general(a, b, (dims, ((), ())), preferred_element_type=F32)


_NN = ((1,), (0,))
_NT = ((1,), (1,))
_TN = ((0,), (0,))


def _dot1(a, b, dims=_NN):
    return _dg(a.astype(BF16), b.astype(BF16), dims)


def _dot3(a, b, dims=_NN):
    ah = a.astype(BF16)
    al = (a - ah.astype(F32)).astype(BF16)
    bh = b.astype(BF16)
    bl = (b - bh.astype(F32)).astype(BF16)
    return _dg(ah, bh, dims) + (_dg(ah, bl, dims) + _dg(al, bh, dims))


def _dot_exact_lhs(a_exact_bf16, b, dims=_NN):
    b0, b1, b2 = _split3(b)
    return _dg(a_exact_bf16, b0, dims) + (_dg(a_exact_bf16, b1, dims) + _dg(a_exact_bf16, b2, dims))


def _dot_exact_rhs(a, b_exact_bf16, dims=_NN):
    a0, a1, a2 = _split3(a)
    return _dg(a0, b_exact_bf16, dims) + (_dg(a1, b_exact_bf16, dims) + _dg(a2, b_exact_bf16, dims))


def _rms_rows(x, gain):
    ms = jnp.mean(x * x, axis=-1, keepdims=True)
    return x * lax.rsqrt(ms + EPS) * gain


def _sigmoid(x):
    return 1.0 / (1.0 + jnp.exp(-x))


def _silu(x):
    return x * _sigmoid(x)


def _gelu(x):
    return 0.5 * x * (1.0 + lax.erf(x * (1.0 / math.sqrt(2.0))))


def _softplus(x):
    return jnp.maximum(x, 0.0) + jnp.log1p(jnp.exp(-jnp.abs(x)))


def _norm_matmul_kernel(x_ref, g_ref, *refs, passes, emit_xn):
    if passes == 1:
        w_refs, rest = refs[:1], refs[1:]
    else:
        w_refs, rest = refs[:2], refs[2:]
    if emit_xn:
        o_ref, xn_out_ref = rest[0], rest[1]
        scr = rest[2:]
    else:
        o_ref = rest[0]
        scr = rest[1:]
    j = pl.program_id(1)

    @pl.when(j == 0)
    def _():
        xn = _rms_rows(x_ref[...], g_ref[...])
        if emit_xn:
            xn_out_ref[...] = xn
        hi = xn.astype(BF16)
        scr[0][...] = hi
        if passes == 3:
            scr[1][...] = (xn - hi.astype(F32)).astype(BF16)

    if passes == 1:
        o_ref[...] = _dg(scr[0][...], w_refs[0][...], _NN)
    else:
        xh = scr[0][...]
        wh = w_refs[0][...]
        o_ref[...] = _dg(xh, wh, _NN) + (_dg(xh, w_refs[1][...], _NN) + _dg(scr[1][...], wh, _NN))


def _norm_matmul(x, gain, w, *, tm, tn, passes=1, emit_xn=False, name):
    n, d = x.shape
    m = w.shape[1]
    assert n % tm == 0 and m % tn == 0
    wh = w.astype(BF16)
    ws = [wh] if passes == 1 else [wh, (w - wh.astype(F32)).astype(BF16)]
    w_specs = [pl.BlockSpec((d, tn), lambda i, j: (0, j)) for _ in ws]
    out_shape = [jax.ShapeDtypeStruct((n, m), F32)]
    out_specs = [pl.BlockSpec((tm, tn), lambda i, j: (i, j))]
    if emit_xn:
        out_shape.append(jax.ShapeDtypeStruct((n, d), F32))
        out_specs.append(pl.BlockSpec((tm, d), lambda i, j: (i, 0)))
    scratch = [pltpu.VMEM((tm, d), BF16) for _ in range(1 if passes == 1 else 2)]
    res = pl.pallas_call(
        functools.partial(_norm_matmul_kernel, passes=passes, emit_xn=emit_xn),
        grid=(n // tm, m // tn),
        in_specs=[pl.BlockSpec((tm, d), lambda i, j: (i, 0)),
                  pl.BlockSpec((1, d), lambda i, j: (0, 0))] + w_specs,
        out_specs=out_specs,
        out_shape=out_shape,
        scratch_shapes=scratch,
        compiler_params=_cparams("parallel", "arbitrary"),
        name=name,
    )(x, gain.reshape(1, d), *ws)
    return res if emit_xn else res[0]


def _conv_kernel(x_ref, buf_ref, w_ref, q_ref, k_ref, v_ref, xp_ref, *, tt):
    t = pl.program_id(1)
    halo = SUBLANES

    @pl.when(t == 0)
    def _():
        xp_ref[0:halo, :] = buf_ref[0]

    @pl.when(t > 0)
    def _():
        xp_ref[0:halo, :] = xp_ref[tt:tt + halo, :]

    xp_ref[halo:halo + tt, :] = x_ref[0]
    base = halo - (CONV_W - 1)
    y = xp_ref[base:base + tt, :] * w_ref[0:1, :]
    for j in range(1, CONV_W):
        y = y + xp_ref[base + j:base + j + tt, :] * w_ref[j:j + 1, :]
    y = _silu(y)
    qk_dim = GDN_HEADS * GDN_DK
    for h in range(GDN_HEADS):
        qh = y[:, h * GDN_DK:(h + 1) * GDN_DK]
        qn = qh * lax.rsqrt(jnp.sum(qh * qh, axis=-1, keepdims=True) + EPS)
        q_ref[0, :, h * GDN_DK:(h + 1) * GDN_DK] = qn * (GDN_DK ** -0.5)
        kh = y[:, qk_dim + h * GDN_DK:qk_dim + (h + 1) * GDN_DK]
        k_ref[0, :, h * GDN_DK:(h + 1) * GDN_DK] = kh * lax.rsqrt(jnp.sum(kh * kh, axis=-1, keepdims=True) + EPS)
    v_ref[0] = y[:, 2 * qk_dim:]


def _conv_qkv(proj3, buf8, conv_w, *, tt):
    b, t, _ = proj3.shape
    cdim = conv_w.shape[1]
    hd = GDN_HEADS * GDN_DK
    assert t % tt == 0
    shp = jax.ShapeDtypeStruct((b, t, hd), F32)
    ospec = pl.BlockSpec((1, tt, hd), lambda i, j: (i, j, 0))
    return pl.pallas_call(
        functools.partial(_conv_kernel, tt=tt),
        grid=(b, t // tt),
        in_specs=[pl.BlockSpec((1, tt, cdim), lambda i, j: (i, j, 0)),
                  pl.BlockSpec((1, SUBLANES, cdim), lambda i, j: (i, 0, 0)),
                  pl.BlockSpec((CONV_W, cdim), lambda i, j: (0, 0))],
        out_specs=[ospec, ospec, ospec],
        out_shape=[shp, shp, shp],
        scratch_shapes=[pltpu.VMEM((tt + 2 * SUBLANES, cdim), F32)],
        compiler_params=_cparams("parallel", "arbitrary"),
        name="conv_qkv",
    )(proj3, buf8, conv_w)


def _gdn_kernel(q_ref, k_ref, v_ref, ab_ref, abt_ref, prm_row_ref, prm_col_ref, s0_ref,
                o_ref, sout_ref, state_ref, *, tt, t_valid):
    tstep = pl.program_id(1)
    c = GDN_CHUNK

    @pl.when(tstep == 0)
    def _():
        state_ref[...] = s0_ref[0]

    ii = lax.broadcasted_iota(jnp.int32, (c, c), 0)
    jj = lax.broadcasted_iota(jnp.int32, (c, c), 1)
    incl = ii >= jj
    strict = ii > jj
    tri = incl.astype(BF16)
    tri_t = (jj >= ii).astype(BF16)
    eye = (ii == jj).astype(F32)

    alog_row = prm_row_ref[0:1, :]
    dtb_row = prm_row_ref[1:2, :]
    alog_col = prm_col_ref[:, 0:1]
    dtb_col = prm_col_ref[:, 1:2]

    def chunk_body(ci, carry):
        r0 = pl.multiple_of(ci * c, c)
        tpos = tstep * tt + r0
        a_blk = ab_ref[0, pl.ds(r0, c), :]
        valid_col = (tpos + lax.broadcasted_iota(jnp.int32, (c, LANES), 0)) < t_valid
        g_col = jnp.where(valid_col, -jnp.exp(alog_row) * _softplus(a_blk + dtb_row), 0.0)
        beta_col = jnp.where(valid_col, _sigmoid(a_blk), 0.0)
        gc_col = _dot_exact_lhs(tri, g_col)
        at_blk = abt_ref[0, ci]
        valid_row = (tpos + lax.broadcasted_iota(jnp.int32, (2 * GDN_HEADS, c), 1)) < t_valid
        g_row = jnp.where(valid_row, -jnp.exp(alog_col) * _softplus(at_blk + dtb_col), 0.0)
        gc_row = _dot_exact_rhs(g_row, tri_t)

        for h in range(GDN_HEADS):
            sl = slice(h * GDN_DK, (h + 1) * GDN_DK)
            qh = q_ref[0, pl.ds(r0, c), sl]
            kh = k_ref[0, pl.ds(r0, c), sl]
            vh = v_ref[0, pl.ds(r0, c), sl]
            gc = gc_col[:, h:h + 1]
            beta = beta_col[:, GDN_HEADS + h:GDN_HEADS + h + 1]
            gcr = gc_row[h:h + 1, :]
            gc_last = gc_col[c - 1:c, h:h + 1]
            diff = gc - gcr
            decay = jnp.where(incl, jnp.exp(jnp.where(incl, diff, 0.0)), 0.0)
            egc = jnp.exp(gc)
            kb = kh * beta
            lmat = jnp.where(strict, _dot1(kb, kh, _NT) * decay, 0.0)
            tinv = eye - lmat
            pw = lmat
            for _ in range(5):
                pw = _dot3(pw, pw)
                tinv = tinv + _dot3(tinv, pw)
            rhs = jnp.concatenate([vh * beta, kb * egc], axis=-1)
            sol = _dot3(tinv, rhs)
            u_blk = sol[:, :GDN_DV]
            w_blk = sol[:, GDN_DV:]
            attn = _dot1(qh, kh, _NT) * decay
            q_dec = qh * egc
            k_dec = kh * jnp.exp(gc_last - gc)
            s = state_ref[h]
            v_new = u_blk - _dot1(w_blk, s)
            o = _dot1(q_dec, s) + _dot1(attn, v_new)
            state_ref[h] = s * jnp.exp(gc_last) + _dot1(k_dec, v_new, _TN)
            o_ref[0, pl.ds(r0, c), sl] = o
        return carry

    lax.fori_loop(0, tt // c, chunk_body, 0)

    @pl.when(tstep == pl.num_programs(1) - 1)
    def _():
        sout_ref[0] = state_ref[...]


def _gdn(q, k, v, ab, abt, a_log, dt_bias, s0, *, tt, t_valid):
    b, t, hd = q.shape
    assert t % tt == 0 and tt % GDN_CHUNK == 0
    prm_row = jnp.zeros((SUBLANES, LANES), F32)
    prm_row = prm_row.at[0, :GDN_HEADS].set(a_log).at[1, :GDN_HEADS].set(dt_bias)
    prm_col = jnp.zeros((2 * GDN_HEADS, LANES), F32)
    prm_col = prm_col.at[:GDN_HEADS, 0].set(a_log).at[:GDN_HEADS, 1].set(dt_bias)
    seq = pl.BlockSpec((1, tt, hd), lambda i, j: (i, j, 0))
    st = pl.BlockSpec((1, GDN_HEADS, GDN_DK, GDN_DV), lambda i, j: (i, 0, 0, 0))
    return pl.pallas_call(
        functools.partial(_gdn_kernel, tt=tt, t_valid=t_valid),
        grid=(b, t // tt),
        in_specs=[seq, seq, seq,
                  pl.BlockSpec((1, tt, LANES), lambda i, j: (i, j, 0)),
                  pl.BlockSpec((1, tt // GDN_CHUNK, 2 * GDN_HEADS, GDN_CHUNK), lambda i, j: (i, j, 0, 0)),
                  pl.BlockSpec((SUBLANES, LANES), lambda i, j: (0, 0)),
                  pl.BlockSpec((2 * GDN_HEADS, LANES), lambda i, j: (0, 0)),
                  st],
        out_specs=[seq, st],
        out_shape=[jax.ShapeDtypeStruct((b, t, hd), F32),
                   jax.ShapeDtypeStruct((b, GDN_HEADS, GDN_DK, GDN_DV), F32)],
        scratch_shapes=[pltpu.VMEM((GDN_HEADS, GDN_DK, GDN_DV), F32)],
        compiler_params=_cparams("parallel", "arbitrary"),
        name="gdn_core",
    )(q, k, v, ab, abt, prm_row, prm_col, s0)


def _mlp_kernel(uv_ref, lng_ref, lnb_ref, ws_ref, bst_ref, y_ref, vn_ref, *, rows, lc):
    mdim = MLP_GROUPS * MLP_GROUP_DIM
    ii = lax.broadcasted_iota(jnp.int32, (lc, lc), 0)
    jj = lax.broadcasted_iota(jnp.int32, (lc, lc), 1)
    keep = ii >= jj
    for r in range(rows // lc):
        rs = slice(r * lc, (r + 1) * lc)
        g = _gelu(uv_ref[0, rs, :])
        u = g[:, :mdim]
        vv = g[:, mdim:]
        mu = jnp.mean(vv, axis=-1, keepdims=True)
        xc = vv - mu
        vn = xc * lax.rsqrt(jnp.mean(xc * xc, axis=-1, keepdims=True) + EPS) * lng_ref[...] + lnb_ref[...]
        vn_ref[0, rs, :] = vn
        for gi in range(MLP_GROUPS):
            gs = slice(gi * MLP_GROUP_DIM, (gi + 1) * MLP_GROUP_DIM)
            wm = jnp.where(keep, ws_ref[gi], 0.0)
            s = _dot1(wm, vn[:, gs]) + bst_ref[:, gi:gi + 1]
            y_ref[0, rs, gs] = u[:, gs] * s


def _mlp_branch(proj3, col_block, ln_g, ln_b, w_s, b_s, *, rows, lc):
    b, t, _ = proj3.shape
    mdim = MLP_GROUPS * MLP_GROUP_DIM
    assert t % rows == 0 and rows % lc == 0
    ws = w_s[:, :lc, :lc]
    bst = jnp.transpose(b_s[:, :lc])
    shp = jax.ShapeDtypeStruct((b, t, mdim), F32)
    ospec = pl.BlockSpec((1, rows, mdim), lambda i, j: (i, j, 0))
    return pl.pallas_call(
        functools.partial(_mlp_kernel, rows=rows, lc=lc),
        grid=(b, t // rows),
        in_specs=[pl.BlockSpec((1, rows, 2 * mdim), lambda i, j: (i, j, col_block)),
                  pl.BlockSpec((1, mdim), lambda i, j: (0, 0)),
                  pl.BlockSpec((1, mdim), lambda i, j: (0, 0)),
                  pl.BlockSpec((MLP_GROUPS, lc, lc), lambda i, j: (0, 0, 0)),
                  pl.BlockSpec((lc, MLP_GROUPS), lambda i, j: (0, 0))],
        out_specs=[ospec, ospec],
        out_shape=[shp, shp],
        compiler_params=_cparams("parallel", "parallel"),
        name="mlp_branch",
    )(proj3, ln_g.reshape(1, mdim), ln_b.reshape(1, mdim), ws, bst)


def _outproj_kernel(o_ref, z_ref, y_ref, h_ref, gn_ref, w_ref, out_ref):
    parts = []
    for hh in range(GDN_HEADS):
        sl = slice(hh * GDN_DV, (hh + 1) * GDN_DV)
        parts.append((_rms_rows(o_ref[:, sl], gn_ref[...]) * _silu(z_ref[:, sl])).astype(BF16))
    parts.append(y_ref[...].astype(BF16))
    cat = jnp.concatenate(parts, axis=-1)
    out_ref[...] = h_ref[...] + _dg(cat, w_ref[...], _NN)


def _outproj(o2, proj2, z_block, y2, h2, gdn_norm, w_out, *, tm):
    n, d = h2.shape
    vd = GDN_HEADS * GDN_DV
    assert n % tm == 0
    return pl.pallas_call(
        _outproj_kernel,
        grid=(n // tm,),
        in_specs=[pl.BlockSpec((tm, vd), lambda i: (i, 0)),
                  pl.BlockSpec((tm, vd), lambda i: (i, z_block)),
                  pl.BlockSpec((tm, vd), lambda i: (i, 0)),
                  pl.BlockSpec((tm, d), lambda i: (i, 0)),
                  pl.BlockSpec((1, GDN_DV), lambda i: (0, 0)),
                  pl.BlockSpec(w_out.shape, lambda i: (0, 0))],
        out_specs=pl.BlockSpec((tm, d), lambda i: (i, 0)),
        out_shape=jax.ShapeDtypeStruct((n, d), F32),
        compiler_params=_cparams("parallel"),
        name="out_proj",
    )(o2, proj2, y2, h2, gdn_norm.reshape(1, GDN_DV), w_out.astype(BF16))


def _topk_axis0(x, k, payloads=()):
    r = x.shape[0]
    iota = lax.broadcasted_iota(jnp.int32, x.shape, 0)
    vals, idxs = [], []
    outs = [[] for _ in payloads]
    for _ in range(k):
        m = jnp.max(x, axis=0, keepdims=True)
        i = jnp.min(jnp.where(x == m, iota, r), axis=0, keepdims=True)
        hit = iota == i
        vals.append(m)
        idxs.append(i)
        for p, acc in zip(payloads, outs):
            acc.append(jnp.sum(jnp.where(hit, p, 0), axis=0, keepdims=True))
        x = jnp.where(hit, -jnp.inf, x)
    cat = lambda parts: jnp.concatenate(parts, axis=0)
    return cat(vals), cat(idxs), [cat(acc) for acc in outs]


def _peer_topk_kernel(q_ref, keys_ref, eidx_ref, gate_ref):
    k = PEER_TOPK

    def head_body(h, carry):
        sv, si = [], []
        for s in range(2):
            c0 = pl.multiple_of(h * (2 * PEER_QHALF) + s * PEER_QHALF, PEER_QHALF)
            qhs = q_ref[:, pl.ds(c0, PEER_QHALF)]
            sc_t = _dot3(keys_ref[s, h], qhs, _NT)
            v, i, _ = _topk_axis0(sc_t, k)
            sv.append(v)
            si.append(i)
        cand = jnp.concatenate([sv[0][a:a + 1, :] + sv[1] for a in range(k)], axis=0)
        cidx = jnp.concatenate([si[0][a:a + 1, :] * N_KEYS + si[1] for a in range(k)], axis=0)
        fv, _, (fe,) = _topk_axis0(cand, k, (cidx,))
        e = jnp.exp(fv - fv[0:1, :])
        gate = e / jnp.sum(e, axis=0, keepdims=True)
        r0 = pl.multiple_of(h * k, k)
        eidx_ref[pl.ds(r0, k), :] = fe
        gate_ref[pl.ds(r0, k), :] = gate
        return carry

    lax.fori_loop(0, PEER_HEADS, head_body, 0)


def _peer_topk(q, sub_keys, *, tk):
    n, qd = q.shape
    assert n % tk == 0
    rows = PEER_HEADS * PEER_TOPK
    return pl.pallas_call(
        _peer_topk_kernel,
        grid=(n // tk,),
        in_specs=[pl.BlockSpec((tk, qd), lambda i: (i, 0)),
                  pl.BlockSpec(sub_keys.shape, lambda i: (0, 0, 0, 0))],
        out_specs=[pl.BlockSpec((rows, tk), lambda i: (0, i)),
                   pl.BlockSpec((rows, tk), lambda i: (0, i))],
        out_shape=[jax.ShapeDtypeStruct((rows, n), jnp.int32),
                   jax.ShapeDtypeStruct((rows, n), F32)],
        compiler_params=_cparams("parallel"),
        name="peer_topk",
    )(q, sub_keys)


def _peer_gather_kernel(idx_ref, x_ref, gate_ref, h_ref, tab_ref, out_ref, buf_ref, sem_ref, *, tb):
    rows = PEER_HEADS * PEER_TOPK

    def row_copy(tok, r, slot):
        e = idx_ref[tok * rows + r]
        return pltpu.make_async_copy(tab_ref.at[pl.ds(e, 1), :], buf_ref.at[slot, pl.ds(r, 1), :], sem_ref.at[slot])

    def start_token(tok, slot):
        for r in range(rows):
            row_copy(tok, r, slot).start()

    def wait_token(slot):
        pltpu.make_async_copy(tab_ref.at[pl.ds(0, rows), :], buf_ref.at[slot], sem_ref.at[slot]).wait()

    start_token(0, 0)
    lane_tok = lax.broadcasted_iota(jnp.int32, (rows, tb), 1)

    def tok_body(t, carry):
        slot = t % 2

        @pl.when(t + 1 < tb)
        def _():
            start_token(t + 1, 1 - slot)

        wait_token(slot)
        packed = buf_ref[slot]
        u = lax.bitcast_convert_type(packed << 16, F32)
        v = lax.bitcast_convert_type(packed & jnp.uint32(0xFFFF0000), F32)
        xt = x_ref[pl.ds(t, 1), :]
        act = jnp.sum(u * xt, axis=-1, keepdims=True)
        gcol = jnp.sum(jnp.where(lane_tok == t, gate_ref[0], 0.0), axis=-1, keepdims=True)
        coef = gcol * _gelu(act)
        out_ref[pl.ds(t, 1), :] = h_ref[pl.ds(t, 1), :] + jnp.sum(v * coef, axis=0, keepdims=True)
        return carry

    lax.fori_loop(0, tb, tok_body, 0)


def _peer_gather(idx_flat, xn, gate_blocks, h, table, *, tb):
    n, d = h.shape
    rows = PEER_HEADS * PEER_TOPK
    assert n % tb == 0
    return pl.pallas_call(
        functools.partial(_peer_gather_kernel, tb=tb),
        grid=(n // tb,),
        in_specs=[pl.BlockSpec((tb * rows,), lambda i: (i,), memory_space=pltpu.SMEM),
                  pl.BlockSpec((tb, d), lambda i: (i, 0)),
                  pl.BlockSpec((1, rows, tb), lambda i: (i, 0, 0)),
                  pl.BlockSpec((tb, d), lambda i: (i, 0)),
                  pl.BlockSpec(memory_space=pl.ANY)],
        out_specs=pl.BlockSpec((tb, d), lambda i: (i, 0)),
        out_shape=jax.ShapeDtypeStruct((n, d), F32),
        scratch_shapes=[pltpu.VMEM((2, rows, d), jnp.uint32),
                        pltpu.SemaphoreType.DMA((2,))],
        compiler_params=_cparams("arbitrary"),
        name="peer_gather",
    )(idx_flat, xn, gate_blocks, h, table)


def _ple_kernel(h_ref, p_ref, n3_ref, fn_ref, wg_ref, wp_ref, y_ref):
    h = h_ref[...]
    gate = _sigmoid(_dot1(_rms_rows(h, n3_ref[...]), wg_ref[...]))
    h = h + gate * _dot1(p_ref[...], wp_ref[...])
    y_ref[...] = _rms_rows(h, fn_ref[...])


def _ple_final(h, p, norm3, final_norm, w_gate, w_proj, *, tm):
    n, d = h.shape
    pd = p.shape[1]
    assert n % tm == 0
    return pl.pallas_call(
        _ple_kernel,
        grid=(n // tm,),
        in_specs=[pl.BlockSpec((tm, d), lambda i: (i, 0)),
                  pl.BlockSpec((tm, pd), lambda i: (i, 0)),
                  pl.BlockSpec((1, d), lambda i: (0, 0)),
                  pl.BlockSpec((1, d), lambda i: (0, 0)),
                  pl.BlockSpec((d, d), lambda i: (0, 0)),
                  pl.BlockSpec((pd, d), lambda i: (0, 0))],
        out_specs=pl.BlockSpec((tm, d), lambda i: (i, 0)),
        out_shape=jax.ShapeDtypeStruct((n, d), F32),
        compiler_params=_cparams("parallel"),
        name="ple_final",
    )(h, p, norm3.reshape(1, d), final_norm.reshape(1, d), w_gate.astype(BF16), w_proj.astype(BF16))


QK_DIM = GDN_HEADS * GDN_DK
V_DIM = GDN_HEADS * GDN_DV
CONV_DIM = 2 * QK_DIM + V_DIM
MLP_DIM = MLP_GROUPS * MLP_GROUP_DIM
Z_COL = CONV_DIM
UV_COL = Z_COL + V_DIM
AB_COL = UV_COL + 2 * MLP_DIM
PROJ_COLS = AB_COL + LANES
PROJ_TN = 896


def _relayout_w_in(w_in):
    c0 = CONV_DIM
    c2 = c0 + 2 * GDN_HEADS
    c3 = c2 + V_DIM
    ab = jnp.pad(w_in[:, c0:c2], ((0, 0), (0, LANES - 2 * GDN_HEADS)))
    return jnp.concatenate([w_in[:, :c0], w_in[:, c2:c3], w_in[:, c3:], ab], axis=1)


def _pack_tables(peer_u, peer_v):
    ub = lax.bitcast_convert_type(peer_u.astype(BF16), jnp.uint16).astype(jnp.uint32)
    vb = lax.bitcast_convert_type(peer_v.astype(BF16), jnp.uint16).astype(jnp.uint32)
    return ub | (vb << 16)


def _row_tile(n, want):
    t = min(n, want)
    assert n % t == 0
    return t


def _layer(x, p, s0, buf, w, final_norm):
    b, t, d = x.shape
    n = b * t
    x2 = x.reshape(n, d)
    tm = _row_tile(n, 512)

    proj = _norm_matmul(x2, w["norm1"], w["w_in"], tm=tm, tn=PROJ_TN, name="in_proj")
    proj3 = proj.reshape(b, t, PROJ_COLS)
    new_buf = jnp.concatenate([buf, proj3[:, :, :CONV_DIM]], axis=1)[:, t:]

    buf8 = jnp.pad(buf, ((0, 0), (SUBLANES - (CONV_W - 1), 0), (0, 0)))
    q, k, v = _conv_qkv(proj3, buf8, w["conv_w"], tt=_row_tile(t, 256))

    tg = -(-t // GDN_CHUNK) * GDN_CHUNK
    ab3 = proj3[:, :, AB_COL:]
    if tg != t:
        padt = ((0, 0), (0, tg - t), (0, 0))
        q, k, v, ab3 = (jnp.pad(a, padt) for a in (q, k, v, ab3))
    abt = jnp.transpose(ab3[:, :, :2 * GDN_HEADS].reshape(b, tg // GDN_CHUNK, GDN_CHUNK, 2 * GDN_HEADS),
                        (0, 1, 3, 2))
    o, s_new = _gdn(q, k, v, ab3, abt, w["a_log"], w["dt_bias"], s0, tt=_row_tile(tg, 512), t_valid=t)
    o = o[:, :t]

    lc = min(t, MLP_CHUNK)
    y_b, vn = _mlp_branch(proj3, UV_COL // (2 * MLP_DIM), w["ln_g"], w["ln_b"], w["w_s"], w["b_s"],
                          rows=_row_tile(t, 512), lc=lc)

    h1 = _outproj(o.reshape(n, V_DIM), proj, Z_COL // V_DIM, y_b.reshape(n, MLP_DIM), x2,
                  w["gdn_norm"], w["w_out"], tm=_row_tile(n, 256))

    qp, xn2 = _norm_matmul(h1, w["norm2"], w["w_q"], tm=tm, tn=512, passes=3, emit_xn=True, name="peer_query")
    tb = 128
    eidx_t, gate_t = _peer_topk(qp, w["sub_keys"], tk=tb)
    rows = PEER_HEADS * PEER_TOPK
    idx_flat = jnp.transpose(eidx_t).reshape(n * rows)
    gate_blocks = jnp.transpose(gate_t.reshape(rows, n // tb, tb), (1, 0, 2))
    h2 = _peer_gather(idx_flat, xn2, gate_blocks, h1, w["table"], tb=tb)

    y = _ple_final(h2, p.reshape(n, -1), w["norm3"], final_norm, w["w_ple_gate"], w["w_ple_proj"],
                   tm=_row_tile(n, 256))
    return y.reshape(b, t, d), s_new, new_buf, vn


def kernel(x_prompt, x_sample, state_gdn, cache_conv, p_prompt, p_sample, norm1, w_in, conv_w, a_log, dt_bias,
           gdn_norm, ln_g, ln_b, w_s, b_s, w_out, norm2, w_q, sub_keys, peer_u, peer_v, norm3, w_ple_gate,
           w_ple_proj, final_norm):
    assert norm1.shape[0] == 1, "single layer"
    w = dict(norm1=norm1[0], w_in=_relayout_w_in(w_in[0]), conv_w=conv_w[0], a_log=a_log[0], dt_bias=dt_bias[0],
             gdn_norm=gdn_norm[0], ln_g=ln_g[0], ln_b=ln_b[0], w_s=w_s[0], b_s=b_s[0], w_out=w_out[0],
             norm2=norm2[0], w_q=w_q[0], sub_keys=sub_keys[0], table=_pack_tables(peer_u[0], peer_v[0]),
             norm3=norm3[0], w_ple_gate=w_ple_gate[0], w_ple_proj=w_ple_proj[0])
    bp = x_prompt.shape[0]
    s0p = jnp.zeros((bp, GDN_HEADS, GDN_DK, GDN_DV), F32)
    b0p = jnp.zeros((bp, CONV_W - 1, CONV_DIM), F32)
    y_s, s_s, b_s_new, v_s = _layer(x_sample, p_sample[0], state_gdn[0], cache_conv[0], w, final_norm)
    y_p, s_p, b_p, _ = _layer(x_prompt, p_prompt[0], s0p, b0p, w, final_norm)
    return (y_p, y_s, s_p[None], b_p[None], s_s[None], b_s_new[None], v_s[None])
```

```python
import functools
import math

import jax
import jax.numpy as jnp
from jax import lax
from jax.experimental import pallas as pl
from jax.experimental.pallas import tpu as pltpu

F32 = jnp.float32
BF16 = jnp.bfloat16

EPS = 1e-6
GDN_CHUNK = 64
GDN_HEADS = 8
GDN_HEAD_GROUP = 8
GDN_DK = 128
GDN_DV = 128
CONV_W = 4
MLP_CHUNK = 128
MLP_GROUPS = 8
MLP_GROUP_DIM = 128
PEER_HEADS = 8
PEER_QHALF = 128
N_KEYS = 128
PEER_TOPK = 16

LANES = 128
SUBLANES = 8
VMEM_LIMIT_BYTES = 56 * 1024 * 1024


def _cparams(*sem):
    return pltpu.CompilerParams(dimension_semantics=sem, vmem_limit_bytes=VMEM_LIMIT_BYTES)


def _split3(x):
    hi = x.astype(BF16)
    r1 = x - hi.astype(F32)
    mid = r1.astype(BF16)
    lo = (r1 - mid.astype(F32)).astype(BF16)
    return hi, mid, lo


def _dg(a, b, dims):
    return lax.dot_general(a, b, (dims, ((), ())), preferred_element_type=F32)


_NN = ((1,), (0,))
_NT = ((1,), (1,))
_TN = ((0,), (0,))


def _dot1(a, b, dims=_NN):
    return _dg(a.astype(BF16), b.astype(BF16), dims)


def _dot3(a, b, dims=_NN):
    ah = a.astype(BF16)
    al = (a - ah.astype(F32)).astype(BF16)
    bh = b.astype(BF16)
    bl = (b - bh.astype(F32)).astype(BF16)
    return _dg(ah, bh, dims) + (_dg(ah, bl, dims) + _dg(al, bh, dims))


def _dot_exact_lhs(a_exact_bf16, b, dims=_NN):
    b0, b1, b2 = _split3(b)
    return _dg(a_exact_bf16, b0, dims) + (_dg(a_exact_bf16, b1, dims) + _dg(a_exact_bf16, b2, dims))


def _dot_exact_rhs(a, b_exact_bf16, dims=_NN):
    a0, a1, a2 = _split3(a)
    return _dg(a0, b_exact_bf16, dims) + (_dg(a1, b_exact_bf16, dims) + _dg(a2, b_exact_bf16, dims))


def _rms_rows(x, gain):
    ms = jnp.mean(x * x, axis=-1, keepdims=True)
    return x * lax.rsqrt(ms + EPS) * gain


def _sigmoid(x):
    return 1.0 / (1.0 + jnp.exp(-x))


def _silu(x):
    return x * _sigmoid(x)


def _gelu(x):
    return 0.5 * x * (1.0 + lax.erf(x * (1.0 / math.sqrt(2.0))))


def _softplus(x):
    return jnp.maximum(x, 0.0) + jnp.log1p(jnp.exp(-jnp.abs(x)))


def _norm_matmul_kernel(x_ref, g_ref, *refs, passes, emit_xn):
    if passes == 1:
        w_refs, rest = refs[:1], refs[1:]
    else:
        w_refs, rest = refs[:2], refs[2:]
    if emit_xn:
        o_ref, xn_out_ref = rest[0], rest[1]
        scr = rest[2:]
    else:
        o_ref = rest[0]
        scr = rest[1:]
    j = pl.program_id(1)

    @pl.when(j == 0)
    def _():
        xn = _rms_rows(x_ref[...], g_ref[...])
        if emit_xn:
            xn_out_ref[...] = xn
        hi = xn.astype(BF16)
        scr[0][...] = hi
        if passes == 3:
            scr[1][...] = (xn - hi.astype(F32)).astype(BF16)

    if passes == 1:
        o_ref[...] = _dg(scr[0][...], w_refs[0][...], _NN)
    else:
        xh = scr[0][...]
        wh = w_refs[0][...]
        o_ref[...] = _dg(xh, wh, _NN) + (_dg(xh, w_refs[1][...], _NN) + _dg(scr[1][...], wh, _NN))


def _norm_matmul(x, gain, w, *, tm, tn, passes=1, emit_xn=False, name):
    n, d = x.shape
    m = w.shape[1]
    assert n % tm == 0 and m % tn == 0
    wh = w.astype(BF16)
    ws = [wh] if passes == 1 else [wh, (w - wh.astype(F32)).astype(BF16)]
    w_specs = [pl.BlockSpec((d, tn), lambda i, j: (0, j)) for _ in ws]
    out_shape = [jax.ShapeDtypeStruct((n, m), F32)]
    out_specs = [pl.BlockSpec((tm, tn), lambda i, j: (i, j))]
    if emit_xn:
        out_shape.append(jax.ShapeDtypeStruct((n, d), F32))
        out_specs.append(pl.BlockSpec((tm, d), lambda i, j: (i, 0)))
    scratch = [pltpu.VMEM((tm, d), BF16) for _ in range(1 if passes == 1 else 2)]
    res = pl.pallas_call(
        functools.partial(_norm_matmul_kernel, passes=passes, emit_xn=emit_xn),
        grid=(n // tm, m // tn),
        in_specs=[pl.BlockSpec((tm, d), lambda i, j: (i, 0)),
                  pl.BlockSpec((1, d), lambda i, j: (0, 0))] + w_specs,
        out_specs=out_specs,
        out_shape=out_shape,
        scratch_shapes=scratch,
        compiler_params=_cparams("parallel", "arbitrary"),
        name=name,
    )(x, gain.reshape(1, d), *ws)
    return res if emit_xn else res[0]


def _conv_kernel(x_ref, buf_ref, w_ref, q_ref, k_ref, v_ref, xp_ref, *, tt):
    t = pl.program_id(1)
    halo = SUBLANES

    @pl.when(t == 0)
    def _():
        xp_ref[0:halo, :] = buf_ref[0]

    @pl.when(t > 0)
    def _():
        xp_ref[0:halo, :] = xp_ref[tt:tt + halo, :]

    xp_ref[halo:halo + tt, :] = x_ref[0]
    base = halo - (CONV_W - 1)
    y = xp_ref[base:base + tt, :] * w_ref[0:1, :]
    for j in range(1, CONV_W):
        y = y + xp_ref[base + j:base + j + tt, :] * w_ref[j:j + 1, :]
    y = _silu(y)
    qk_dim = GDN_HEADS * GDN_DK
    for h in range(GDN_HEADS):
        qh = y[:, h * GDN_DK:(h + 1) * GDN_DK]
        qn = qh * lax.rsqrt(jnp.sum(qh * qh, axis=-1, keepdims=True) + EPS)
        q_ref[0, :, h * GDN_DK:(h + 1) * GDN_DK] = qn * (GDN_DK ** -0.5)
        kh = y[:, qk_dim + h * GDN_DK:qk_dim + (h + 1) * GDN_DK]
        k_ref[0, :, h * GDN_DK:(h + 1) * GDN_DK] = kh * lax.rsqrt(jnp.sum(kh * kh, axis=-1, keepdims=True) + EPS)
    v_ref[0] = y[:, 2 * qk_dim:]


def _conv_qkv(proj3, buf8, conv_w, *, tt):
    b, t, _ = proj3.shape
    cdim = conv_w.shape[1]
    hd = GDN_HEADS * GDN_DK
    assert t % tt == 0
    shp = jax.ShapeDtypeStruct((b, t, hd), F32)
    ospec = pl.BlockSpec((1, tt, hd), lambda i, j: (i, j, 0))
    return pl.pallas_call(
        functools.partial(_conv_kernel, tt=tt),
        grid=(b, t // tt),
        in_specs=[pl.BlockSpec((1, tt, cdim), lambda i, j: (i, j, 0)),
                  pl.BlockSpec((1, SUBLANES, cdim), lambda i, j: (i, 0, 0)),
                  pl.BlockSpec((CONV_W, cdim), lambda i, j: (0, 0))],
        out_specs=[ospec, ospec, ospec],
        out_shape=[shp, shp, shp],
        scratch_shapes=[pltpu.VMEM((tt + 2 * SUBLANES, cdim), F32)],
        compiler_params=_cparams("parallel", "arbitrary"),
        name="conv_qkv",
    )(proj3, buf8, conv_w)


def _gdn_kernel(q_ref, k_ref, v_ref, ab_ref, abt_ref, prm_row_ref, prm_col_ref, s0_ref,
                o_ref, sout_ref, state_ref, *, tt, t_valid):
    tstep = pl.program_id(1)
    c = GDN_CHUNK

    @pl.when(tstep == 0)
    def _():
        state_ref[...] = s0_ref[0]

    ii = lax.broadcasted_iota(jnp.int32, (c, c), 0)
    jj = lax.broadcasted_iota(jnp.int32, (c, c), 1)
    incl = ii >= jj
    strict = ii > jj
    tri = incl.astype(BF16)
    tri_t = (jj >= ii).astype(BF16)
    eye = (ii == jj).astype(F32)

    alog_row = prm_row_ref[0:1, :]
    dtb_row = prm_row_ref[1:2, :]
    alog_col = prm_col_ref[:, 0:1]
    dtb_col = prm_col_ref[:, 1:2]

    def chunk_body(ci, carry):
        r0 = pl.multiple_of(ci * c, c)
        tpos = tstep * tt + r0
        a_blk = ab_ref[0, pl.ds(r0, c), :]
        valid_col = (tpos + lax.broadcasted_iota(jnp.int32, (c, LANES), 0)) < t_valid
        g_col = jnp.where(valid_col, -jnp.exp(alog_row) * _softplus(a_blk + dtb_row), 0.0)
        beta_col = jnp.where(valid_col, _sigmoid(a_blk), 0.0)
        gc_col = _dot_exact_lhs(tri, g_col)
        at_blk = abt_ref[0, ci]
        valid_row = (tpos + lax.broadcasted_iota(jnp.int32, (2 * GDN_HEADS, c), 1)) < t_valid
        g_row = jnp.where(valid_row, -jnp.exp(alog_col) * _softplus(at_blk + dtb_col), 0.0)
        gc_row = _dot_exact_rhs(g_row, tri_t)

        for h0 in range(0, GDN_HEADS, GDN_HEAD_GROUP):
            heads = range(h0, h0 + GDN_HEAD_GROUP)
            st = {}
            for h in heads:
                sl = slice(h * GDN_DK, (h + 1) * GDN_DK)
                qh = q_ref[0, pl.ds(r0, c), sl]
                kh = k_ref[0, pl.ds(r0, c), sl]
                vh = v_ref[0, pl.ds(r0, c), sl]
                gc = gc_col[:, h:h + 1]
                beta = beta_col[:, GDN_HEADS + h:GDN_HEADS + h + 1]
                gcr = gc_row[h:h + 1, :]
                gc_last = gc_col[c - 1:c, h:h + 1]
                decay = jnp.where(incl, jnp.exp(jnp.where(incl, gc - gcr, 0.0)), 0.0)
                egc = jnp.exp(gc)
                kb = kh * beta
                lmat = jnp.where(strict, _dot1(kb, kh, _NT) * decay, 0.0)
                st[h] = dict(
                    sl=sl, tinv=eye - lmat, pw=lmat,
                    rhs=jnp.concatenate([vh * beta, kb * egc], axis=-1),
                    attn=_dot1(qh, kh, _NT) * decay, q_dec=qh * egc,
                    k_dec=kh * jnp.exp(gc_last - gc), bd=jnp.exp(gc_last))
            for _ in range(5):
                for h in heads:
                    st[h]["pw"] = _dot3(st[h]["pw"], st[h]["pw"])
                for h in heads:
                    st[h]["tinv"] = st[h]["tinv"] + _dot3(st[h]["tinv"], st[h]["pw"])
            for h in heads:
                st[h]["sol"] = _dot3(st[h]["tinv"], st[h]["rhs"])
            for h in heads:
                s = state_ref[h]
                st[h]["s"] = s
                st[h]["v_new"] = st[h]["sol"][:, :GDN_DV] - _dot1(st[h]["sol"][:, GDN_DV:], s)
            for h in heads:
                d = st[h]
                o_ref[0, pl.ds(r0, c), d["sl"]] = _dot1(d["q_dec"], d["s"]) + _dot1(d["attn"], d["v_new"])
                state_ref[h] = d["s"] * d["bd"] + _dot1(d["k_dec"], d["v_new"], _TN)
        return carry

    lax.fori_loop(0, tt // c, chunk_body, 0)

    @pl.when(tstep == pl.num_programs(1) - 1)
    def _():
        sout_ref[0] = state_ref[...]


def _gdn(q, k, v, ab, abt, a_log, dt_bias, s0, *, tt, t_valid):
    b, t, hd = q.shape
    assert t % tt == 0 and tt % GDN_CHUNK == 0
    prm_row = jnp.zeros((SUBLANES, LANES), F32)
    prm_row = prm_row.at[0, :GDN_HEADS].set(a_log).at[1, :GDN_HEADS].set(dt_bias)
    prm_col = jnp.zeros((2 * GDN_HEADS, LANES), F32)
    prm_col = prm_col.at[:GDN_HEADS, 0].set(a_log).at[:GDN_HEADS, 1].set(dt_bias)
    seq = pl.BlockSpec((1, tt, hd), lambda i, j: (i, j, 0))
    st = pl.BlockSpec((1, GDN_HEADS, GDN_DK, GDN_DV), lambda i, j: (i, 0, 0, 0))
    return pl.pallas_call(
        functools.partial(_gdn_kernel, tt=tt, t_valid=t_valid),
        grid=(b, t // tt),
        in_specs=[seq, seq, seq,
                  pl.BlockSpec((1, tt, LANES), lambda i, j: (i, j, 0)),
                  pl.BlockSpec((1, tt // GDN_CHUNK, 2 * GDN_HEADS, GDN_CHUNK), lambda i, j: (i, j, 0, 0)),
                  pl.BlockSpec((SUBLANES, LANES), lambda i, j: (0, 0)),
                  pl.BlockSpec((2 * GDN_HEADS, LANES), lambda i, j: (0, 0)),
                  st],
        out_specs=[seq, st],
        out_shape=[jax.ShapeDtypeStruct((b, t, hd), F32),
                   jax.ShapeDtypeStruct((b, GDN_HEADS, GDN_DK, GDN_DV), F32)],
        scratch_shapes=[pltpu.VMEM((GDN_HEADS, GDN_DK, GDN_DV), F32)],
        compiler_params=_cparams("parallel", "arbitrary"),
        name="gdn_core",
    )(q, k, v, ab, abt, prm_row, prm_col, s0)


def _mlp_kernel(uv_ref, lng_ref, lnb_ref, ws_ref, bst_ref, y_ref, vn_ref, *, rows, lc):
    mdim = MLP_GROUPS * MLP_GROUP_DIM
    ii = lax.broadcasted_iota(jnp.int32, (lc, lc), 0)
    jj = lax.broadcasted_iota(jnp.int32, (lc, lc), 1)
    keep = ii >= jj
    for r in range(rows // lc):
        rs = slice(r * lc, (r + 1) * lc)
        g = _gelu(uv_ref[0, rs, :])
        u = g[:, :mdim]
        vv = g[:, mdim:]
        mu = jnp.mean(vv, axis=-1, keepdims=True)
        xc = vv - mu
        vn = xc * lax.rsqrt(jnp.mean(xc * xc, axis=-1, keepdims=True) + EPS) * lng_ref[...] + lnb_ref[...]
        vn_ref[0, rs, :] = vn
        for gi in range(MLP_GROUPS):
            gs = slice(gi * MLP_GROUP_DIM, (gi + 1) * MLP_GROUP_DIM)
            wm = jnp.where(keep, ws_ref[gi], 0.0)
            s = _dot1(wm, vn[:, gs]) + bst_ref[:, gi:gi + 1]
            y_ref[0, rs, gs] = u[:, gs] * s


def _mlp_branch(proj3, col_block, ln_g, ln_b, w_s, b_s, *, rows, lc):
    b, t, _ = proj3.shape
    mdim = MLP_GROUPS * MLP_GROUP_DIM
    assert t % rows == 0 and rows % lc == 0
    ws = w_s[:, :lc, :lc]
    bst = jnp.transpose(b_s[:, :lc])
    shp = jax.ShapeDtypeStruct((b, t, mdim), F32)
    ospec = pl.BlockSpec((1, rows, mdim), lambda i, j: (i, j, 0))
    return pl.pallas_call(
        functools.partial(_mlp_kernel, rows=rows, lc=lc),
        grid=(b, t // rows),
        in_specs=[pl.BlockSpec((1, rows, 2 * mdim), lambda i, j: (i, j, col_block)),
                  pl.BlockSpec((1, mdim), lambda i, j: (0, 0)),
                  pl.BlockSpec((1, mdim), lambda i, j: (0, 0)),
                  pl.BlockSpec((MLP_GROUPS, lc, lc), lambda i, j: (0, 0, 0)),
                  pl.BlockSpec((lc, MLP_GROUPS), lambda i, j: (0, 0))],
        out_specs=[ospec, ospec],
        out_shape=[shp, shp],
        compiler_params=_cparams("parallel", "parallel"),
        name="mlp_branch",
    )(proj3, ln_g.reshape(1, mdim), ln_b.reshape(1, mdim), ws, bst)


def _outproj_kernel(o_ref, z_ref, y_ref, h_ref, gn_ref, w_ref, out_ref):
    parts = []
    for hh in range(GDN_HEADS):
        sl = slice(hh * GDN_DV, (hh + 1) * GDN_DV)
        parts.append((_rms_rows(o_ref[:, sl], gn_ref[...]) * _silu(z_ref[:, sl])).astype(BF16))
    parts.append(y_ref[...].astype(BF16))
    cat = jnp.concatenate(parts, axis=-1)
    out_ref[...] = h_ref[...] + _dg(cat, w_ref[...], _NN)


def _outproj(o2, proj2, z_block, y2, h2, gdn_norm, w_out, *, tm):
    n, d = h2.shape
    vd = GDN_HEADS * GDN_DV
    assert n % tm == 0
    return pl.pallas_call(
        _outproj_kernel,
        grid=(n // tm,),
        in_specs=[pl.BlockSpec((tm, vd), lambda i: (i, 0)),
                  pl.BlockSpec((tm, vd), lambda i: (i, z_block)),
                  pl.BlockSpec((tm, vd), lambda i: (i, 0)),
                  pl.BlockSpec((tm, d), lambda i: (i, 0)),
                  pl.BlockSpec((1, GDN_DV), lambda i: (0, 0)),
                  pl.BlockSpec(w_out.shape, lambda i: (0, 0))],
        out_specs=pl.BlockSpec((tm, d), lambda i: (i, 0)),
        out_shape=jax.ShapeDtypeStruct((n, d), F32),
        compiler_params=_cparams("parallel"),
        name="out_proj",
    )(o2, proj2, y2, h2, gdn_norm.reshape(1, GDN_DV), w_out.astype(BF16))


def _topk_axis0(x, k, payloads=()):
    r = x.shape[0]
    iota = lax.broadcasted_iota(jnp.int32, x.shape, 0)
    vals, idxs = [], []
    outs = [[] for _ in payloads]
    for _ in range(k):
        m = jnp.max(x, axis=0, keepdims=True)
        i = jnp.min(jnp.where(x == m, iota, r), axis=0, keepdims=True)
        hit = iota == i
        vals.append(m)
        idxs.append(i)
        for p, acc in zip(payloads, outs):
            acc.append(jnp.sum(jnp.where(hit, p, 0), axis=0, keepdims=True))
        x = jnp.where(hit, -jnp.inf, x)
    cat = lambda parts: jnp.concatenate(parts, axis=0)
    return cat(vals), cat(idxs), [cat(acc) for acc in outs]


def _peer_topk_kernel(q_ref, keys_ref, eidx_ref, gate_ref):
    k = PEER_TOPK

    def head_body(h, carry):
        sv, si = [], []
        for s in range(2):
            c0 = pl.multiple_of(h * (2 * PEER_QHALF) + s * PEER_QHALF, PEER_QHALF)
            qhs = q_ref[:, pl.ds(c0, PEER_QHALF)]
            sc_t = _dot3(keys_ref[s, h], qhs, _NT)
            v, i, _ = _topk_axis0(sc_t, k)
            sv.append(v)
            si.append(i)
        half = k // 2
        sub = lax.broadcasted_iota(jnp.int32, (half, sv[0].shape[1]), 0)
        cand_parts = [sv[0][0:1, :] + sv[1]]
        cidx_parts = [si[0][0:1, :] * N_KEYS + si[1]]
        for a in range(1, half):
            keep = sub < (k // (a + 1))
            cand_parts.append(jnp.where(keep, sv[0][a:a + 1, :] + sv[1][0:half, :], -jnp.inf))
            cidx_parts.append(si[0][a:a + 1, :] * N_KEYS + si[1][0:half, :])
        cand_parts.append(sv[0][half:, :] + sv[1][0:1, :])
        cidx_parts.append(si[0][half:, :] * N_KEYS + si[1][0:1, :])
        cand = jnp.concatenate(cand_parts, axis=0)
        cidx = jnp.concatenate(cidx_parts, axis=0)
        fv, _, (fe,) = _topk_axis0(cand, k, (cidx,))
        e = jnp.exp(fv - fv[0:1, :])
        gate = e / jnp.sum(e, axis=0, keepdims=True)
        r0 = pl.multiple_of(h * k, k)
        eidx_ref[pl.ds(r0, k), :] = fe
        gate_ref[pl.ds(r0, k), :] = gate
        return carry

    lax.fori_loop(0, PEER_HEADS, head_body, 0)


def _peer_topk(q, sub_keys, *, tk):
    n, qd = q.shape
    assert n % tk == 0
    rows = PEER_HEADS * PEER_TOPK
    return pl.pallas_call(
        _peer_topk_kernel,
        grid=(n // tk,),
        in_specs=[pl.BlockSpec((tk, qd), lambda i: (i, 0)),
                  pl.BlockSpec(sub_keys.shape, lambda i: (0, 0, 0, 0))],
        out_specs=[pl.BlockSpec((rows, tk), lambda i: (0, i)),
                   pl.BlockSpec((rows, tk), lambda i: (0, i))],
        out_shape=[jax.ShapeDtypeStruct((rows, n), jnp.int32),
                   jax.ShapeDtypeStruct((rows, n), F32)],
        compiler_params=_cparams("parallel"),
        name="peer_topk",
    )(q, sub_keys)


PEER_ROWS = PEER_HEADS * PEER_TOPK
PEER_SLOTS = 4
PEER_LOOK = PEER_SLOTS - 1
PEER_IDX_TAIL = 8
assert PEER_LOOK <= PEER_IDX_TAIL and (PEER_IDX_TAIL * PEER_ROWS) % 1024 == 0


def _peer_gather_kernel(idx_ref, x_ref, gate_ref, h_ref, tab_ref, out_ref, *scratch, tb):
    bufs, sem_ref = scratch[:PEER_SLOTS], scratch[PEER_SLOTS]
    rows = PEER_ROWS
    d = x_ref.shape[1]
    step = pl.program_id(0)

    def start_token(tok, slot):
        base = tok * rows
        for r in range(rows):
            e = idx_ref[base + r]
            pltpu.make_async_copy(tab_ref.at[pl.ds(e, 1), :], bufs[slot].at[pl.ds(r, 1), :],
                                  sem_ref.at[slot]).start(priority=r % 2)

    def wait_token(slot):
        pltpu.make_async_copy(tab_ref.at[pl.ds(0, rows), :], bufs[slot], sem_ref.at[slot]).wait()

    def evaluate(t, slot):
        xt = x_ref[pl.ds(t, 1), :]
        parts = []
        for g in range(rows // SUBLANES):
            w = bufs[slot][g * SUBLANES:(g + 1) * SUBLANES, :]
            p = lax.bitcast_convert_type(w << 16, F32) * xt
            q = p[:, 0:LANES]
            for c in range(1, d // LANES):
                q = q + p[:, c * LANES:(c + 1) * LANES]
            parts.append(q)
        act = jnp.sum(jnp.concatenate(parts, axis=0), axis=-1, keepdims=True)
        lane_tok = lax.broadcasted_iota(jnp.int32, (rows, tb), 1)
        gcol = jnp.sum(jnp.where(lane_tok == t, gate_ref[0], 0.0), axis=-1, keepdims=True)
        coef = gcol * _gelu(act)
        acc = jnp.zeros((SUBLANES, d), F32)
        for g in range(rows // SUBLANES):
            w = bufs[slot][g * SUBLANES:(g + 1) * SUBLANES, :]
            v = lax.bitcast_convert_type(w & jnp.uint32(0xFFFF0000), F32)
            acc = acc + v * coef[g * SUBLANES:(g + 1) * SUBLANES, :]
        out_ref[pl.ds(t, 1), :] = h_ref[pl.ds(t, 1), :] + jnp.sum(acc, axis=0, keepdims=True)

    @pl.when(step == 0)
    def _():
        for s in range(PEER_LOOK):
            start_token(s, s)

    def group_body(g, carry):
        for s in range(PEER_SLOTS):
            t = g * PEER_SLOTS + s
            wait_token(s)
            start_token(t + PEER_LOOK, (s + PEER_LOOK) % PEER_SLOTS)
            evaluate(t, s)
        return carry

    lax.fori_loop(0, tb // PEER_SLOTS, group_body, 0)

    @pl.when(step == pl.num_programs(0) - 1)
    def _():
        for s in range(PEER_LOOK):
            wait_token(s)


def _peer_gather(eidx, xn, gate_blocks, h, table, *, tb):
    n, d = h.shape
    rows = PEER_ROWS
    assert n % tb == 0 and tb % PEER_SLOTS == 0
    nb = n // tb
    idx2 = eidx.reshape(nb, tb * rows)
    idx_ext = jnp.concatenate([idx2, jnp.roll(idx2[:, :PEER_IDX_TAIL * rows], -1, axis=0)], axis=1).reshape(-1)
    return pl.pallas_call(
        functools.partial(_peer_gather_kernel, tb=tb),
        grid=(nb,),
        in_specs=[pl.BlockSpec(((tb + PEER_IDX_TAIL) * rows,), lambda i: (i,), memory_space=pltpu.SMEM),
                  pl.BlockSpec((tb, d), lambda i: (i, 0)),
                  pl.BlockSpec((1, rows, tb), lambda i: (i, 0, 0)),
                  pl.BlockSpec((tb, d), lambda i: (i, 0)),
                  pl.BlockSpec(memory_space=pl.ANY)],
        out_specs=pl.BlockSpec((tb, d), lambda i: (i, 0)),
        out_shape=jax.ShapeDtypeStruct((n, d), F32),
        scratch_shapes=[pltpu.VMEM((rows, d), jnp.uint32) for _ in range(PEER_SLOTS)]
        + [pltpu.SemaphoreType.DMA((PEER_SLOTS,))],
        compiler_params=_cparams("arbitrary"),
        name="peer_gather",
    )(idx_ext, xn, gate_blocks, h, table)


def _ple_kernel(h_ref, p_ref, n3_ref, fn_ref, wg_ref, wp_ref, y_ref):
    h = h_ref[...]
    gate = _sigmoid(_dot1(_rms_rows(h, n3_ref[...]), wg_ref[...]))
    h = h + gate * _dot1(p_ref[...], wp_ref[...])
    y_ref[...] = _rms_rows(h, fn_ref[...])


def _ple_final(h, p, norm3, final_norm, w_gate, w_proj, *, tm):
    n, d = h.shape
    pd = p.shape[1]
    assert n % tm == 0
    return pl.pallas_call(
        _ple_kernel,
        grid=(n // tm,),
        in_specs=[pl.BlockSpec((tm, d), lambda i: (i, 0)),
                  pl.BlockSpec((tm, pd), lambda i: (i, 0)),
                  pl.BlockSpec((1, d), lambda i: (0, 0)),
                  pl.BlockSpec((1, d), lambda i: (0, 0)),
                  pl.BlockSpec((d, d), lambda i: (0, 0)),
                  pl.BlockSpec((pd, d), lambda i: (0, 0))],
        out_specs=pl.BlockSpec((tm, d), lambda i: (i, 0)),
        out_shape=jax.ShapeDtypeStruct((n, d), F32),
        compiler_params=_cparams("parallel"),
        name="ple_final",
    )(h, p, norm3.reshape(1, d), final_norm.reshape(1, d), w_gate.astype(BF16), w_proj.astype(BF16))


QK_DIM = GDN_HEADS * GDN_DK
V_DIM = GDN_HEADS * GDN_DV
CONV_DIM = 2 * QK_DIM + V_DIM
MLP_DIM = MLP_GROUPS * MLP_GROUP_DIM
Z_COL = CONV_DIM
UV_COL = Z_COL + V_DIM
AB_COL = UV_COL + 2 * MLP_DIM
PROJ_COLS = AB_COL + LANES
PROJ_TN = 896


def _relayout_w_in(w_in):
    c0 = CONV_DIM
    c2 = c0 + 2 * GDN_HEADS
    c3 = c2 + V_DIM
    ab = jnp.pad(w_in[:, c0:c2], ((0, 0), (0, LANES - 2 * GDN_HEADS)))
    return jnp.concatenate([w_in[:, :c0], w_in[:, c2:c3], w_in[:, c3:], ab], axis=1)


def _pack_tables(peer_u, peer_v):
    ub = lax.bitcast_convert_type(peer_u.astype(BF16), jnp.uint16).astype(jnp.uint32)
    vb = lax.bitcast_convert_type(peer_v.astype(BF16), jnp.uint16).astype(jnp.uint32)
    return ub | (vb << 16)


def _row_tile(n, want):
    t = min(n, want)
    assert n % t == 0
    return t


def _layer(x, p, s0, buf, w, final_norm):
    b, t, d = x.shape
    n = b * t
    x2 = x.reshape(n, d)
    tm = _row_tile(n, 512)

    proj = _norm_matmul(x2, w["norm1"], w["w_in"], tm=tm, tn=PROJ_TN, name="in_proj")
    proj3 = proj.reshape(b, t, PROJ_COLS)
    new_buf = jnp.concatenate([buf, proj3[:, :, :CONV_DIM]], axis=1)[:, t:]

    buf8 = jnp.pad(buf, ((0, 0), (SUBLANES - (CONV_W - 1), 0), (0, 0)))
    q, k, v = _conv_qkv(proj3, buf8, w["conv_w"], tt=_row_tile(t, 256))

    tg = -(-t // GDN_CHUNK) * GDN_CHUNK
    ab3 = proj3[:, :, AB_COL:]
    if tg != t:
        padt = ((0, 0), (0, tg - t), (0, 0))
        q, k, v, ab3 = (jnp.pad(a, padt) for a in (q, k, v, ab3))
    abt = jnp.transpose(ab3[:, :, :2 * GDN_HEADS].reshape(b, tg // GDN_CHUNK, GDN_CHUNK, 2 * GDN_HEADS),
                        (0, 1, 3, 2))
    o, s_new = _gdn(q, k, v, ab3, abt, w["a_log"], w["dt_bias"], s0, tt=_row_tile(tg, 512), t_valid=t)
    o = o[:, :t]

    lc = min(t, MLP_CHUNK)
    y_b, vn = _mlp_branch(proj3, UV_COL // (2 * MLP_DIM), w["ln_g"], w["ln_b"], w["w_s"], w["b_s"],
                          rows=_row_tile(t, 512), lc=lc)

    h1 = _outproj(o.reshape(n, V_DIM), proj, Z_COL // V_DIM, y_b.reshape(n, MLP_DIM), x2,
                  w["gdn_norm"], w["w_out"], tm=_row_tile(n, 256))

    qp, xn2 = _norm_matmul(h1, w["norm2"], w["w_q"], tm=tm, tn=512, passes=3, emit_xn=True, name="peer_query")
    tb = 128
    eidx_t, gate_t = _peer_topk(qp, w["sub_keys"], tk=tb)
    rows = PEER_HEADS * PEER_TOPK
    gate_blocks = jnp.transpose(gate_t.reshape(rows, n // tb, tb), (1, 0, 2))
    h2 = _peer_gather(jnp.transpose(eidx_t), xn2, gate_blocks, h1, w["table"], tb=tb)

    y = _ple_final(h2, p.reshape(n, -1), w["norm3"], final_norm, w["w_ple_gate"], w["w_ple_proj"],
                   tm=_row_tile(n, 256))
    return y.reshape(b, t, d), s_new, new_buf, vn


def kernel(x_prompt, x_sample, state_gdn, cache_conv, p_prompt, p_sample, norm1, w_in, conv_w, a_log, dt_bias,
           gdn_norm, ln_g, ln_b, w_s, b_s, w_out, norm2, w_q, sub_keys, peer_u, peer_v, norm3, w_ple_gate,
           w_ple_proj, final_norm):
    assert norm1.shape[0] == 1, "single layer"
    w = dict(norm1=norm1[0], w_in=_relayout_w_in(w_in[0]), conv_w=conv_w[0], a_log=a_log[0], dt_bias=dt_bias[0],
             gdn_norm=gdn_norm[0], ln_g=ln_g[0], ln_b=ln_b[0], w_s=w_s[0], b_s=b_s[0], w_out=w_out[0],
             norm2=norm2[0], w_q=w_q[0], sub_keys=sub_keys[0], table=_pack_tables(peer_u[0], peer_v[0]),
             norm3=norm3[0], w_ple_gate=w_ple_gate[0], w_ple_proj=w_ple_proj[0])
    bp = x_prompt.shape[0]
    s0p = jnp.zeros((bp, GDN_HEADS, GDN_DK, GDN_DV), F32)
    b0p = jnp.zeros((bp, CONV_W - 1, CONV_DIM), F32)
    y_s, s_s, b_s_new, v_s = _layer(x_sample, p_sample[0], state_gdn[0], cache_conv[0], w, final_norm)
    y_p, s_p, b_p, _ = _layer(x_prompt, p_prompt[0], s0p, b0p, w, final_norm)
    return (y_p, y_s, s_p[None], b_p[None], s_s[None], b_s_new[None], v_s[None])
```

```python
import functools
import math

import jax
import jax.numpy as jnp
from jax import lax
from jax.experimental import pallas as pl
from jax.experimental.pallas import tpu as pltpu

F32 = jnp.float32
BF16 = jnp.bfloat16

EPS = 1e-6
GDN_CHUNK = 64
GDN_HEADS = 8
GDN_HEAD_GROUP = 8
GDN_DK = 128
GDN_DV = 128
CONV_W = 4
MLP_CHUNK = 128
MLP_GROUPS = 8
MLP_GROUP_DIM = 128
PEER_HEADS = 8
PEER_QHALF = 128
N_KEYS = 128
PEER_TOPK = 16
PEER_TOPK_HEADS_PER_ITER = 4

LANES = 128
SUBLANES = 8
VMEM_LIMIT_BYTES = 56 * 1024 * 1024


def _cparams(*sem):
    return pltpu.CompilerParams(dimension_semantics=sem, vmem_limit_bytes=VMEM_LIMIT_BYTES)


def _split3(x):
    hi = x.astype(BF16)
    r1 = x - hi.astype(F32)
    mid = r1.astype(BF16)
    lo = (r1 - mid.astype(F32)).astype(BF16)
    return hi, mid, lo


def _dg(a, b, dims):
    return lax.dot_general(a, b, (dims, ((), ())), preferred_element_type=F32)


_NN = ((1,), (0,))
_NT = ((1,), (1,))
_TN = ((0,), (0,))


def _dot1(a, b, dims=_NN):
    return _dg(a.astype(BF16), b.astype(BF16), dims)


def _dot3(a, b, dims=_NN):
    ah = a.astype(BF16)
    al = (a - ah.astype(F32)).astype(BF16)
    bh = b.astype(BF16)
    bl = (b - bh.astype(F32)).astype(BF16)
    return _dg(ah, bh, dims) + (_dg(ah, bl, dims) + _dg(al, bh, dims))


def _dot_exact_lhs(a_exact_bf16, b, dims=_NN):
    b0, b1, b2 = _split3(b)
    return _dg(a_exact_bf16, b0, dims) + (_dg(a_exact_bf16, b1, dims) + _dg(a_exact_bf16, b2, dims))


def _dot_exact_rhs(a, b_exact_bf16, dims=_NN):
    a0, a1, a2 = _split3(a)
    return _dg(a0, b_exact_bf16, dims) + (_dg(a1, b_exact_bf16, dims) + _dg(a2, b_exact_bf16, dims))


def _rms_rows(x, gain):
    ms = jnp.mean(x * x, axis=-1, keepdims=True)
    return x * lax.rsqrt(ms + EPS) * gain


def _sigmoid(x):
    return 1.0 / (1.0 + jnp.exp(-x))


def _silu(x):
    return x * _sigmoid(x)


def _gelu(x):
    return 0.5 * x * (1.0 + lax.erf(x * (1.0 / math.sqrt(2.0))))


def _softplus(x):
    return jnp.maximum(x, 0.0) + jnp.log1p(jnp.exp(-jnp.abs(x)))


def _norm_matmul_kernel(x_ref, g_ref, *refs, passes, emit_xn):
    if passes == 1:
        w_refs, rest = refs[:1], refs[1:]
    else:
        w_refs, rest = refs[:2], refs[2:]
    if emit_xn:
        o_ref, xn_out_ref = rest[0], rest[1]
        scr = rest[2:]
    else:
        o_ref = rest[0]
        scr = rest[1:]
    j = pl.program_id(1)

    @pl.when(j == 0)
    def _():
        xn = _rms_rows(x_ref[...], g_ref[...])
        if emit_xn:
            xn_out_ref[...] = xn
        hi = xn.astype(BF16)
        scr[0][...] = hi
        if passes == 3:
            scr[1][...] = (xn - hi.astype(F32)).astype(BF16)

    if passes == 1:
        o_ref[...] = _dg(scr[0][...], w_refs[0][...], _NN)
    else:
        xh = scr[0][...]
        wh = w_refs[0][...]
        o_ref[...] = _dg(xh, wh, _NN) + (_dg(xh, w_refs[1][...], _NN) + _dg(scr[1][...], wh, _NN))


def _norm_matmul(x, gain, w, *, tm, tn, passes=1, emit_xn=False, name):
    n, d = x.shape
    m = w.shape[1]
    assert n % tm == 0 and m % tn == 0
    wh = w.astype(BF16)
    ws = [wh] if passes == 1 else [wh, (w - wh.astype(F32)).astype(BF16)]
    w_specs = [pl.BlockSpec((d, tn), lambda i, j: (0, j)) for _ in ws]
    out_shape = [jax.ShapeDtypeStruct((n, m), F32)]
    out_specs = [pl.BlockSpec((tm, tn), lambda i, j: (i, j))]
    if emit_xn:
        out_shape.append(jax.ShapeDtypeStruct((n, d), F32))
        out_specs.append(pl.BlockSpec((tm, d), lambda i, j: (i, 0)))
    scratch = [pltpu.VMEM((tm, d), BF16) for _ in range(1 if passes == 1 else 2)]
    res = pl.pallas_call(
        functools.partial(_norm_matmul_kernel, passes=passes, emit_xn=emit_xn),
        grid=(n // tm, m // tn),
        in_specs=[pl.BlockSpec((tm, d), lambda i, j: (i, 0)),
                  pl.BlockSpec((1, d), lambda i, j: (0, 0))] + w_specs,
        out_specs=out_specs,
        out_shape=out_shape,
        scratch_shapes=scratch,
        compiler_params=_cparams("parallel", "arbitrary"),
        name=name,
    )(x, gain.reshape(1, d), *ws)
    return res if emit_xn else res[0]


def _conv_kernel(x_ref, buf_ref, w_ref, q_ref, k_ref, v_ref, xp_ref, *, tt):
    t = pl.program_id(1)
    halo = SUBLANES

    @pl.when(t == 0)
    def _():
        xp_ref[0:halo, :] = buf_ref[0]

    @pl.when(t > 0)
    def _():
        xp_ref[0:halo, :] = xp_ref[tt:tt + halo, :]

    xp_ref[halo:halo + tt, :] = x_ref[0]
    base = halo - (CONV_W - 1)
    y = xp_ref[base:base + tt, :] * w_ref[0:1, :]
    for j in range(1, CONV_W):
        y = y + xp_ref[base + j:base + j + tt, :] * w_ref[j:j + 1, :]
    y = _silu(y)
    qk_dim = GDN_HEADS * GDN_DK
    for h in range(GDN_HEADS):
        qh = y[:, h * GDN_DK:(h + 1) * GDN_DK]
        qn = qh * lax.rsqrt(jnp.sum(qh * qh, axis=-1, keepdims=True) + EPS)
        q_ref[0, :, h * GDN_DK:(h + 1) * GDN_DK] = qn * (GDN_DK ** -0.5)
        kh = y[:, qk_dim + h * GDN_DK:qk_dim + (h + 1) * GDN_DK]
        k_ref[0, :, h * GDN_DK:(h + 1) * GDN_DK] = kh * lax.rsqrt(jnp.sum(kh * kh, axis=-1, keepdims=True) + EPS)
    v_ref[0] = y[:, 2 * qk_dim:]


def _conv_qkv(proj3, buf8, conv_w, *, tt):
    b, t, _ = proj3.shape
    cdim = conv_w.shape[1]
    hd = GDN_HEADS * GDN_DK
    assert t % tt == 0
    shp = jax.ShapeDtypeStruct((b, t, hd), F32)
    ospec = pl.BlockSpec((1, tt, hd), lambda i, j: (i, j, 0))
    return pl.pallas_call(
        functools.partial(_conv_kernel, tt=tt),
        grid=(b, t // tt),
        in_specs=[pl.BlockSpec((1, tt, cdim), lambda i, j: (i, j, 0)),
                  pl.BlockSpec((1, SUBLANES, cdim), lambda i, j: (i, 0, 0)),
                  pl.BlockSpec((CONV_W, cdim), lambda i, j: (0, 0))],
        out_specs=[ospec, ospec, ospec],
        out_shape=[shp, shp, shp],
        scratch_shapes=[pltpu.VMEM((tt + 2 * SUBLANES, cdim), F32)],
        compiler_params=_cparams("parallel", "arbitrary"),
        name="conv_qkv",
    )(proj3, buf8, conv_w)


def _gdn_kernel(q_ref, k_ref, v_ref, ab_ref, abt_ref, prm_row_ref, prm_col_ref, s0_ref,
                o_ref, sout_ref, state_ref, *, tt, t_valid):
    tstep = pl.program_id(1)
    c = GDN_CHUNK

    @pl.when(tstep == 0)
    def _():
        state_ref[...] = s0_ref[0]

    ii = lax.broadcasted_iota(jnp.int32, (c, c), 0)
    jj = lax.broadcasted_iota(jnp.int32, (c, c), 1)
    incl = ii >= jj
    strict = ii > jj
    tri = incl.astype(BF16)
    tri_t = (jj >= ii).astype(BF16)
    eye = (ii == jj).astype(F32)

    alog_row = prm_row_ref[0:1, :]
    dtb_row = prm_row_ref[1:2, :]
    alog_col = prm_col_ref[:, 0:1]
    dtb_col = prm_col_ref[:, 1:2]

    def chunk_body(ci, carry):
        r0 = pl.multiple_of(ci * c, c)
        tpos = tstep * tt + r0
        a_blk = ab_ref[0, pl.ds(r0, c), :]
        valid_col = (tpos + lax.broadcasted_iota(jnp.int32, (c, LANES), 0)) < t_valid
        g_col = jnp.where(valid_col, -jnp.exp(alog_row) * _softplus(a_blk + dtb_row), 0.0)
        beta_col = jnp.where(valid_col, _sigmoid(a_blk), 0.0)
        gc_col = _dot_exact_lhs(tri, g_col)
        at_blk = abt_ref[0, ci]
        valid_row = (tpos + lax.broadcasted_iota(jnp.int32, (2 * GDN_HEADS, c), 1)) < t_valid
        g_row = jnp.where(valid_row, -jnp.exp(alog_col) * _softplus(at_blk + dtb_col), 0.0)
        gc_row = _dot_exact_rhs(g_row, tri_t)

        for h0 in range(0, GDN_HEADS, GDN_HEAD_GROUP):
            heads = range(h0, h0 + GDN_HEAD_GROUP)
            st = {}
            for h in heads:
                sl = slice(h * GDN_DK, (h + 1) * GDN_DK)
                qh = q_ref[0, pl.ds(r0, c), sl]
                kh = k_ref[0, pl.ds(r0, c), sl]
                vh = v_ref[0, pl.ds(r0, c), sl]
                gc = gc_col[:, h:h + 1]
                beta = beta_col[:, GDN_HEADS + h:GDN_HEADS + h + 1]
                gcr = gc_row[h:h + 1, :]
                gc_last = gc_col[c - 1:c, h:h + 1]
                decay = jnp.where(incl, jnp.exp(jnp.where(incl, gc - gcr, 0.0)), 0.0)
                egc = jnp.exp(gc)
                kb = kh * beta
                lmat = jnp.where(strict, _dot1(kb, kh, _NT) * decay, 0.0)
                st[h] = dict(
                    sl=sl, tinv=eye - lmat, pw=lmat,
                    rhs=jnp.concatenate([vh * beta, kb * egc], axis=-1),
                    attn=_dot1(qh, kh, _NT) * decay, q_dec=qh * egc,
                    k_dec=kh * jnp.exp(gc_last - gc), bd=jnp.exp(gc_last))
            for _ in range(5):
                for h in heads:
                    st[h]["pw"] = _dot3(st[h]["pw"], st[h]["pw"])
                for h in heads:
                    st[h]["tinv"] = st[h]["tinv"] + _dot3(st[h]["tinv"], st[h]["pw"])
            for h in heads:
                st[h]["sol"] = _dot3(st[h]["tinv"], st[h]["rhs"])
            for h in heads:
                s = state_ref[h]
                st[h]["s"] = s
                st[h]["v_new"] = st[h]["sol"][:, :GDN_DV] - _dot1(st[h]["sol"][:, GDN_DV:], s)
            for h in heads:
                d = st[h]
                o_ref[0, pl.ds(r0, c), d["sl"]] = _dot1(d["q_dec"], d["s"]) + _dot1(d["attn"], d["v_new"])
                state_ref[h] = d["s"] * d["bd"] + _dot1(d["k_dec"], d["v_new"], _TN)
        return carry

    lax.fori_loop(0, tt // c, chunk_body, 0)

    @pl.when(tstep == pl.num_programs(1) - 1)
    def _():
        sout_ref[0] = state_ref[...]


def _gdn(q, k, v, ab, abt, a_log, dt_bias, s0, *, tt, t_valid):
    b, t, hd = q.shape
    assert t % tt == 0 and tt % GDN_CHUNK == 0
    prm_row = jnp.zeros((SUBLANES, LANES), F32)
    prm_row = prm_row.at[0, :GDN_HEADS].set(a_log).at[1, :GDN_HEADS].set(dt_bias)
    prm_col = jnp.zeros((2 * GDN_HEADS, LANES), F32)
    prm_col = prm_col.at[:GDN_HEADS, 0].set(a_log).at[:GDN_HEADS, 1].set(dt_bias)
    seq = pl.BlockSpec((1, tt, hd), lambda i, j: (i, j, 0))
    st = pl.BlockSpec((1, GDN_HEADS, GDN_DK, GDN_DV), lambda i, j: (i, 0, 0, 0))
    return pl.pallas_call(
        functools.partial(_gdn_kernel, tt=tt, t_valid=t_valid),
        grid=(b, t // tt),
        in_specs=[seq, seq, seq,
                  pl.BlockSpec((1, tt, LANES), lambda i, j: (i, j, 0)),
                  pl.BlockSpec((1, tt // GDN_CHUNK, 2 * GDN_HEADS, GDN_CHUNK), lambda i, j: (i, j, 0, 0)),
                  pl.BlockSpec((SUBLANES, LANES), lambda i, j: (0, 0)),
                  pl.BlockSpec((2 * GDN_HEADS, LANES), lambda i, j: (0, 0)),
                  st],
        out_specs=[seq, st],
        out_shape=[jax.ShapeDtypeStruct((b, t, hd), F32),
                   jax.ShapeDtypeStruct((b, GDN_HEADS, GDN_DK, GDN_DV), F32)],
        scratch_shapes=[pltpu.VMEM((GDN_HEADS, GDN_DK, GDN_DV), F32)],
        compiler_params=_cparams("parallel", "arbitrary"),
        name="gdn_core",
    )(q, k, v, ab, abt, prm_row, prm_col, s0)


def _mlp_kernel(uv_ref, lng_ref, lnb_ref, ws_ref, bst_ref, y_ref, vn_ref, *, rows, lc):
    mdim = MLP_GROUPS * MLP_GROUP_DIM
    ii = lax.broadcasted_iota(jnp.int32, (lc, lc), 0)
    jj = lax.broadcasted_iota(jnp.int32, (lc, lc), 1)
    keep = ii >= jj
    for r in range(rows // lc):
        rs = slice(r * lc, (r + 1) * lc)
        g = _gelu(uv_ref[0, rs, :])
        u = g[:, :mdim]
        vv = g[:, mdim:]
        mu = jnp.mean(vv, axis=-1, keepdims=True)
        xc = vv - mu
        vn = xc * lax.rsqrt(jnp.mean(xc * xc, axis=-1, keepdims=True) + EPS) * lng_ref[...] + lnb_ref[...]
        vn_ref[0, rs, :] = vn
        for gi in range(MLP_GROUPS):
            gs = slice(gi * MLP_GROUP_DIM, (gi + 1) * MLP_GROUP_DIM)
            wm = jnp.where(keep, ws_ref[gi], 0.0)
            s = _dot1(wm, vn[:, gs]) + bst_ref[:, gi:gi + 1]
            y_ref[0, rs, gs] = u[:, gs] * s


def _mlp_branch(proj3, col_block, ln_g, ln_b, w_s, b_s, *, rows, lc):
    b, t, _ = proj3.shape
    mdim = MLP_GROUPS * MLP_GROUP_DIM
    assert t % rows == 0 and rows % lc == 0
    ws = w_s[:, :lc, :lc]
    bst = jnp.transpose(b_s[:, :lc])
    shp = jax.ShapeDtypeStruct((b, t, mdim), F32)
    ospec = pl.BlockSpec((1, rows, mdim), lambda i, j: (i, j, 0))
    return pl.pallas_call(
        functools.partial(_mlp_kernel, rows=rows, lc=lc),
        grid=(b, t // rows),
        in_specs=[pl.BlockSpec((1, rows, 2 * mdim), lambda i, j: (i, j, col_block)),
                  pl.BlockSpec((1, mdim), lambda i, j: (0, 0)),
                  pl.BlockSpec((1, mdim), lambda i, j: (0, 0)),
                  pl.BlockSpec((MLP_GROUPS, lc, lc), lambda i, j: (0, 0, 0)),
                  pl.BlockSpec((lc, MLP_GROUPS), lambda i, j: (0, 0))],
        out_specs=[ospec, ospec],
        out_shape=[shp, shp],
        compiler_params=_cparams("parallel", "parallel"),
        name="mlp_branch",
    )(proj3, ln_g.reshape(1, mdim), ln_b.reshape(1, mdim), ws, bst)


def _outproj_kernel(o_ref, z_ref, y_ref, h_ref, gn_ref, w_ref, out_ref):
    parts = []
    for hh in range(GDN_HEADS):
        sl = slice(hh * GDN_DV, (hh + 1) * GDN_DV)
        parts.append((_rms_rows(o_ref[:, sl], gn_ref[...]) * _silu(z_ref[:, sl])).astype(BF16))
    parts.append(y_ref[...].astype(BF16))
    cat = jnp.concatenate(parts, axis=-1)
    out_ref[...] = h_ref[...] + _dg(cat, w_ref[...], _NN)


def _outproj(o2, proj2, z_block, y2, h2, gdn_norm, w_out, *, tm):
    n, d = h2.shape
    vd = GDN_HEADS * GDN_DV
    assert n % tm == 0
    return pl.pallas_call(
        _outproj_kernel,
        grid=(n // tm,),
        in_specs=[pl.BlockSpec((tm, vd), lambda i: (i, 0)),
                  pl.BlockSpec((tm, vd), lambda i: (i, z_block)),
                  pl.BlockSpec((tm, vd), lambda i: (i, 0)),
                  pl.BlockSpec((tm, d), lambda i: (i, 0)),
                  pl.BlockSpec((1, GDN_DV), lambda i: (0, 0)),
                  pl.BlockSpec(w_out.shape, lambda i: (0, 0))],
        out_specs=pl.BlockSpec((tm, d), lambda i: (i, 0)),
        out_shape=jax.ShapeDtypeStruct((n, d), F32),
        compiler_params=_cparams("parallel"),
        name="out_proj",
    )(o2, proj2, y2, h2, gdn_norm.reshape(1, GDN_DV), w_out.astype(BF16))


def _topk_axis0(x, k, payloads=()):
    r = x.shape[0]
    iota = lax.broadcasted_iota(jnp.int32, x.shape, 0)
    vals, idxs = [], []
    outs = [[] for _ in payloads]
    for _ in range(k):
        m = jnp.max(x, axis=0, keepdims=True)
        i = jnp.min(jnp.where(x == m, iota, r), axis=0, keepdims=True)
        hit = iota == i
        vals.append(m)
        idxs.append(i)
        for p, acc in zip(payloads, outs):
            acc.append(jnp.sum(jnp.where(hit, p, 0), axis=0, keepdims=True))
        x = jnp.where(hit, -jnp.inf, x)
    cat = lambda parts: jnp.concatenate(parts, axis=0)
    return cat(vals), cat(idxs), [cat(acc) for acc in outs]


def _peer_topk_kernel(q_ref, keys_ref, eidx_ref, gate_ref):
    k = PEER_TOPK

    def one_head(h):
        sv, si = [], []
        for s in range(2):
            c0 = pl.multiple_of(h * (2 * PEER_QHALF) + s * PEER_QHALF, PEER_QHALF)
            qhs = q_ref[:, pl.ds(c0, PEER_QHALF)]
            sc_t = _dot3(keys_ref[s, h], qhs, _NT)
            v, i, _ = _topk_axis0(sc_t, k)
            sv.append(v)
            si.append(i)
        half = k // 2
        sub = lax.broadcasted_iota(jnp.int32, (half, sv[0].shape[1]), 0)
        cand_parts = [sv[0][0:1, :] + sv[1]]
        cidx_parts = [si[0][0:1, :] * N_KEYS + si[1]]
        for a in range(1, half):
            keep = sub < (k // (a + 1))
            cand_parts.append(jnp.where(keep, sv[0][a:a + 1, :] + sv[1][0:half, :], -jnp.inf))
            cidx_parts.append(si[0][a:a + 1, :] * N_KEYS + si[1][0:half, :])
        cand_parts.append(sv[0][half:, :] + sv[1][0:1, :])
        cidx_parts.append(si[0][half:, :] * N_KEYS + si[1][0:1, :])
        cand = jnp.concatenate(cand_parts, axis=0)
        cidx = jnp.concatenate(cidx_parts, axis=0)
        fv, _, (fe,) = _topk_axis0(cand, k, (cidx,))
        e = jnp.exp(fv - fv[0:1, :])
        gate = e / jnp.sum(e, axis=0, keepdims=True)
        r0 = pl.multiple_of(h * k, k)
        eidx_ref[pl.ds(r0, k), :] = fe
        gate_ref[pl.ds(r0, k), :] = gate

    def heads_body(hp, carry):
        for j in range(PEER_TOPK_HEADS_PER_ITER):
            one_head(hp * PEER_TOPK_HEADS_PER_ITER + j)
        return carry

    lax.fori_loop(0, PEER_HEADS // PEER_TOPK_HEADS_PER_ITER, heads_body, 0)


def _peer_topk(q, sub_keys, *, tk):
    n, qd = q.shape
    assert n % tk == 0
    rows = PEER_HEADS * PEER_TOPK
    return pl.pallas_call(
        _peer_topk_kernel,
        grid=(n // tk,),
        in_specs=[pl.BlockSpec((tk, qd), lambda i: (i, 0)),
                  pl.BlockSpec(sub_keys.shape, lambda i: (0, 0, 0, 0))],
        out_specs=[pl.BlockSpec((rows, tk), lambda i: (0, i)),
                   pl.BlockSpec((rows, tk), lambda i: (0, i))],
        out_shape=[jax.ShapeDtypeStruct((rows, n), jnp.int32),
                   jax.ShapeDtypeStruct((rows, n), F32)],
        compiler_params=_cparams("parallel"),
        name="peer_topk",
    )(q, sub_keys)


PEER_ROWS = PEER_HEADS * PEER_TOPK
PEER_SLOTS = 8
PEER_LOOK = PEER_SLOTS - 1
assert PEER_SLOTS % 2 == 0 and PEER_LOOK % 2 == 1
PEER_IDX_TAIL = 8
assert PEER_LOOK <= PEER_IDX_TAIL and (PEER_IDX_TAIL * PEER_ROWS) % 1024 == 0
PEER_ACCS = 4
PEER_DOT_COPIES = 5
PEER_MIX_COPIES = 3
assert (PEER_DOT_COPIES + PEER_MIX_COPIES) * (PEER_ROWS // SUBLANES) == PEER_ROWS


def _sublane_fold(parts):
    sub = lax.broadcasted_iota(jnp.int32, parts[0].shape, 0)
    for dist in (4, 2, 1):
        low = (sub & dist) == 0
        nxt = []
        for a in range(len(parts) // 2):
            lo_src, hi_src = parts[a], parts[a + len(parts) // 2]
            kept = jnp.where(low, lo_src, hi_src)
            moved = jnp.where(low, pltpu.roll(lo_src, SUBLANES - dist, axis=0), pltpu.roll(hi_src, dist, axis=0))
            nxt.append(kept + moved)
        parts = nxt
    return parts[0]


def _peer_gather_kernel(idx_ref, x_ref, gate_ref, h_ref, tab_ref, out_ref, buf_even, buf_odd, sem_ref, coef_ref, *, tb):
    rows = PEER_ROWS
    nc = x_ref.shape[1] // LANES
    half = nc // 2
    groups = rows // SUBLANES
    step = pl.program_id(0)
    hi_mask = jnp.uint32(0xFFFF0000)
    bufs = (buf_even, buf_odd)

    def slot_of(tok):
        return (tok // 2) % (PEER_SLOTS // 2)

    def start_rows(arr, tok, r0, count):
        for j in range(count):
            r = r0 + j
            pltpu.make_async_copy(tab_ref.at[idx_ref[tok * rows + r]], arr.at[slot_of(tok), r],
                                  sem_ref.at[tok % PEER_SLOTS]).start(priority=j % 2)

    def wait_token(arr, tok):
        pltpu.make_async_copy(tab_ref.at[pl.ds(0, rows)], arr.at[slot_of(tok)], sem_ref.at[tok % PEER_SLOTS]).wait()

    def load_x(t):
        xrow = x_ref[pl.ds(t, 1), :]
        x_lo = jnp.concatenate([xrow[:, c * LANES:(c + 1) * LANES] for c in range(half)], axis=0)
        x_hi = jnp.concatenate([xrow[:, c * LANES:(c + 1) * LANES] for c in range(half, nc)], axis=0)
        return x_lo, x_hi

    def dot_group(buf, x, g):
        parts = []
        for s in range(SUBLANES):
            r = g * SUBLANES + s
            u_lo = lax.bitcast_convert_type(buf[r, 0:half, :] << 16, F32)
            u_hi = lax.bitcast_convert_type(buf[r, half:nc, :] << 16, F32)
            parts.append(u_lo * x[0] + u_hi * x[1])
        return _sublane_fold(parts)

    def finish_dot(t, folded):
        act = jnp.sum(jnp.concatenate(folded, axis=0), axis=-1, keepdims=True)
        lane_tok = lax.broadcasted_iota(jnp.int32, (rows, tb), 1)
        gcol = jnp.sum(jnp.where(lane_tok == t, gate_ref[0], 0.0), axis=-1, keepdims=True)
        coef_ref[...] = jnp.broadcast_to(gcol * _gelu(act), (rows, LANES))

    def mix_group(buf, g, acc_lo, acc_hi):
        for s in range(SUBLANES):
            r = g * SUBLANES + s
            c = jnp.broadcast_to(coef_ref[r:r + 1, :], (half, LANES))
            a = r % PEER_ACCS
            acc_lo[a] = acc_lo[a] + lax.bitcast_convert_type(buf[r, 0:half, :] & hi_mask, F32) * c
            acc_hi[a] = acc_hi[a] + lax.bitcast_convert_type(buf[r, half:nc, :] & hi_mask, F32) * c

    def emit(t, acc_lo, acc_hi):
        mix_lo = (acc_lo[0] + acc_lo[1]) + (acc_lo[2] + acc_lo[3])
        mix_hi = (acc_hi[0] + acc_hi[1]) + (acc_hi[2] + acc_hi[3])
        mix_row = jnp.concatenate([mix_lo[c:c + 1, :] for c in range(half)]
                                  + [mix_hi[c:c + 1, :] for c in range(half)], axis=1)
        out_ref[pl.ds(t, 1), :] = h_ref[pl.ds(t, 1), :] + mix_row

    def token(t, parity):
        arr, arr_next = bufs[parity], bufs[(parity + PEER_LOOK) % 2]
        tok_next = t + PEER_LOOK
        wait_token(arr, t)
        buf = arr.at[slot_of(t)]
        x = load_x(t)
        folded = []
        for g in range(groups):
            start_rows(arr_next, tok_next, g * PEER_DOT_COPIES, PEER_DOT_COPIES)
            folded.append(dot_group(buf, x, g))
        finish_dot(t, folded)
        acc_lo = [jnp.zeros((half, LANES), F32) for _ in range(PEER_ACCS)]
        acc_hi = [jnp.zeros((half, LANES), F32) for _ in range(PEER_ACCS)]
        for g in range(groups):
            start_rows(arr_next, tok_next, groups * PEER_DOT_COPIES + g * PEER_MIX_COPIES, PEER_MIX_COPIES)
            mix_group(buf, g, acc_lo, acc_hi)
        emit(t, acc_lo, acc_hi)

    @pl.when(step == 0)
    def _():
        for s in range(PEER_LOOK):
            def prime(i, carry, s=s):
                start_rows(bufs[s % 2], s, i * SUBLANES, SUBLANES)
                return carry
            lax.fori_loop(0, groups, prime, 0)

    def pair(p, carry):
        token(2 * p, 0)
        token(2 * p + 1, 1)
        return carry

    lax.fori_loop(0, tb // 2, pair, 0)

    @pl.when(step == pl.num_programs(0) - 1)
    def _():
        for s in range(PEER_LOOK):
            wait_token(bufs[(tb + s) % 2], tb + s)


def _peer_gather(eidx, xn, gate_blocks, h, table, *, tb):
    n, d = h.shape
    rows = PEER_ROWS
    assert n % tb == 0 and tb % PEER_SLOTS == 0 and PEER_ACCS == 4
    nb = n // tb
    idx2 = eidx.reshape(nb, tb * rows)
    idx_ext = jnp.concatenate([idx2, jnp.roll(idx2[:, :PEER_IDX_TAIL * rows], -1, axis=0)], axis=1).reshape(-1)
    return pl.pallas_call(
        functools.partial(_peer_gather_kernel, tb=tb),
        grid=(nb,),
        in_specs=[pl.BlockSpec(((tb + PEER_IDX_TAIL) * rows,), lambda i: (i,), memory_space=pltpu.SMEM),
                  pl.BlockSpec((tb, d), lambda i: (i, 0)),
                  pl.BlockSpec((1, rows, tb), lambda i: (i, 0, 0)),
                  pl.BlockSpec((tb, d), lambda i: (i, 0)),
                  pl.BlockSpec(memory_space=pl.ANY)],
        out_specs=pl.BlockSpec((tb, d), lambda i: (i, 0)),
        out_shape=jax.ShapeDtypeStruct((n, d), F32),
        scratch_shapes=[pltpu.VMEM((PEER_SLOTS // 2, rows, d // LANES, LANES), jnp.uint32),
                        pltpu.VMEM((PEER_SLOTS // 2, rows, d // LANES, LANES), jnp.uint32),
                        pltpu.SemaphoreType.DMA((PEER_SLOTS,)),
                        pltpu.VMEM((rows, LANES), F32)],
        compiler_params=_cparams("arbitrary"),
        name="peer_gather",
    )(idx_ext, xn, gate_blocks, h, table)


def _ple_kernel(h_ref, p_ref, n3_ref, fn_ref, wg_ref, wp_ref, y_ref):
    h = h_ref[...]
    gate = _sigmoid(_dot1(_rms_rows(h, n3_ref[...]), wg_ref[...]))
    h = h + gate * _dot1(p_ref[...], wp_ref[...])
    y_ref[...] = _rms_rows(h, fn_ref[...])


def _ple_final(h, p, norm3, final_norm, w_gate, w_proj, *, tm):
    n, d = h.shape
    pd = p.shape[1]
    assert n % tm == 0
    return pl.pallas_call(
        _ple_kernel,
        grid=(n // tm,),
        in_specs=[pl.BlockSpec((tm, d), lambda i: (i, 0)),
                  pl.BlockSpec((tm, pd), lambda i: (i, 0)),
                  pl.BlockSpec((1, d), lambda i: (0, 0)),
                  pl.BlockSpec((1, d), lambda i: (0, 0)),
                  pl.BlockSpec((d, d), lambda i: (0, 0)),
                  pl.BlockSpec((pd, d), lambda i: (0, 0))],
        out_specs=pl.BlockSpec((tm, d), lambda i: (i, 0)),
        out_shape=jax.ShapeDtypeStruct((n, d), F32),
        compiler_params=_cparams("parallel"),
        name="ple_final",
    )(h, p, norm3.reshape(1, d), final_norm.reshape(1, d), w_gate.astype(BF16), w_proj.astype(BF16))


QK_DIM = GDN_HEADS * GDN_DK
V_DIM = GDN_HEADS * GDN_DV
CONV_DIM = 2 * QK_DIM + V_DIM
MLP_DIM = MLP_GROUPS * MLP_GROUP_DIM
Z_COL = CONV_DIM
UV_COL = Z_COL + V_DIM
AB_COL = UV_COL + 2 * MLP_DIM
PROJ_COLS = AB_COL + LANES
PROJ_TN = 896


def _relayout_w_in(w_in):
    c0 = CONV_DIM
    c2 = c0 + 2 * GDN_HEADS
    c3 = c2 + V_DIM
    ab = jnp.pad(w_in[:, c0:c2], ((0, 0), (0, LANES - 2 * GDN_HEADS)))
    return jnp.concatenate([w_in[:, :c0], w_in[:, c2:c3], w_in[:, c3:], ab], axis=1)


def _pack_tables(peer_u, peer_v):
    ub = lax.bitcast_convert_type(peer_u.astype(BF16), jnp.uint16).astype(jnp.uint32)
    vb = lax.bitcast_convert_type(peer_v.astype(BF16), jnp.uint16).astype(jnp.uint32)
    e, d = peer_u.shape
    return (ub | (vb << 16)).reshape(e, d // LANES, LANES)


def _row_tile(n, want):
    t = min(n, want)
    assert n % t == 0
    return t


def _layer(x, p, s0, buf, w, final_norm):
    b, t, d = x.shape
    n = b * t
    x2 = x.reshape(n, d)
    tm = _row_tile(n, 512)

    proj = _norm_matmul(x2, w["norm1"], w["w_in"], tm=tm, tn=PROJ_TN, name="in_proj")
    proj3 = proj.reshape(b, t, PROJ_COLS)
    new_buf = jnp.concatenate([buf, proj3[:, :, :CONV_DIM]], axis=1)[:, t:]

    buf8 = jnp.pad(buf, ((0, 0), (SUBLANES - (CONV_W - 1), 0), (0, 0)))
    q, k, v = _conv_qkv(proj3, buf8, w["conv_w"], tt=_row_tile(t, 256))

    tg = -(-t // GDN_CHUNK) * GDN_CHUNK
    ab3 = proj3[:, :, AB_COL:]
    if tg != t:
        padt = ((0, 0), (0, tg - t), (0, 0))
        q, k, v, ab3 = (jnp.pad(a, padt) for a in (q, k, v, ab3))
    abt = jnp.transpose(ab3[:, :, :2 * GDN_HEADS].reshape(b, tg // GDN_CHUNK, GDN_CHUNK, 2 * GDN_HEADS),
                        (0, 1, 3, 2))
    o, s_new = _gdn(q, k, v, ab3, abt, w["a_log"], w["dt_bias"], s0, tt=_row_tile(tg, 512), t_valid=t)
    o = o[:, :t]

    lc = min(t, MLP_CHUNK)
    y_b, vn = _mlp_branch(proj3, UV_COL // (2 * MLP_DIM), w["ln_g"], w["ln_b"], w["w_s"], w["b_s"],
                          rows=_row_tile(t, 512), lc=lc)

    h1 = _outproj(o.reshape(n, V_DIM), proj, Z_COL // V_DIM, y_b.reshape(n, MLP_DIM), x2,
                  w["gdn_norm"], w["w_out"], tm=_row_tile(n, 256))

    qp, xn2 = _norm_matmul(h1, w["norm2"], w["w_q"], tm=tm, tn=512, passes=3, emit_xn=True, name="peer_query")
    tb = 128
    eidx_t, gate_t = _peer_topk(qp, w["sub_keys"], tk=tb)
    rows = PEER_HEADS * PEER_TOPK
    gate_blocks = jnp.transpose(gate_t.reshape(rows, n // tb, tb), (1, 0, 2))
    h2 = _peer_gather(jnp.transpose(eidx_t), xn2, gate_blocks, h1, w["table"], tb=tb)

    y = _ple_final(h2, p.reshape(n, -1), w["norm3"], final_norm, w["w_ple_gate"], w["w_ple_proj"],
                   tm=_row_tile(n, 256))
    return y.reshape(b, t, d), s_new, new_buf, vn


def kernel(x_prompt, x_sample, state_gdn, cache_conv, p_prompt, p_sample, norm1, w_in, conv_w, a_log, dt_bias,
           gdn_norm, ln_g, ln_b, w_s, b_s, w_out, norm2, w_q, sub_keys, peer_u, peer_v, norm3, w_ple_gate,
           w_ple_proj, final_norm):
    assert norm1.shape[0] == 1, "single layer"
    w = dict(norm1=norm1[0], w_in=_relayout_w_in(w_in[0]), conv_w=conv_w[0], a_log=a_log[0], dt_bias=dt_bias[0],
             gdn_norm=gdn_norm[0], ln_g=ln_g[0], ln_b=ln_b[0], w_s=w_s[0], b_s=b_s[0], w_out=w_out[0],
             norm2=norm2[0], w_q=w_q[0], sub_keys=sub_keys[0], table=_pack_tables(peer_u[0], peer_v[0]),
             norm3=norm3[0], w_ple_gate=w_ple_gate[0], w_ple_proj=w_ple_proj[0])
    bp = x_prompt.shape[0]
    s0p = jnp.zeros((bp, GDN_HEADS, GDN_DK, GDN_DV), F32)
    b0p = jnp.zeros((bp, CONV_W - 1, CONV_DIM), F32)
    y_s, s_s, b_s_new, v_s = _layer(x_sample, p_sample[0], state_gdn[0], cache_conv[0], w, final_norm)
    y_p, s_p, b_p, _ = _layer(x_prompt, p_prompt[0], s0p, b0p, w, final_norm)
    return (y_p, y_s, s_p[None], b_p[None], s_s[None], b_s_new[None], v_s[None])
```

```python
import functools
import math

import jax
import jax.numpy as jnp
from jax import lax
from jax.experimental import pallas as pl
from jax.experimental.pallas import tpu as pltpu

F32 = jnp.float32
BF16 = jnp.bfloat16

EPS = 1e-6
GDN_CHUNK = 64
GDN_HEADS = 8
GDN_HEAD_GROUP = 8
GDN_DK = 128
GDN_DV = 128
CONV_W = 4
MLP_CHUNK = 128
MLP_GROUPS = 8
MLP_GROUP_DIM = 128
PEER_HEADS = 8
PEER_QHALF = 128
N_KEYS = 128
PEER_TOPK = 16
PEER_TOPK_HEADS_PER_ITER = 4

LANES = 128
SUBLANES = 8
VMEM_LIMIT_BYTES = 56 * 1024 * 1024


def _cparams(*sem):
    return pltpu.CompilerParams(dimension_semantics=sem, vmem_limit_bytes=VMEM_LIMIT_BYTES)


def _split3(x):
    hi = x.astype(BF16)
    r1 = x - hi.astype(F32)
    mid = r1.astype(BF16)
    lo = (r1 - mid.astype(F32)).astype(BF16)
    return hi, mid, lo


def _dg(a, b, dims):
    return lax.dot_general(a, b, (dims, ((), ())), preferred_element_type=F32)


_NN = ((1,), (0,))
_NT = ((1,), (1,))
_TN = ((0,), (0,))


def _dot1(a, b, dims=_NN):
    return _dg(a.astype(BF16), b.astype(BF16), dims)


def _dot3(a, b, dims=_NN):
    ah = a.astype(BF16)
    al = (a - ah.astype(F32)).astype(BF16)
    bh = b.astype(BF16)
    bl = (b - bh.astype(F32)).astype(BF16)
    return _dg(ah, bh, dims) + (_dg(ah, bl, dims) + _dg(al, bh, dims))


def _dot_exact_lhs(a_exact_bf16, b, dims=_NN):
    b0, b1, b2 = _split3(b)
    return _dg(a_exact_bf16, b0, dims) + (_dg(a_exact_bf16, b1, dims) + _dg(a_exact_bf16, b2, dims))


def _dot_exact_rhs(a, b_exact_bf16, dims=_NN):
    a0, a1, a2 = _split3(a)
    return _dg(a0, b_exact_bf16, dims) + (_dg(a1, b_exact_bf16, dims) + _dg(a2, b_exact_bf16, dims))


def _rms_rows(x, gain):
    ms = jnp.mean(x * x, axis=-1, keepdims=True)
    return x * lax.rsqrt(ms + EPS) * gain


def _sigmoid(x):
    return 1.0 / (1.0 + jnp.exp(-x))


def _silu(x):
    return x * _sigmoid(x)


def _gelu(x):
    return 0.5 * x * (1.0 + lax.erf(x * (1.0 / math.sqrt(2.0))))


def _softplus(x):
    return jnp.maximum(x, 0.0) + jnp.log1p(jnp.exp(-jnp.abs(x)))


def _norm_matmul_kernel(x_ref, g_ref, *refs, passes, emit_xn):
    if passes == 1:
        w_refs, rest = refs[:1], refs[1:]
    else:
        w_refs, rest = refs[:2], refs[2:]
    if emit_xn:
        o_ref, xn_out_ref = rest[0], rest[1]
        scr = rest[2:]
    else:
        o_ref = rest[0]
        scr = rest[1:]
    j = pl.program_id(1)

    @pl.when(j == 0)
    def _():
        xn = _rms_rows(x_ref[...], g_ref[...])
        if emit_xn:
            xn_out_ref[...] = xn
        hi = xn.astype(BF16)
        scr[0][...] = hi
        if passes == 3:
            scr[1][...] = (xn - hi.astype(F32)).astype(BF16)

    if passes == 1:
        o_ref[...] = _dg(scr[0][...], w_refs[0][...], _NN)
    else:
        xh = scr[0][...]
        wh = w_refs[0][...]
        o_ref[...] = _dg(xh, wh, _NN) + (_dg(xh, w_refs[1][...], _NN) + _dg(scr[1][...], wh, _NN))


def _norm_matmul(x, gain, w, *, tm, tn, passes=1, emit_xn=False, name):
    n, d = x.shape
    m = w.shape[1]
    assert n % tm == 0 and m % tn == 0
    wh = w.astype(BF16)
    ws = [wh] if passes == 1 else [wh, (w - wh.astype(F32)).astype(BF16)]
    w_specs = [pl.BlockSpec((d, tn), lambda i, j: (0, j)) for _ in ws]
    out_shape = [jax.ShapeDtypeStruct((n, m), F32)]
    out_specs = [pl.BlockSpec((tm, tn), lambda i, j: (i, j))]
    if emit_xn:
        out_shape.append(jax.ShapeDtypeStruct((n, d), F32))
        out_specs.append(pl.BlockSpec((tm, d), lambda i, j: (i, 0)))
    scratch = [pltpu.VMEM((tm, d), BF16) for _ in range(1 if passes == 1 else 2)]
    res = pl.pallas_call(
        functools.partial(_norm_matmul_kernel, passes=passes, emit_xn=emit_xn),
        grid=(n // tm, m // tn),
        in_specs=[pl.BlockSpec((tm, d), lambda i, j: (i, 0)),
                  pl.BlockSpec((1, d), lambda i, j: (0, 0))] + w_specs,
        out_specs=out_specs,
        out_shape=out_shape,
        scratch_shapes=scratch,
        compiler_params=_cparams("parallel", "arbitrary"),
        name=name,
    )(x, gain.reshape(1, d), *ws)
    return res if emit_xn else res[0]


def _conv_kernel(x_ref, buf_ref, w_ref, q_ref, k_ref, v_ref, xp_ref, *, tt):
    t = pl.program_id(1)
    halo = SUBLANES

    @pl.when(t == 0)
    def _():
        xp_ref[0:halo, :] = buf_ref[0]

    @pl.when(t > 0)
    def _():
        xp_ref[0:halo, :] = xp_ref[tt:tt + halo, :]

    xp_ref[halo:halo + tt, :] = x_ref[0]
    base = halo - (CONV_W - 1)
    y = xp_ref[base:base + tt, :] * w_ref[0:1, :]
    for j in range(1, CONV_W):
        y = y + xp_ref[base + j:base + j + tt, :] * w_ref[j:j + 1, :]
    y = _silu(y)
    qk_dim = GDN_HEADS * GDN_DK
    for h in range(GDN_HEADS):
        qh = y[:, h * GDN_DK:(h + 1) * GDN_DK]
        qn = qh * lax.rsqrt(jnp.sum(qh * qh, axis=-1, keepdims=True) + EPS)
        q_ref[0, :, h * GDN_DK:(h + 1) * GDN_DK] = qn * (GDN_DK ** -0.5)
        kh = y[:, qk_dim + h * GDN_DK:qk_dim + (h + 1) * GDN_DK]
        k_ref[0, :, h * GDN_DK:(h + 1) * GDN_DK] = kh * lax.rsqrt(jnp.sum(kh * kh, axis=-1, keepdims=True) + EPS)
    v_ref[0] = y[:, 2 * qk_dim:]


def _conv_qkv(proj3, buf8, conv_w, *, tt):
    b, t, _ = proj3.shape
    cdim = conv_w.shape[1]
    hd = GDN_HEADS * GDN_DK
    assert t % tt == 0
    shp = jax.ShapeDtypeStruct((b, t, hd), F32)
    ospec = pl.BlockSpec((1, tt, hd), lambda i, j: (i, j, 0))
    return pl.pallas_call(
        functools.partial(_conv_kernel, tt=tt),
        grid=(b, t // tt),
        in_specs=[pl.BlockSpec((1, tt, cdim), lambda i, j: (i, j, 0)),
                  pl.BlockSpec((1, SUBLANES, cdim), lambda i, j: (i, 0, 0)),
                  pl.BlockSpec((CONV_W, cdim), lambda i, j: (0, 0))],
        out_specs=[ospec, ospec, ospec],
        out_shape=[shp, shp, shp],
        scratch_shapes=[pltpu.VMEM((tt + 2 * SUBLANES, cdim), F32)],
        compiler_params=_cparams("parallel", "arbitrary"),
        name="conv_qkv",
    )(proj3, buf8, conv_w)


def _gdn_kernel(q_ref, k_ref, v_ref, ab_ref, abt_ref, prm_row_ref, prm_col_ref, s0_ref,
                o_ref, sout_ref, state_ref, *, tt, t_valid):
    tstep = pl.program_id(1)
    c = GDN_CHUNK

    @pl.when(tstep == 0)
    def _():
        state_ref[...] = s0_ref[0]

    ii = lax.broadcasted_iota(jnp.int32, (c, c), 0)
    jj = lax.broadcasted_iota(jnp.int32, (c, c), 1)
    incl = ii >= jj
    strict = ii > jj
    tri = incl.astype(BF16)
    tri_t = (jj >= ii).astype(BF16)
    eye = (ii == jj).astype(F32)

    alog_row = prm_row_ref[0:1, :]
    dtb_row = prm_row_ref[1:2, :]
    alog_col = prm_col_ref[:, 0:1]
    dtb_col = prm_col_ref[:, 1:2]

    def chunk_body(ci, carry):
        r0 = pl.multiple_of(ci * c, c)
        tpos = tstep * tt + r0
        a_blk = ab_ref[0, pl.ds(r0, c), :]
        valid_col = (tpos + lax.broadcasted_iota(jnp.int32, (c, LANES), 0)) < t_valid
        g_col = jnp.where(valid_col, -jnp.exp(alog_row) * _softplus(a_blk + dtb_row), 0.0)
        beta_col = jnp.where(valid_col, _sigmoid(a_blk), 0.0)
        gc_col = _dot_exact_lhs(tri, g_col)
        at_blk = abt_ref[0, ci]
        valid_row = (tpos + lax.broadcasted_iota(jnp.int32, (2 * GDN_HEADS, c), 1)) < t_valid
        g_row = jnp.where(valid_row, -jnp.exp(alog_col) * _softplus(at_blk + dtb_col), 0.0)
        gc_row = _dot_exact_rhs(g_row, tri_t)

        for h0 in range(0, GDN_HEADS, GDN_HEAD_GROUP):
            heads = range(h0, h0 + GDN_HEAD_GROUP)
            st = {}
            for h in heads:
                sl = slice(h * GDN_DK, (h + 1) * GDN_DK)
                qh = q_ref[0, pl.ds(r0, c), sl]
                kh = k_ref[0, pl.ds(r0, c), sl]
                vh = v_ref[0, pl.ds(r0, c), sl]
                gc = gc_col[:, h:h + 1]
                beta = beta_col[:, GDN_HEADS + h:GDN_HEADS + h + 1]
                gcr = gc_row[h:h + 1, :]
                gc_last = gc_col[c - 1:c, h:h + 1]
                decay = jnp.where(incl, jnp.exp(jnp.where(incl, gc - gcr, 0.0)), 0.0)
                egc = jnp.exp(gc)
                kb = kh * beta
                lmat = jnp.where(strict, _dot1(kb, kh, _NT) * decay, 0.0)
                st[h] = dict(
                    sl=sl, tinv=eye - lmat, pw=lmat,
                    rhs=jnp.concatenate([vh * beta, kb * egc], axis=-1),
                    attn=_dot1(qh, kh, _NT) * decay, q_dec=qh * egc,
                    k_dec=kh * jnp.exp(gc_last - gc), bd=jnp.exp(gc_last))
            for _ in range(5):
                for h in heads:
                    st[h]["pw"] = _dot3(st[h]["pw"], st[h]["pw"])
                for h in heads:
                    st[h]["tinv"] = st[h]["tinv"] + _dot3(st[h]["tinv"], st[h]["pw"])
            for h in heads:
                st[h]["sol"] = _dot3(st[h]["tinv"], st[h]["rhs"])
            for h in heads:
                s = state_ref[h]
                st[h]["s"] = s
                st[h]["v_new"] = st[h]["sol"][:, :GDN_DV] - _dot1(st[h]["sol"][:, GDN_DV:], s)
            for h in heads:
                d = st[h]
                o_ref[0, pl.ds(r0, c), d["sl"]] = _dot1(d["q_dec"], d["s"]) + _dot1(d["attn"], d["v_new"])
                state_ref[h] = d["s"] * d["bd"] + _dot1(d["k_dec"], d["v_new"], _TN)
        return carry

    lax.fori_loop(0, tt // c, chunk_body, 0)

    @pl.when(tstep == pl.num_programs(1) - 1)
    def _():
        sout_ref[0] = state_ref[...]


def _gdn(q, k, v, ab, abt, a_log, dt_bias, s0, *, tt, t_valid):
    b, t, hd = q.shape
    assert t % tt == 0 and tt % GDN_CHUNK == 0
    prm_row = jnp.zeros((SUBLANES, LANES), F32)
    prm_row = prm_row.at[0, :GDN_HEADS].set(a_log).at[1, :GDN_HEADS].set(dt_bias)
    prm_col = jnp.zeros((2 * GDN_HEADS, LANES), F32)
    prm_col = prm_col.at[:GDN_HEADS, 0].set(a_log).at[:GDN_HEADS, 1].set(dt_bias)
    seq = pl.BlockSpec((1, tt, hd), lambda i, j: (i, j, 0))
    st = pl.BlockSpec((1, GDN_HEADS, GDN_DK, GDN_DV), lambda i, j: (i, 0, 0, 0))
    return pl.pallas_call(
        functools.partial(_gdn_kernel, tt=tt, t_valid=t_valid),
        grid=(b, t // tt),
        in_specs=[seq, seq, seq,
                  pl.BlockSpec((1, tt, LANES), lambda i, j: (i, j, 0)),
                  pl.BlockSpec((1, tt // GDN_CHUNK, 2 * GDN_HEADS, GDN_CHUNK), lambda i, j: (i, j, 0, 0)),
                  pl.BlockSpec((SUBLANES, LANES), lambda i, j: (0, 0)),
                  pl.BlockSpec((2 * GDN_HEADS, LANES), lambda i, j: (0, 0)),
                  st],
        out_specs=[seq, st],
        out_shape=[jax.ShapeDtypeStruct((b, t, hd), F32),
                   jax.ShapeDtypeStruct((b, GDN_HEADS, GDN_DK, GDN_DV), F32)],
        scratch_shapes=[pltpu.VMEM((GDN_HEADS, GDN_DK, GDN_DV), F32)],
        compiler_params=_cparams("parallel", "arbitrary"),
        name="gdn_core",
    )(q, k, v, ab, abt, prm_row, prm_col, s0)


def _mlp_kernel(uv_ref, lng_ref, lnb_ref, ws_ref, bst_ref, y_ref, vn_ref, *, rows, lc):
    mdim = MLP_GROUPS * MLP_GROUP_DIM
    ii = lax.broadcasted_iota(jnp.int32, (lc, lc), 0)
    jj = lax.broadcasted_iota(jnp.int32, (lc, lc), 1)
    keep = ii >= jj
    for r in range(rows // lc):
        rs = slice(r * lc, (r + 1) * lc)
        g = _gelu(uv_ref[0, rs, :])
        u = g[:, :mdim]
        vv = g[:, mdim:]
        mu = jnp.mean(vv, axis=-1, keepdims=True)
        xc = vv - mu
        vn = xc * lax.rsqrt(jnp.mean(xc * xc, axis=-1, keepdims=True) + EPS) * lng_ref[...] + lnb_ref[...]
        vn_ref[0, rs, :] = vn
        for gi in range(MLP_GROUPS):
            gs = slice(gi * MLP_GROUP_DIM, (gi + 1) * MLP_GROUP_DIM)
            wm = jnp.where(keep, ws_ref[gi], 0.0)
            s = _dot1(wm, vn[:, gs]) + bst_ref[:, gi:gi + 1]
            y_ref[0, rs, gs] = u[:, gs] * s


def _mlp_branch(proj3, col_block, ln_g, ln_b, w_s, b_s, *, rows, lc):
    b, t, _ = proj3.shape
    mdim = MLP_GROUPS * MLP_GROUP_DIM
    assert t % rows == 0 and rows % lc == 0
    ws = w_s[:, :lc, :lc]
    bst = jnp.transpose(b_s[:, :lc])
    shp = jax.ShapeDtypeStruct((b, t, mdim), F32)
    ospec = pl.BlockSpec((1, rows, mdim), lambda i, j: (i, j, 0))
    return pl.pallas_call(
        functools.partial(_mlp_kernel, rows=rows, lc=lc),
        grid=(b, t // rows),
        in_specs=[pl.BlockSpec((1, rows, 2 * mdim), lambda i, j: (i, j, col_block)),
                  pl.BlockSpec((1, mdim), lambda i, j: (0, 0)),
                  pl.BlockSpec((1, mdim), lambda i, j: (0, 0)),
                  pl.BlockSpec((MLP_GROUPS, lc, lc), lambda i, j: (0, 0, 0)),
                  pl.BlockSpec((lc, MLP_GROUPS), lambda i, j: (0, 0))],
        out_specs=[ospec, ospec],
        out_shape=[shp, shp],
        compiler_params=_cparams("parallel", "parallel"),
        name="mlp_branch",
    )(proj3, ln_g.reshape(1, mdim), ln_b.reshape(1, mdim), ws, bst)


def _outproj_kernel(o_ref, z_ref, y_ref, h_ref, gn_ref, w_ref, out_ref):
    parts = []
    for hh in range(GDN_HEADS):
        sl = slice(hh * GDN_DV, (hh + 1) * GDN_DV)
        parts.append((_rms_rows(o_ref[:, sl], gn_ref[...]) * _silu(z_ref[:, sl])).astype(BF16))
    parts.append(y_ref[...].astype(BF16))
    cat = jnp.concatenate(parts, axis=-1)
    out_ref[...] = h_ref[...] + _dg(cat, w_ref[...], _NN)


def _outproj(o2, proj2, z_block, y2, h2, gdn_norm, w_out, *, tm):
    n, d = h2.shape
    vd = GDN_HEADS * GDN_DV
    assert n % tm == 0
    return pl.pallas_call(
        _outproj_kernel,
        grid=(n // tm,),
        in_specs=[pl.BlockSpec((tm, vd), lambda i: (i, 0)),
                  pl.BlockSpec((tm, vd), lambda i: (i, z_block)),
                  pl.BlockSpec((tm, vd), lambda i: (i, 0)),
                  pl.BlockSpec((tm, d), lambda i: (i, 0)),
                  pl.BlockSpec((1, GDN_DV), lambda i: (0, 0)),
                  pl.BlockSpec(w_out.shape, lambda i: (0, 0))],
        out_specs=pl.BlockSpec((tm, d), lambda i: (i, 0)),
        out_shape=jax.ShapeDtypeStruct((n, d), F32),
        compiler_params=_cparams("parallel"),
        name="out_proj",
    )(o2, proj2, y2, h2, gdn_norm.reshape(1, GDN_DV), w_out.astype(BF16))


def _topk_axis0(x, k, payloads=()):
    r = x.shape[0]
    iota = lax.broadcasted_iota(jnp.int32, x.shape, 0)
    vals, idxs = [], []
    outs = [[] for _ in payloads]
    for _ in range(k):
        m = jnp.max(x, axis=0, keepdims=True)
        i = jnp.min(jnp.where(x == m, iota, r), axis=0, keepdims=True)
        hit = iota == i
        vals.append(m)
        idxs.append(i)
        for p, acc in zip(payloads, outs):
            acc.append(jnp.sum(jnp.where(hit, p, 0), axis=0, keepdims=True))
        x = jnp.where(hit, -jnp.inf, x)
    cat = lambda parts: jnp.concatenate(parts, axis=0)
    return cat(vals), cat(idxs), [cat(acc) for acc in outs]


def _peer_topk_kernel(q_ref, keys_ref, eidx_ref, gate_ref):
    k = PEER_TOPK

    def one_head(h):
        sv, si = [], []
        for s in range(2):
            c0 = pl.multiple_of(h * (2 * PEER_QHALF) + s * PEER_QHALF, PEER_QHALF)
            qhs = q_ref[:, pl.ds(c0, PEER_QHALF)]
            sc_t = _dot3(keys_ref[s, h], qhs, _NT)
            v, i, _ = _topk_axis0(sc_t, k)
            sv.append(v)
            si.append(i)
        half = k // 2
        sub = lax.broadcasted_iota(jnp.int32, (half, sv[0].shape[1]), 0)
        cand_parts = [sv[0][0:1, :] + sv[1]]
        cidx_parts = [si[0][0:1, :] * N_KEYS + si[1]]
        for a in range(1, half):
            keep = sub < (k // (a + 1))
            cand_parts.append(jnp.where(keep, sv[0][a:a + 1, :] + sv[1][0:half, :], -jnp.inf))
            cidx_parts.append(si[0][a:a + 1, :] * N_KEYS + si[1][0:half, :])
        cand_parts.append(sv[0][half:, :] + sv[1][0:1, :])
        cidx_parts.append(si[0][half:, :] * N_KEYS + si[1][0:1, :])
        cand = jnp.concatenate(cand_parts, axis=0)
        cidx = jnp.concatenate(cidx_parts, axis=0)
        fv, _, (fe,) = _topk_axis0(cand, k, (cidx,))
        e = jnp.exp(fv - fv[0:1, :])
        gate = e / jnp.sum(e, axis=0, keepdims=True)
        r0 = pl.multiple_of(h * k, k)
        eidx_ref[pl.ds(r0, k), :] = fe
        gate_ref[pl.ds(r0, k), :] = gate

    def heads_body(hp, carry):
        for j in range(PEER_TOPK_HEADS_PER_ITER):
            one_head(hp * PEER_TOPK_HEADS_PER_ITER + j)
        return carry

    lax.fori_loop(0, PEER_HEADS // PEER_TOPK_HEADS_PER_ITER, heads_body, 0)


def _peer_topk(q, sub_keys, *, tk):
    n, qd = q.shape
    assert n % tk == 0
    rows = PEER_HEADS * PEER_TOPK
    return pl.pallas_call(
        _peer_topk_kernel,
        grid=(n // tk,),
        in_specs=[pl.BlockSpec((tk, qd), lambda i: (i, 0)),
                  pl.BlockSpec(sub_keys.shape, lambda i: (0, 0, 0, 0))],
        out_specs=[pl.BlockSpec((rows, tk), lambda i: (0, i)),
                   pl.BlockSpec((rows, tk), lambda i: (0, i))],
        out_shape=[jax.ShapeDtypeStruct((rows, n), jnp.int32),
                   jax.ShapeDtypeStruct((rows, n), F32)],
        compiler_params=_cparams("parallel"),
        name="peer_topk",
    )(q, sub_keys)


PEER_ROWS = PEER_HEADS * PEER_TOPK
PEER_SLOTS = 16
PEER_LOOK = PEER_SLOTS - 1
assert PEER_SLOTS % 2 == 0 and PEER_LOOK % 2 == 1
PEER_IDX_TAIL = 16
assert PEER_LOOK <= PEER_IDX_TAIL and (PEER_IDX_TAIL * PEER_ROWS) % 1024 == 0
PEER_ACCS = 4
PEER_DOT_COPIES = 5
PEER_MIX_COPIES = 3
assert (PEER_DOT_COPIES + PEER_MIX_COPIES) * (PEER_ROWS // SUBLANES) == PEER_ROWS


def _sublane_fold(parts):
    sub = lax.broadcasted_iota(jnp.int32, parts[0].shape, 0)
    for dist in (4, 2, 1):
        low = (sub & dist) == 0
        nxt = []
        for a in range(len(parts) // 2):
            lo_src, hi_src = parts[a], parts[a + len(parts) // 2]
            kept = jnp.where(low, lo_src, hi_src)
            moved = jnp.where(low, pltpu.roll(lo_src, SUBLANES - dist, axis=0), pltpu.roll(hi_src, dist, axis=0))
            nxt.append(kept + moved)
        parts = nxt
    return parts[0]


def _peer_gather_kernel(idx_ref, x_ref, gate_ref, h_ref, tab_ref, out_ref, buf_even, buf_odd, sem_ref, coef_ref, *, tb):
    rows = PEER_ROWS
    nc = x_ref.shape[1] // LANES
    half = nc // 2
    groups = rows // SUBLANES
    step = pl.program_id(0)
    hi_mask = jnp.uint32(0xFFFF0000)
    bufs = (buf_even, buf_odd)

    def slot_of(tok):
        return (tok // 2) % (PEER_SLOTS // 2)

    def start_rows(arr, tok, r0, count):
        for j in range(count):
            r = r0 + j
            pltpu.make_async_copy(tab_ref.at[idx_ref[tok * rows + r]], arr.at[slot_of(tok), r],
                                  sem_ref.at[tok % PEER_SLOTS]).start(priority=j % 2)

    def wait_token(arr, tok):
        pltpu.make_async_copy(tab_ref.at[pl.ds(0, rows)], arr.at[slot_of(tok)], sem_ref.at[tok % PEER_SLOTS]).wait()

    def load_x(t):
        xrow = x_ref[pl.ds(t, 1), :]
        x_lo = jnp.concatenate([xrow[:, c * LANES:(c + 1) * LANES] for c in range(half)], axis=0)
        x_hi = jnp.concatenate([xrow[:, c * LANES:(c + 1) * LANES] for c in range(half, nc)], axis=0)
        return x_lo, x_hi

    def dot_group(buf, x, g):
        parts = []
        for s in range(SUBLANES):
            r = g * SUBLANES + s
            u_lo = lax.bitcast_convert_type(buf[r, 0:half, :] << 16, F32)
            u_hi = lax.bitcast_convert_type(buf[r, half:nc, :] << 16, F32)
            parts.append(u_lo * x[0] + u_hi * x[1])
        return _sublane_fold(parts)

    def finish_dot(t, folded):
        act = jnp.sum(jnp.concatenate(folded, axis=0), axis=-1, keepdims=True)
        lane_tok = lax.broadcasted_iota(jnp.int32, (rows, tb), 1)
        gcol = jnp.sum(jnp.where(lane_tok == t, gate_ref[0], 0.0), axis=-1, keepdims=True)
        coef_ref[...] = jnp.broadcast_to(gcol * _gelu(act), (rows, LANES))

    def mix_group(buf, g, acc_lo, acc_hi):
        for s in range(SUBLANES):
            r = g * SUBLANES + s
            c = jnp.broadcast_to(coef_ref[r:r + 1, :], (half, LANES))
            a = r % PEER_ACCS
            acc_lo[a] = acc_lo[a] + lax.bitcast_convert_type(buf[r, 0:half, :] & hi_mask, F32) * c
            acc_hi[a] = acc_hi[a] + lax.bitcast_convert_type(buf[r, half:nc, :] & hi_mask, F32) * c

    def emit(t, acc_lo, acc_hi):
        mix_lo = (acc_lo[0] + acc_lo[1]) + (acc_lo[2] + acc_lo[3])
        mix_hi = (acc_hi[0] + acc_hi[1]) + (acc_hi[2] + acc_hi[3])
        mix_row = jnp.concatenate([mix_lo[c:c + 1, :] for c in range(half)]
                                  + [mix_hi[c:c + 1, :] for c in range(half)], axis=1)
        out_ref[pl.ds(t, 1), :] = h_ref[pl.ds(t, 1), :] + mix_row

    def token(t, parity):
        arr, arr_next = bufs[parity], bufs[(parity + PEER_LOOK) % 2]
        tok_next = t + PEER_LOOK
        wait_token(arr, t)
        buf = arr.at[slot_of(t)]
        x = load_x(t)
        folded = []
        for g in range(groups):
            start_rows(arr_next, tok_next, g * PEER_DOT_COPIES, PEER_DOT_COPIES)
            folded.append(dot_group(buf, x, g))
        finish_dot(t, folded)
        acc_lo = [jnp.zeros((half, LANES), F32) for _ in range(PEER_ACCS)]
        acc_hi = [jnp.zeros((half, LANES), F32) for _ in range(PEER_ACCS)]
        for g in range(groups):
            start_rows(arr_next, tok_next, groups * PEER_DOT_COPIES + g * PEER_MIX_COPIES, PEER_MIX_COPIES)
            mix_group(buf, g, acc_lo, acc_hi)
        emit(t, acc_lo, acc_hi)

    @pl.when(step == 0)
    def _():
        for s in range(PEER_LOOK):
            def prime(i, carry, s=s):
                start_rows(bufs[s % 2], s, i * SUBLANES, SUBLANES)
                return carry
            lax.fori_loop(0, groups, prime, 0)

    def pair(p, carry):
        token(2 * p, 0)
        token(2 * p + 1, 1)
        return carry

    lax.fori_loop(0, tb // 2, pair, 0)

    @pl.when(step == pl.num_programs(0) - 1)
    def _():
        for s in range(PEER_LOOK):
            wait_token(bufs[(tb + s) % 2], tb + s)


def _peer_gather(eidx, xn, gate_blocks, h, table, *, tb):
    n, d = h.shape
    rows = PEER_ROWS
    assert n % tb == 0 and tb % PEER_SLOTS == 0 and PEER_ACCS == 4
    nb = n // tb
    idx2 = eidx.reshape(nb, tb * rows)
    idx_ext = jnp.concatenate([idx2, jnp.roll(idx2[:, :PEER_IDX_TAIL * rows], -1, axis=0)], axis=1).reshape(-1)
    return pl.pallas_call(
        functools.partial(_peer_gather_kernel, tb=tb),
        grid=(nb,),
        in_specs=[pl.BlockSpec(((tb + PEER_IDX_TAIL) * rows,), lambda i: (i,), memory_space=pltpu.SMEM),
                  pl.BlockSpec((tb, d), lambda i: (i, 0)),
                  pl.BlockSpec((1, rows, tb), lambda i: (i, 0, 0)),
                  pl.BlockSpec((tb, d), lambda i: (i, 0)),
                  pl.BlockSpec(memory_space=pl.ANY)],
        out_specs=pl.BlockSpec((tb, d), lambda i: (i, 0)),
        out_shape=jax.ShapeDtypeStruct((n, d), F32),
        scratch_shapes=[pltpu.VMEM((PEER_SLOTS // 2, rows, d // LANES, LANES), jnp.uint32),
                        pltpu.VMEM((PEER_SLOTS // 2, rows, d // LANES, LANES), jnp.uint32),
                        pltpu.SemaphoreType.DMA((PEER_SLOTS,)),
                        pltpu.VMEM((rows, LANES), F32)],
        compiler_params=_cparams("arbitrary"),
        name="peer_gather",
    )(idx_ext, xn, gate_blocks, h, table)


def _ple_kernel(h_ref, p_ref, n3_ref, fn_ref, wg_ref, wp_ref, y_ref):
    h = h_ref[...]
    gate = _sigmoid(_dot1(_rms_rows(h, n3_ref[...]), wg_ref[...]))
    h = h + gate * _dot1(p_ref[...], wp_ref[...])
    y_ref[...] = _rms_rows(h, fn_ref[...])


def _ple_final(h, p, norm3, final_norm, w_gate, w_proj, *, tm):
    n, d = h.shape
    pd = p.shape[1]
    assert n % tm == 0
    return pl.pallas_call(
        _ple_kernel,
        grid=(n // tm,),
        in_specs=[pl.BlockSpec((tm, d), lambda i: (i, 0)),
                  pl.BlockSpec((tm, pd), lambda i: (i, 0)),
                  pl.BlockSpec((1, d), lambda i: (0, 0)),
                  pl.BlockSpec((1, d), lambda i: (0, 0)),
                  pl.BlockSpec((d, d), lambda i: (0, 0)),
                  pl.BlockSpec((pd, d), lambda i: (0, 0))],
        out_specs=pl.BlockSpec((tm, d), lambda i: (i, 0)),
        out_shape=jax.ShapeDtypeStruct((n, d), F32),
        compiler_params=_cparams("parallel"),
        name="ple_final",
    )(h, p, norm3.reshape(1, d), final_norm.reshape(1, d), w_gate.astype(BF16), w_proj.astype(BF16))


QK_DIM = GDN_HEADS * GDN_DK
V_DIM = GDN_HEADS * GDN_DV
CONV_DIM = 2 * QK_DIM + V_DIM
MLP_DIM = MLP_GROUPS * MLP_GROUP_DIM
Z_COL = CONV_DIM
UV_COL = Z_COL + V_DIM
AB_COL = UV_COL + 2 * MLP_DIM
PROJ_COLS = AB_COL + LANES
PROJ_TN = 896


def _relayout_w_in(w_in):
    c0 = CONV_DIM
    c2 = c0 + 2 * GDN_HEADS
    c3 = c2 + V_DIM
    ab = jnp.pad(w_in[:, c0:c2], ((0, 0), (0, LANES - 2 * GDN_HEADS)))
    return jnp.concatenate([w_in[:, :c0], w_in[:, c2:c3], w_in[:, c3:], ab], axis=1)


def _pack_tables(peer_u, peer_v):
    ub = lax.bitcast_convert_type(peer_u.astype(BF16), jnp.uint16).astype(jnp.uint32)
    vb = lax.bitcast_convert_type(peer_v.astype(BF16), jnp.uint16).astype(jnp.uint32)
    e, d = peer_u.shape
    return (ub | (vb << 16)).reshape(e, d // LANES, LANES)


def _row_tile(n, want):
    t = min(n, want)
    assert n % t == 0
    return t


def _layer(x, p, s0, buf, w, final_norm):
    b, t, d = x.shape
    n = b * t
    x2 = x.reshape(n, d)
    tm = _row_tile(n, 1024)

    proj = _norm_matmul(x2, w["norm1"], w["w_in"], tm=tm, tn=PROJ_TN, name="in_proj")
    proj3 = proj.reshape(b, t, PROJ_COLS)
    new_buf = jnp.concatenate([buf, proj3[:, :, :CONV_DIM]], axis=1)[:, t:]

    buf8 = jnp.pad(buf, ((0, 0), (SUBLANES - (CONV_W - 1), 0), (0, 0)))
    q, k, v = _conv_qkv(proj3, buf8, w["conv_w"], tt=_row_tile(t, 256))

    tg = -(-t // GDN_CHUNK) * GDN_CHUNK
    ab3 = proj3[:, :, AB_COL:]
    if tg != t:
        padt = ((0, 0), (0, tg - t), (0, 0))
        q, k, v, ab3 = (jnp.pad(a, padt) for a in (q, k, v, ab3))
    abt = jnp.transpose(ab3[:, :, :2 * GDN_HEADS].reshape(b, tg // GDN_CHUNK, GDN_CHUNK, 2 * GDN_HEADS),
                        (0, 1, 3, 2))
    o, s_new = _gdn(q, k, v, ab3, abt, w["a_log"], w["dt_bias"], s0, tt=_row_tile(tg, 512), t_valid=t)
    o = o[:, :t]

    lc = min(t, MLP_CHUNK)
    y_b, vn = _mlp_branch(proj3, UV_COL // (2 * MLP_DIM), w["ln_g"], w["ln_b"], w["w_s"], w["b_s"],
                          rows=_row_tile(t, 512), lc=lc)

    h1 = _outproj(o.reshape(n, V_DIM), proj, Z_COL // V_DIM, y_b.reshape(n, MLP_DIM), x2,
                  w["gdn_norm"], w["w_out"], tm=_row_tile(n, 256))

    qp, xn2 = _norm_matmul(h1, w["norm2"], w["w_q"], tm=tm, tn=512, passes=1, emit_xn=True, name="peer_query")
    tb = 128
    eidx_t, gate_t = _peer_topk(qp, w["sub_keys"], tk=tb)
    rows = PEER_HEADS * PEER_TOPK
    gate_blocks = jnp.transpose(gate_t.reshape(rows, n // tb, tb), (1, 0, 2))
    h2 = _peer_gather(jnp.transpose(eidx_t), xn2, gate_blocks, h1, w["table"], tb=tb)

    y = _ple_final(h2, p.reshape(n, -1), w["norm3"], final_norm, w["w_ple_gate"], w["w_ple_proj"],
                   tm=_row_tile(n, 256))
    return y.reshape(b, t, d), s_new, new_buf, vn


def kernel(x_prompt, x_sample, state_gdn, cache_conv, p_prompt, p_sample, norm1, w_in, conv_w, a_log, dt_bias,
           gdn_norm, ln_g, ln_b, w_s, b_s, w_out, norm2, w_q, sub_keys, peer_u, peer_v, norm3, w_ple_gate,
           w_ple_proj, final_norm):
    assert norm1.shape[0] == 1, "single layer"
    w = dict(norm1=norm1[0], w_in=_relayout_w_in(w_in[0]), conv_w=conv_w[0], a_log=a_log[0], dt_bias=dt_bias[0],
             gdn_norm=gdn_norm[0], ln_g=ln_g[0], ln_b=ln_b[0], w_s=w_s[0], b_s=b_s[0], w_out=w_out[0],
             norm2=norm2[0], w_q=w_q[0], sub_keys=sub_keys[0], table=_pack_tables(peer_u[0], peer_v[0]),
             norm3=norm3[0], w_ple_gate=w_ple_gate[0], w_ple_proj=w_ple_proj[0])
    bp = x_prompt.shape[0]
    s0p = jnp.zeros((bp, GDN_HEADS, GDN_DK, GDN_DV), F32)
    b0p = jnp.zeros((bp, CONV_W - 1, CONV_DIM), F32)
    y_s, s_s, b_s_new, v_s = _layer(x_sample, p_sample[0], state_gdn[0], cache_conv[0], w, final_norm)
    y_p, s_p, b_p, _ = _layer(x_prompt, p_prompt[0], s0p, b0p, w, final_norm)
    return (y_p, y_s, s_p[None], b_p[None], s_s[None], b_s_new[None], v_s[None])
```

```python
import functools
import math

import jax
import jax.numpy as jnp
from jax import lax
from jax.experimental import pallas as pl
from jax.experimental.pallas import tpu as pltpu

F32 = jnp.float32
BF16 = jnp.bfloat16

EPS = 1e-6
GDN_CHUNK = 64
GDN_HEADS = 8
GDN_HEAD_GROUP = 8
GDN_DK = 128
GDN_DV = 128
CONV_W = 4
MLP_CHUNK = 128
MLP_GROUPS = 8
MLP_GROUP_DIM = 128
PEER_HEADS = 8
PEER_QHALF = 128
N_KEYS = 128
PEER_TOPK = 16
PEER_TOPK_HEADS_PER_ITER = 8

LANES = 128
SUBLANES = 8
VMEM_LIMIT_BYTES = 56 * 1024 * 1024


def _cparams(*sem):
    return pltpu.CompilerParams(dimension_semantics=sem, vmem_limit_bytes=VMEM_LIMIT_BYTES)


def _split3(x):
    hi = x.astype(BF16)
    r1 = x - hi.astype(F32)
    mid = r1.astype(BF16)
    lo = (r1 - mid.astype(F32)).astype(BF16)
    return hi, mid, lo


def _dg(a, b, dims):
    return lax.dot_general(a, b, (dims, ((), ())), preferred_element_type=F32)


_NN = ((1,), (0,))
_NT = ((1,), (1,))
_TN = ((0,), (0,))


def _dot1(a, b, dims=_NN):
    return _dg(a.astype(BF16), b.astype(BF16), dims)


def _dot3(a, b, dims=_NN):
    ah = a.astype(BF16)
    al = (a - ah.astype(F32)).astype(BF16)
    bh = b.astype(BF16)
    bl = (b - bh.astype(F32)).astype(BF16)
    return _dg(ah, bh, dims) + (_dg(ah, bl, dims) + _dg(al, bh, dims))


def _dot_exact_lhs(a_exact_bf16, b, dims=_NN):
    b0, b1, b2 = _split3(b)
    return _dg(a_exact_bf16, b0, dims) + (_dg(a_exact_bf16, b1, dims) + _dg(a_exact_bf16, b2, dims))


def _dot_exact_rhs(a, b_exact_bf16, dims=_NN):
    a0, a1, a2 = _split3(a)
    return _dg(a0, b_exact_bf16, dims) + (_dg(a1, b_exact_bf16, dims) + _dg(a2, b_exact_bf16, dims))


def _rms_rows(x, gain):
    ms = jnp.mean(x * x, axis=-1, keepdims=True)
    return x * lax.rsqrt(ms + EPS) * gain


def _sigmoid(x):
    return 1.0 / (1.0 + jnp.exp(-x))


def _silu(x):
    return x * _sigmoid(x)


def _gelu(x):
    return 0.5 * x * (1.0 + lax.erf(x * (1.0 / math.sqrt(2.0))))


def _softplus(x):
    return jnp.maximum(x, 0.0) + jnp.log1p(jnp.exp(-jnp.abs(x)))


def _norm_matmul_kernel(x_ref, g_ref, *refs, passes, emit_xn):
    if passes == 1:
        w_refs, rest = refs[:1], refs[1:]
    else:
        w_refs, rest = refs[:2], refs[2:]
    if emit_xn:
        o_ref, xn_out_ref = rest[0], rest[1]
        scr = rest[2:]
    else:
        o_ref = rest[0]
        scr = rest[1:]
    j = pl.program_id(1)

    @pl.when(j == 0)
    def _():
        xn = _rms_rows(x_ref[...], g_ref[...])
        if emit_xn:
            xn_out_ref[...] = xn
        hi = xn.astype(BF16)
        scr[0][...] = hi
        if passes == 3:
            scr[1][...] = (xn - hi.astype(F32)).astype(BF16)

    if passes == 1:
        o_ref[...] = _dg(scr[0][...], w_refs[0][...], _NN)
    else:
        xh = scr[0][...]
        wh = w_refs[0][...]
        o_ref[...] = _dg(xh, wh, _NN) + (_dg(xh, w_refs[1][...], _NN) + _dg(scr[1][...], wh, _NN))


def _norm_matmul(x, gain, w, *, tm, tn, passes=1, emit_xn=False, name):
    n, d = x.shape
    m = w.shape[1]
    assert n % tm == 0 and m % tn == 0
    wh = w.astype(BF16)
    ws = [wh] if passes == 1 else [wh, (w - wh.astype(F32)).astype(BF16)]
    w_specs = [pl.BlockSpec((d, tn), lambda i, j: (0, j)) for _ in ws]
    out_shape = [jax.ShapeDtypeStruct((n, m), F32)]
    out_specs = [pl.BlockSpec((tm, tn), lambda i, j: (i, j))]
    if emit_xn:
        out_shape.append(jax.ShapeDtypeStruct((n, d), F32))
        out_specs.append(pl.BlockSpec((tm, d), lambda i, j: (i, 0)))
    scratch = [pltpu.VMEM((tm, d), BF16) for _ in range(1 if passes == 1 else 2)]
    res = pl.pallas_call(
        functools.partial(_norm_matmul_kernel, passes=passes, emit_xn=emit_xn),
        grid=(n // tm, m // tn),
        in_specs=[pl.BlockSpec((tm, d), lambda i, j: (i, 0)),
                  pl.BlockSpec((1, d), lambda i, j: (0, 0))] + w_specs,
        out_specs=out_specs,
        out_shape=out_shape,
        scratch_shapes=scratch,
        compiler_params=_cparams("parallel", "arbitrary"),
        name=name,
    )(x, gain.reshape(1, d), *ws)
    return res if emit_xn else res[0]


def _conv_kernel(x_ref, buf_ref, w_ref, q_ref, k_ref, v_ref, xp_ref, *, tt):
    t = pl.program_id(1)
    halo = SUBLANES

    @pl.when(t == 0)
    def _():
        xp_ref[0:halo, :] = buf_ref[0]

    @pl.when(t > 0)
    def _():
        xp_ref[0:halo, :] = xp_ref[tt:tt + halo, :]

    xp_ref[halo:halo + tt, :] = x_ref[0]
    base = halo - (CONV_W - 1)
    y = xp_ref[base:base + tt, :] * w_ref[0:1, :]
    for j in range(1, CONV_W):
        y = y + xp_ref[base + j:base + j + tt, :] * w_ref[j:j + 1, :]
    y = _silu(y)
    qk_dim = GDN_HEADS * GDN_DK
    for h in range(GDN_HEADS):
        qh = y[:, h * GDN_DK:(h + 1) * GDN_DK]
        qn = qh * lax.rsqrt(jnp.sum(qh * qh, axis=-1, keepdims=True) + EPS)
        q_ref[0, :, h * GDN_DK:(h + 1) * GDN_DK] = qn * (GDN_DK ** -0.5)
        kh = y[:, qk_dim + h * GDN_DK:qk_dim + (h + 1) * GDN_DK]
        k_ref[0, :, h * GDN_DK:(h + 1) * GDN_DK] = kh * lax.rsqrt(jnp.sum(kh * kh, axis=-1, keepdims=True) + EPS)
    v_ref[0] = y[:, 2 * qk_dim:]


def _conv_qkv(proj3, buf8, conv_w, *, tt):
    b, t, _ = proj3.shape
    cdim = conv_w.shape[1]
    hd = GDN_HEADS * GDN_DK
    assert t % tt == 0
    shp = jax.ShapeDtypeStruct((b, t, hd), F32)
    ospec = pl.BlockSpec((1, tt, hd), lambda i, j: (i, j, 0))
    return pl.pallas_call(
        functools.partial(_conv_kernel, tt=tt),
        grid=(b, t // tt),
        in_specs=[pl.BlockSpec((1, tt, cdim), lambda i, j: (i, j, 0)),
                  pl.BlockSpec((1, SUBLANES, cdim), lambda i, j: (i, 0, 0)),
                  pl.BlockSpec((CONV_W, cdim), lambda i, j: (0, 0))],
        out_specs=[ospec, ospec, ospec],
        out_shape=[shp, shp, shp],
        scratch_shapes=[pltpu.VMEM((tt + 2 * SUBLANES, cdim), F32)],
        compiler_params=_cparams("parallel", "arbitrary"),
        name="conv_qkv",
    )(proj3, buf8, conv_w)


def _gdn_kernel(q_ref, k_ref, v_ref, ab_ref, abt_ref, prm_row_ref, prm_col_ref, s0_ref,
                o_ref, sout_ref, state_ref, *, tt, t_valid):
    tstep = pl.program_id(1)
    c = GDN_CHUNK

    @pl.when(tstep == 0)
    def _():
        state_ref[...] = s0_ref[0]

    ii = lax.broadcasted_iota(jnp.int32, (c, c), 0)
    jj = lax.broadcasted_iota(jnp.int32, (c, c), 1)
    incl = ii >= jj
    strict = ii > jj
    tri = incl.astype(BF16)
    tri_t = (jj >= ii).astype(BF16)
    eye = (ii == jj).astype(F32)

    alog_row = prm_row_ref[0:1, :]
    dtb_row = prm_row_ref[1:2, :]
    alog_col = prm_col_ref[:, 0:1]
    dtb_col = prm_col_ref[:, 1:2]

    def chunk_body(ci, carry):
        r0 = pl.multiple_of(ci * c, c)
        tpos = tstep * tt + r0
        a_blk = ab_ref[0, pl.ds(r0, c), :]
        valid_col = (tpos + lax.broadcasted_iota(jnp.int32, (c, LANES), 0)) < t_valid
        g_col = jnp.where(valid_col, -jnp.exp(alog_row) * _softplus(a_blk + dtb_row), 0.0)
        beta_col = jnp.where(valid_col, _sigmoid(a_blk), 0.0)
        gc_col = _dot_exact_lhs(tri, g_col)
        at_blk = abt_ref[0, ci]
        valid_row = (tpos + lax.broadcasted_iota(jnp.int32, (2 * GDN_HEADS, c), 1)) < t_valid
        g_row = jnp.where(valid_row, -jnp.exp(alog_col) * _softplus(at_blk + dtb_col), 0.0)
        gc_row = _dot_exact_rhs(g_row, tri_t)

        for h0 in range(0, GDN_HEADS, GDN_HEAD_GROUP):
            heads = range(h0, h0 + GDN_HEAD_GROUP)
            st = {}
            for h in heads:
                sl = slice(h * GDN_DK, (h + 1) * GDN_DK)
                qh = q_ref[0, pl.ds(r0, c), sl]
                kh = k_ref[0, pl.ds(r0, c), sl]
                vh = v_ref[0, pl.ds(r0, c), sl]
                gc = gc_col[:, h:h + 1]
                beta = beta_col[:, GDN_HEADS + h:GDN_HEADS + h + 1]
                gcr = gc_row[h:h + 1, :]
                gc_last = gc_col[c - 1:c, h:h + 1]
                decay = jnp.where(incl, jnp.exp(jnp.where(incl, gc - gcr, 0.0)), 0.0)
                egc = jnp.exp(gc)
                kb = kh * beta
                lmat = jnp.where(strict, _dot1(kb, kh, _NT) * decay, 0.0)
                st[h] = dict(
                    sl=sl, tinv=eye - lmat, pw=lmat,
                    rhs=jnp.concatenate([vh * beta, kb * egc], axis=-1),
                    attn=_dot1(qh, kh, _NT) * decay, q_dec=qh * egc,
                    k_dec=kh * jnp.exp(gc_last - gc), bd=jnp.exp(gc_last))
            for _ in range(5):
                for h in heads:
                    st[h]["pw"] = _dot3(st[h]["pw"], st[h]["pw"])
                for h in heads:
                    st[h]["tinv"] = st[h]["tinv"] + _dot3(st[h]["tinv"], st[h]["pw"])
            for h in heads:
                st[h]["sol"] = _dot3(st[h]["tinv"], st[h]["rhs"])
            for h in heads:
                s = state_ref[h]
                st[h]["s"] = s
                st[h]["v_new"] = st[h]["sol"][:, :GDN_DV] - _dot1(st[h]["sol"][:, GDN_DV:], s)
            for h in heads:
                d = st[h]
                o_ref[0, pl.ds(r0, c), d["sl"]] = _dot1(d["q_dec"], d["s"]) + _dot1(d["attn"], d["v_new"])
                state_ref[h] = d["s"] * d["bd"] + _dot1(d["k_dec"], d["v_new"], _TN)
        return carry

    lax.fori_loop(0, tt // c, chunk_body, 0)

    @pl.when(tstep == pl.num_programs(1) - 1)
    def _():
        sout_ref[0] = state_ref[...]


def _gdn(q, k, v, ab, abt, a_log, dt_bias, s0, *, tt, t_valid):
    b, t, hd = q.shape
    assert t % tt == 0 and tt % GDN_CHUNK == 0
    prm_row = jnp.zeros((SUBLANES, LANES), F32)
    prm_row = prm_row.at[0, :GDN_HEADS].set(a_log).at[1, :GDN_HEADS].set(dt_bias)
    prm_col = jnp.zeros((2 * GDN_HEADS, LANES), F32)
    prm_col = prm_col.at[:GDN_HEADS, 0].set(a_log).at[:GDN_HEADS, 1].set(dt_bias)
    seq = pl.BlockSpec((1, tt, hd), lambda i, j: (i, j, 0))
    st = pl.BlockSpec((1, GDN_HEADS, GDN_DK, GDN_DV), lambda i, j: (i, 0, 0, 0))
    return pl.pallas_call(
        functools.partial(_gdn_kernel, tt=tt, t_valid=t_valid),
        grid=(b, t // tt),
        in_specs=[seq, seq, seq,
                  pl.BlockSpec((1, tt, LANES), lambda i, j: (i, j, 0)),
                  pl.BlockSpec((1, tt // GDN_CHUNK, 2 * GDN_HEADS, GDN_CHUNK), lambda i, j: (i, j, 0, 0)),
                  pl.BlockSpec((SUBLANES, LANES), lambda i, j: (0, 0)),
                  pl.BlockSpec((2 * GDN_HEADS, LANES), lambda i, j: (0, 0)),
                  st],
        out_specs=[seq, st],
        out_shape=[jax.ShapeDtypeStruct((b, t, hd), F32),
                   jax.ShapeDtypeStruct((b, GDN_HEADS, GDN_DK, GDN_DV), F32)],
        scratch_shapes=[pltpu.VMEM((GDN_HEADS, GDN_DK, GDN_DV), F32)],
        compiler_params=_cparams("parallel", "arbitrary"),
        name="gdn_core",
    )(q, k, v, ab, abt, prm_row, prm_col, s0)


def _mlp_kernel(uv_ref, lng_ref, lnb_ref, ws_ref, bst_ref, y_ref, vn_ref, *, rows, lc):
    mdim = MLP_GROUPS * MLP_GROUP_DIM
    ii = lax.broadcasted_iota(jnp.int32, (lc, lc), 0)
    jj = lax.broadcasted_iota(jnp.int32, (lc, lc), 1)
    keep = ii >= jj
    for r in range(rows // lc):
        rs = slice(r * lc, (r + 1) * lc)
        g = _gelu(uv_ref[0, rs, :])
        u = g[:, :mdim]
        vv = g[:, mdim:]
        mu = jnp.mean(vv, axis=-1, keepdims=True)
        xc = vv - mu
        vn = xc * lax.rsqrt(jnp.mean(xc * xc, axis=-1, keepdims=True) + EPS) * lng_ref[...] + lnb_ref[...]
        vn_ref[0, rs, :] = vn
        for gi in range(MLP_GROUPS):
            gs = slice(gi * MLP_GROUP_DIM, (gi + 1) * MLP_GROUP_DIM)
            wm = jnp.where(keep, ws_ref[gi], 0.0)
            s = _dot1(wm, vn[:, gs]) + bst_ref[:, gi:gi + 1]
            y_ref[0, rs, gs] = u[:, gs] * s


def _mlp_branch(proj3, col_block, ln_g, ln_b, w_s, b_s, *, rows, lc):
    b, t, _ = proj3.shape
    mdim = MLP_GROUPS * MLP_GROUP_DIM
    assert t % rows == 0 and rows % lc == 0
    ws = w_s[:, :lc, :lc]
    bst = jnp.transpose(b_s[:, :lc])
    shp = jax.ShapeDtypeStruct((b, t, mdim), F32)
    ospec = pl.BlockSpec((1, rows, mdim), lambda i, j: (i, j, 0))
    return pl.pallas_call(
        functools.partial(_mlp_kernel, rows=rows, lc=lc),
        grid=(b, t // rows),
        in_specs=[pl.BlockSpec((1, rows, 2 * mdim), lambda i, j: (i, j, col_block)),
                  pl.BlockSpec((1, mdim), lambda i, j: (0, 0)),
                  pl.BlockSpec((1, mdim), lambda i, j: (0, 0)),
                  pl.BlockSpec((MLP_GROUPS, lc, lc), lambda i, j: (0, 0, 0)),
                  pl.BlockSpec((lc, MLP_GROUPS), lambda i, j: (0, 0))],
        out_specs=[ospec, ospec],
        out_shape=[shp, shp],
        compiler_params=_cparams("parallel", "parallel"),
        name="mlp_branch",
    )(proj3, ln_g.reshape(1, mdim), ln_b.reshape(1, mdim), ws, bst)


def _outproj_kernel(o_ref, z_ref, y_ref, h_ref, gn_ref, w_ref, out_ref):
    parts = []
    for hh in range(GDN_HEADS):
        sl = slice(hh * GDN_DV, (hh + 1) * GDN_DV)
        parts.append((_rms_rows(o_ref[:, sl], gn_ref[...]) * _silu(z_ref[:, sl])).astype(BF16))
    parts.append(y_ref[...].astype(BF16))
    cat = jnp.concatenate(parts, axis=-1)
    out_ref[...] = h_ref[...] + _dg(cat, w_ref[...], _NN)


def _outproj(o2, proj2, z_block, y2, h2, gdn_norm, w_out, *, tm):
    n, d = h2.shape
    vd = GDN_HEADS * GDN_DV
    assert n % tm == 0
    return pl.pallas_call(
        _outproj_kernel,
        grid=(n // tm,),
        in_specs=[pl.BlockSpec((tm, vd), lambda i: (i, 0)),
                  pl.BlockSpec((tm, vd), lambda i: (i, z_block)),
                  pl.BlockSpec((tm, vd), lambda i: (i, 0)),
                  pl.BlockSpec((tm, d), lambda i: (i, 0)),
                  pl.BlockSpec((1, GDN_DV), lambda i: (0, 0)),
                  pl.BlockSpec(w_out.shape, lambda i: (0, 0))],
        out_specs=pl.BlockSpec((tm, d), lambda i: (i, 0)),
        out_shape=jax.ShapeDtypeStruct((n, d), F32),
        compiler_params=_cparams("parallel"),
        name="out_proj",
    )(o2, proj2, y2, h2, gdn_norm.reshape(1, GDN_DV), w_out.astype(BF16))


def _topk_axis0(x, k, payloads=()):
    r = x.shape[0]
    iota = lax.broadcasted_iota(jnp.int32, x.shape, 0)
    vals, idxs = [], []
    outs = [[] for _ in payloads]
    for _ in range(k):
        m = jnp.max(x, axis=0, keepdims=True)
        i = jnp.min(jnp.where(x == m, iota, r), axis=0, keepdims=True)
        hit = iota == i
        vals.append(m)
        idxs.append(i)
        for p, acc in zip(payloads, outs):
            acc.append(jnp.sum(jnp.where(hit, p, 0), axis=0, keepdims=True))
        x = jnp.where(hit, -jnp.inf, x)
    cat = lambda parts: jnp.concatenate(parts, axis=0)
    return cat(vals), cat(idxs), [cat(acc) for acc in outs]


def _peer_topk_kernel(q_ref, keys_ref, eidx_ref, gate_ref):
    k = PEER_TOPK

    def one_head(h):
        sv, si = [], []
        for s in range(2):
            c0 = pl.multiple_of(h * (2 * PEER_QHALF) + s * PEER_QHALF, PEER_QHALF)
            qhs = q_ref[:, pl.ds(c0, PEER_QHALF)]
            sc_t = _dot3(keys_ref[s, h], qhs, _NT)
            v, i, _ = _topk_axis0(sc_t, k)
            sv.append(v)
            si.append(i)
        half = k // 2
        sub = lax.broadcasted_iota(jnp.int32, (half, sv[0].shape[1]), 0)
        cand_parts = [sv[0][0:1, :] + sv[1]]
        cidx_parts = [si[0][0:1, :] * N_KEYS + si[1]]
        for a in range(1, half):
            keep = sub < (k // (a + 1))
            cand_parts.append(jnp.where(keep, sv[0][a:a + 1, :] + sv[1][0:half, :], -jnp.inf))
            cidx_parts.append(si[0][a:a + 1, :] * N_KEYS + si[1][0:half, :])
        cand_parts.append(sv[0][half:, :] + sv[1][0:1, :])
        cidx_parts.append(si[0][half:, :] * N_KEYS + si[1][0:1, :])
        cand = jnp.concatenate(cand_parts, axis=0)
        cidx = jnp.concatenate(cidx_parts, axis=0)
        fv, _, (fe,) = _topk_axis0(cand, k, (cidx,))
        e = jnp.exp(fv - fv[0:1, :])
        gate = e / jnp.sum(e, axis=0, keepdims=True)
        r0 = pl.multiple_of(h * k, k)
        eidx_ref[pl.ds(r0, k), :] = fe
        gate_ref[pl.ds(r0, k), :] = gate

    def heads_body(hp, carry):
        for j in range(PEER_TOPK_HEADS_PER_ITER):
            one_head(hp * PEER_TOPK_HEADS_PER_ITER + j)
        return carry

    lax.fori_loop(0, PEER_HEADS // PEER_TOPK_HEADS_PER_ITER, heads_body, 0)


def _peer_topk(q, sub_keys, *, tk):
    n, qd = q.shape
    assert n % tk == 0
    rows = PEER_HEADS * PEER_TOPK
    return pl.pallas_call(
        _peer_topk_kernel,
        grid=(n // tk,),
        in_specs=[pl.BlockSpec((tk, qd), lambda i: (i, 0)),
                  pl.BlockSpec(sub_keys.shape, lambda i: (0, 0, 0, 0))],
        out_specs=[pl.BlockSpec((rows, tk), lambda i: (0, i)),
                   pl.BlockSpec((rows, tk), lambda i: (0, i))],
        out_shape=[jax.ShapeDtypeStruct((rows, n), jnp.int32),
                   jax.ShapeDtypeStruct((rows, n), F32)],
        compiler_params=_cparams("parallel"),
        name="peer_topk",
    )(q, sub_keys)


PEER_ROWS = PEER_HEADS * PEER_TOPK
PEER_SLOTS = 16
PEER_LOOK = PEER_SLOTS - 1
assert PEER_SLOTS % 2 == 0 and PEER_LOOK % 2 == 1
PEER_IDX_TAIL = 16
assert PEER_LOOK <= PEER_IDX_TAIL and (PEER_IDX_TAIL * PEER_ROWS) % 1024 == 0
PEER_ACCS = 4
PEER_DOT_COPIES = 5
PEER_MIX_COPIES = 3
assert (PEER_DOT_COPIES + PEER_MIX_COPIES) * (PEER_ROWS // SUBLANES) == PEER_ROWS


def _sublane_fold(parts):
    sub = lax.broadcasted_iota(jnp.int32, parts[0].shape, 0)
    for dist in (4, 2, 1):
        low = (sub & dist) == 0
        nxt = []
        for a in range(len(parts) // 2):
            lo_src, hi_src = parts[a], parts[a + len(parts) // 2]
            kept = jnp.where(low, lo_src, hi_src)
            if 2 * dist == SUBLANES:
                moved = pltpu.roll(jnp.where(low, hi_src, lo_src), dist, axis=0)
            else:
                moved = jnp.where(low, pltpu.roll(lo_src, SUBLANES - dist, axis=0), pltpu.roll(hi_src, dist, axis=0))
            nxt.append(kept + moved)
        parts = nxt
    return parts[0]


def _peer_gather_kernel(idx_ref, x_ref, gate_ref, h_ref, tab_ref, out_ref, buf_even, buf_odd, sem_ref, coef_ref, *, tb):
    rows = PEER_ROWS
    nc = x_ref.shape[1] // LANES
    half = nc // 2
    groups = rows // SUBLANES
    step = pl.program_id(0)
    hi_mask = jnp.uint32(0xFFFF0000)
    bufs = (buf_even, buf_odd)

    def slot_of(tok):
        return (tok // 2) % (PEER_SLOTS // 2)

    def start_rows(arr, tok, r0, count):
        for j in range(count):
            r = r0 + j
            pltpu.make_async_copy(tab_ref.at[idx_ref[tok * rows + r]], arr.at[slot_of(tok), r],
                                  sem_ref.at[tok % PEER_SLOTS]).start(priority=j % 2)

    def wait_token(arr, tok):
        pltpu.make_async_copy(tab_ref.at[pl.ds(0, rows)], arr.at[slot_of(tok)], sem_ref.at[tok % PEER_SLOTS]).wait()

    def load_x(t):
        xrow = x_ref[pl.ds(t, 1), :]
        x_lo = jnp.concatenate([xrow[:, c * LANES:(c + 1) * LANES] for c in range(half)], axis=0)
        x_hi = jnp.concatenate([xrow[:, c * LANES:(c + 1) * LANES] for c in range(half, nc)], axis=0)
        return x_lo, x_hi

    def dot_group(buf, x, g):
        parts = []
        for s in range(SUBLANES):
            r = g * SUBLANES + s
            u_lo = lax.bitcast_convert_type(buf[r, 0:half, :] << 16, F32)
            u_hi = lax.bitcast_convert_type(buf[r, half:nc, :] << 16, F32)
            parts.append(u_lo * x[0] + u_hi * x[1])
        return _sublane_fold(parts)

    def finish_dot(t, folded):
        act = jnp.sum(jnp.concatenate(folded, axis=0), axis=-1, keepdims=True)
        lane_tok = lax.broadcasted_iota(jnp.int32, (rows, tb), 1)
        gcol = jnp.sum(jnp.where(lane_tok == t, gate_ref[0], 0.0), axis=-1, keepdims=True)
        coef_ref[...] = jnp.broadcast_to(gcol * _gelu(act), (rows, LANES))

    def mix_group(buf, g, acc_lo, acc_hi):
        for s in range(SUBLANES):
            r = g * SUBLANES + s
            c = jnp.broadcast_to(coef_ref[r:r + 1, :], (half, LANES))
            a = r % PEER_ACCS
            acc_lo[a] = acc_lo[a] + lax.bitcast_convert_type(buf[r, 0:half, :] & hi_mask, F32) * c
            acc_hi[a] = acc_hi[a] + lax.bitcast_convert_type(buf[r, half:nc, :] & hi_mask, F32) * c

    def emit(t, acc_lo, acc_hi):
        mix_lo = (acc_lo[0] + acc_lo[1]) + (acc_lo[2] + acc_lo[3])
        mix_hi = (acc_hi[0] + acc_hi[1]) + (acc_hi[2] + acc_hi[3])
        mix_row = jnp.concatenate([mix_lo[c:c + 1, :] for c in range(half)]
                                  + [mix_hi[c:c + 1, :] for c in range(half)], axis=1)
        out_ref[pl.ds(t, 1), :] = h_ref[pl.ds(t, 1), :] + mix_row

    def token(t, parity):
        arr, arr_next = bufs[parity], bufs[(parity + PEER_LOOK) % 2]
        tok_next = t + PEER_LOOK
        wait_token(arr, t)
        buf = arr.at[slot_of(t)]
        x = load_x(t)
        folded = []
        for g in range(groups):
            start_rows(arr_next, tok_next, g * PEER_DOT_COPIES, PEER_DOT_COPIES)
            folded.append(dot_group(buf, x, g))
        finish_dot(t, folded)
        acc_lo = [jnp.zeros((half, LANES), F32) for _ in range(PEER_ACCS)]
        acc_hi = [jnp.zeros((half, LANES), F32) for _ in range(PEER_ACCS)]
        for g in range(groups):
            start_rows(arr_next, tok_next, groups * PEER_DOT_COPIES + g * PEER_MIX_COPIES, PEER_MIX_COPIES)
            mix_group(buf, g, acc_lo, acc_hi)
        emit(t, acc_lo, acc_hi)

    @pl.when(step == 0)
    def _():
        for s in range(PEER_LOOK):
            def prime(i, carry, s=s):
                start_rows(bufs[s % 2], s, i * SUBLANES, SUBLANES)
                return carry
            lax.fori_loop(0, groups, prime, 0)

    def pair(p, carry):
        token(2 * p, 0)
        token(2 * p + 1, 1)
        return carry

    lax.fori_loop(0, tb // 2, pair, 0)

    @pl.when(step == pl.num_programs(0) - 1)
    def _():
        for s in range(PEER_LOOK):
            wait_token(bufs[(tb + s) % 2], tb + s)


def _peer_gather(eidx, xn, gate_blocks, h, table, *, tb):
    n, d = h.shape
    rows = PEER_ROWS
    assert n % tb == 0 and tb % PEER_SLOTS == 0 and PEER_ACCS == 4
    nb = n // tb
    idx2 = eidx.reshape(nb, tb * rows)
    idx_ext = jnp.concatenate([idx2, jnp.roll(idx2[:, :PEER_IDX_TAIL * rows], -1, axis=0)], axis=1).reshape(-1)
    return pl.pallas_call(
        functools.partial(_peer_gather_kernel, tb=tb),
        grid=(nb,),
        in_specs=[pl.BlockSpec(((tb + PEER_IDX_TAIL) * rows,), lambda i: (i,), memory_space=pltpu.SMEM),
                  pl.BlockSpec((tb, d), lambda i: (i, 0)),
                  pl.BlockSpec((1, rows, tb), lambda i: (i, 0, 0)),
                  pl.BlockSpec((tb, d), lambda i: (i, 0)),
                  pl.BlockSpec(memory_space=pl.ANY)],
        out_specs=pl.BlockSpec((tb, d), lambda i: (i, 0)),
        out_shape=jax.ShapeDtypeStruct((n, d), F32),
        scratch_shapes=[pltpu.VMEM((PEER_SLOTS // 2, rows, d // LANES, LANES), jnp.uint32),
                        pltpu.VMEM((PEER_SLOTS // 2, rows, d // LANES, LANES), jnp.uint32),
                        pltpu.SemaphoreType.DMA((PEER_SLOTS,)),
                        pltpu.VMEM((rows, LANES), F32)],
        compiler_params=_cparams("arbitrary"),
        name="peer_gather",
    )(idx_ext, xn, gate_blocks, h, table)


def _ple_kernel(h_ref, p_ref, n3_ref, fn_ref, wg_ref, wp_ref, y_ref):
    h = h_ref[...]
    gate = _sigmoid(_dot1(_rms_rows(h, n3_ref[...]), wg_ref[...]))
    h = h + gate * _dot1(p_ref[...], wp_ref[...])
    y_ref[...] = _rms_rows(h, fn_ref[...])


def _ple_final(h, p, norm3, final_norm, w_gate, w_proj, *, tm):
    n, d = h.shape
    pd = p.shape[1]
    assert n % tm == 0
    return pl.pallas_call(
        _ple_kernel,
        grid=(n // tm,),
        in_specs=[pl.BlockSpec((tm, d), lambda i: (i, 0)),
                  pl.BlockSpec((tm, pd), lambda i: (i, 0)),
                  pl.BlockSpec((1, d), lambda i: (0, 0)),
                  pl.BlockSpec((1, d), lambda i: (0, 0)),
                  pl.BlockSpec((d, d), lambda i: (0, 0)),
                  pl.BlockSpec((pd, d), lambda i: (0, 0))],
        out_specs=pl.BlockSpec((tm, d), lambda i: (i, 0)),
        out_shape=jax.ShapeDtypeStruct((n, d), F32),
        compiler_params=_cparams("parallel"),
        name="ple_final",
    )(h, p, norm3.reshape(1, d), final_norm.reshape(1, d), w_gate.astype(BF16), w_proj.astype(BF16))


QK_DIM = GDN_HEADS * GDN_DK
V_DIM = GDN_HEADS * GDN_DV
CONV_DIM = 2 * QK_DIM + V_DIM
MLP_DIM = MLP_GROUPS * MLP_GROUP_DIM
Z_COL = CONV_DIM
UV_COL = Z_COL + V_DIM
AB_COL = UV_COL + 2 * MLP_DIM
PROJ_COLS = AB_COL + LANES
PROJ_TN = 896


def _relayout_w_in(w_in):
    c0 = CONV_DIM
    c2 = c0 + 2 * GDN_HEADS
    c3 = c2 + V_DIM
    ab = jnp.pad(w_in[:, c0:c2], ((0, 0), (0, LANES - 2 * GDN_HEADS)))
    return jnp.concatenate([w_in[:, :c0], w_in[:, c2:c3], w_in[:, c3:], ab], axis=1)


def _pack_tables(peer_u, peer_v):
    ub = lax.bitcast_convert_type(peer_u.astype(BF16), jnp.uint16).astype(jnp.uint32)
    vb = lax.bitcast_convert_type(peer_v.astype(BF16), jnp.uint16).astype(jnp.uint32)
    e, d = peer_u.shape
    return (ub | (vb << 16)).reshape(e, d // LANES, LANES)


def _row_tile(n, want):
    t = min(n, want)
    assert n % t == 0
    return t


def _layer(x, p, s0, buf, w, final_norm):
    b, t, d = x.shape
    n = b * t
    x2 = x.reshape(n, d)
    tm = _row_tile(n, 1024)

    proj = _norm_matmul(x2, w["norm1"], w["w_in"], tm=tm, tn=PROJ_TN, name="in_proj")
    proj3 = proj.reshape(b, t, PROJ_COLS)
    new_buf = jnp.concatenate([buf, proj3[:, :, :CONV_DIM]], axis=1)[:, t:]

    buf8 = jnp.pad(buf, ((0, 0), (SUBLANES - (CONV_W - 1), 0), (0, 0)))
    q, k, v = _conv_qkv(proj3, buf8, w["conv_w"], tt=_row_tile(t, 256))

    tg = -(-t // GDN_CHUNK) * GDN_CHUNK
    ab3 = proj3[:, :, AB_COL:]
    if tg != t:
        padt = ((0, 0), (0, tg - t), (0, 0))
        q, k, v, ab3 = (jnp.pad(a, padt) for a in (q, k, v, ab3))
    abt = jnp.transpose(ab3[:, :, :2 * GDN_HEADS].reshape(b, tg // GDN_CHUNK, GDN_CHUNK, 2 * GDN_HEADS),
                        (0, 1, 3, 2))
    o, s_new = _gdn(q, k, v, ab3, abt, w["a_log"], w["dt_bias"], s0, tt=_row_tile(tg, 512), t_valid=t)
    o = o[:, :t]

    lc = min(t, MLP_CHUNK)
    y_b, vn = _mlp_branch(proj3, UV_COL // (2 * MLP_DIM), w["ln_g"], w["ln_b"], w["w_s"], w["b_s"],
                          rows=_row_tile(t, 512), lc=lc)

    h1 = _outproj(o.reshape(n, V_DIM), proj, Z_COL // V_DIM, y_b.reshape(n, MLP_DIM), x2,
                  w["gdn_norm"], w["w_out"], tm=_row_tile(n, 256))

    qp, xn2 = _norm_matmul(h1, w["norm2"], w["w_q"], tm=tm, tn=512, passes=1, emit_xn=True, name="peer_query")
    tb = 128
    eidx_t, gate_t = _peer_topk(qp, w["sub_keys"], tk=_row_tile(n, 512))
    rows = PEER_HEADS * PEER_TOPK
    gate_blocks = jnp.transpose(gate_t.reshape(rows, n // tb, tb), (1, 0, 2))
    h2 = _peer_gather(jnp.transpose(eidx_t), xn2, gate_blocks, h1, w["table"], tb=tb)

    y = _ple_final(h2, p.reshape(n, -1), w["norm3"], final_norm, w["w_ple_gate"], w["w_ple_proj"],
                   tm=_row_tile(n, 256))
    return y.reshape(b, t, d), s_new, new_buf, vn


def kernel(x_prompt, x_sample, state_gdn, cache_conv, p_prompt, p_sample, norm1, w_in, conv_w, a_log, dt_bias,
           gdn_norm, ln_g, ln_b, w_s, b_s, w_out, norm2, w_q, sub_keys, peer_u, peer_v, norm3, w_ple_gate,
           w_ple_proj, final_norm):
    assert norm1.shape[0] == 1, "single layer"
    w = dict(norm1=norm1[0], w_in=_relayout_w_in(w_in[0]), conv_w=conv_w[0], a_log=a_log[0], dt_bias=dt_bias[0],
             gdn_norm=gdn_norm[0], ln_g=ln_g[0], ln_b=ln_b[0], w_s=w_s[0], b_s=b_s[0], w_out=w_out[0],
             norm2=norm2[0], w_q=w_q[0], sub_keys=sub_keys[0], table=_pack_tables(peer_u[0], peer_v[0]),
             norm3=norm3[0], w_ple_gate=w_ple_gate[0], w_ple_proj=w_ple_proj[0])
    bp = x_prompt.shape[0]
    s0p = jnp.zeros((bp, GDN_HEADS, GDN_DK, GDN_DV), F32)
    b0p = jnp.zeros((bp, CONV_W - 1, CONV_DIM), F32)
    y_s, s_s, b_s_new, v_s = _layer(x_sample, p_sample[0], state_gdn[0], cache_conv[0], w, final_norm)
    y_p, s_p, b_p, _ = _layer(x_prompt, p_prompt[0], s0p, b0p, w, final_norm)
    return (y_p, y_s, s_p[None], b_p[None], s_s[None], b_s_new[None], v_s[None])
```

```python
import functools
import math

import jax
import jax.numpy as jnp
from jax import lax
from jax.experimental import pallas as pl
from jax.experimental.pallas import tpu as pltpu

F32 = jnp.float32
BF16 = jnp.bfloat16

EPS = 1e-6
GDN_CHUNK = 64
GDN_HEADS = 8
GDN_CHUNKS_PER_ITER = 4
GDN_DK = 128
GDN_DV = 128
CONV_W = 4
MLP_CHUNK = 128
MLP_GROUPS = 8
MLP_GROUP_DIM = 128
PEER_HEADS = 8
PEER_QHALF = 128
N_KEYS = 128
PEER_TOPK = 16
PEER_TOPK_HEADS_PER_ITER = 8

LANES = 128
SUBLANES = 8
VMEM_LIMIT_BYTES = 56 * 1024 * 1024


def _cparams(*sem):
    return pltpu.CompilerParams(dimension_semantics=sem, vmem_limit_bytes=VMEM_LIMIT_BYTES)


def _split3(x):
    hi = x.astype(BF16)
    r1 = x - hi.astype(F32)
    mid = r1.astype(BF16)
    lo = (r1 - mid.astype(F32)).astype(BF16)
    return hi, mid, lo


def _dg(a, b, dims):
    return lax.dot_general(a, b, (dims, ((), ())), preferred_element_type=F32)


_NN = ((1,), (0,))
_NT = ((1,), (1,))
_TN = ((0,), (0,))


def _dot1(a, b, dims=_NN):
    return _dg(a.astype(BF16), b.astype(BF16), dims)


def _dot3(a, b, dims=_NN):
    ah = a.astype(BF16)
    al = (a - ah.astype(F32)).astype(BF16)
    bh = b.astype(BF16)
    bl = (b - bh.astype(F32)).astype(BF16)
    return _dg(ah, bh, dims) + (_dg(ah, bl, dims) + _dg(al, bh, dims))


def _dot_exact_lhs(a_exact_bf16, b, dims=_NN):
    b0, b1, b2 = _split3(b)
    return _dg(a_exact_bf16, b0, dims) + (_dg(a_exact_bf16, b1, dims) + _dg(a_exact_bf16, b2, dims))


def _dot_exact_rhs(a, b_exact_bf16, dims=_NN):
    a0, a1, a2 = _split3(a)
    return _dg(a0, b_exact_bf16, dims) + (_dg(a1, b_exact_bf16, dims) + _dg(a2, b_exact_bf16, dims))


def _rms_rows(x, gain):
    ms = jnp.mean(x * x, axis=-1, keepdims=True)
    return x * lax.rsqrt(ms + EPS) * gain


def _sigmoid(x):
    return 1.0 / (1.0 + jnp.exp(-x))


def _silu(x):
    return x * _sigmoid(x)


def _gelu(x):
    return 0.5 * x * (1.0 + lax.erf(x * (1.0 / math.sqrt(2.0))))


def _softplus(x):
    return jnp.maximum(x, 0.0) + jnp.log1p(jnp.exp(-jnp.abs(x)))


def _norm_matmul_kernel(x_ref, g_ref, *refs, passes, emit_xn):
    if passes == 1:
        w_refs, rest = refs[:1], refs[1:]
    else:
        w_refs, rest = refs[:2], refs[2:]
    if emit_xn:
        o_ref, xn_out_ref = rest[0], rest[1]
        scr = rest[2:]
    else:
        o_ref = rest[0]
        scr = rest[1:]
    j = pl.program_id(1)

    @pl.when(j == 0)
    def _():
        xn = _rms_rows(x_ref[...], g_ref[...])
        if emit_xn:
            xn_out_ref[...] = xn
        hi = xn.astype(BF16)
        scr[0][...] = hi
        if passes == 3:
            scr[1][...] = (xn - hi.astype(F32)).astype(BF16)

    if passes == 1:
        o_ref[...] = _dg(scr[0][...], w_refs[0][...], _NN)
    else:
        xh = scr[0][...]
        wh = w_refs[0][...]
        o_ref[...] = _dg(xh, wh, _NN) + (_dg(xh, w_refs[1][...], _NN) + _dg(scr[1][...], wh, _NN))


def _norm_matmul(x, gain, w, *, tm, tn, passes=1, emit_xn=False, name):
    n, d = x.shape
    m = w.shape[1]
    assert n % tm == 0 and m % tn == 0
    wh = w.astype(BF16)
    ws = [wh] if passes == 1 else [wh, (w - wh.astype(F32)).astype(BF16)]
    w_specs = [pl.BlockSpec((d, tn), lambda i, j: (0, j)) for _ in ws]
    out_shape = [jax.ShapeDtypeStruct((n, m), F32)]
    out_specs = [pl.BlockSpec((tm, tn), lambda i, j: (i, j))]
    if emit_xn:
        out_shape.append(jax.ShapeDtypeStruct((n, d), F32))
        out_specs.append(pl.BlockSpec((tm, d), lambda i, j: (i, 0)))
    scratch = [pltpu.VMEM((tm, d), BF16) for _ in range(1 if passes == 1 else 2)]
    res = pl.pallas_call(
        functools.partial(_norm_matmul_kernel, passes=passes, emit_xn=emit_xn),
        grid=(n // tm, m // tn),
        in_specs=[pl.BlockSpec((tm, d), lambda i, j: (i, 0)),
                  pl.BlockSpec((1, d), lambda i, j: (0, 0))] + w_specs,
        out_specs=out_specs,
        out_shape=out_shape,
        scratch_shapes=scratch,
        compiler_params=_cparams("parallel", "arbitrary"),
        name=name,
    )(x, gain.reshape(1, d), *ws)
    return res if emit_xn else res[0]


def _conv_kernel(x_ref, buf_ref, w_ref, q_ref, k_ref, v_ref, xp_ref, *, tt):
    t = pl.program_id(1)
    halo = SUBLANES

    @pl.when(t == 0)
    def _():
        xp_ref[0:halo, :] = buf_ref[0]

    @pl.when(t > 0)
    def _():
        xp_ref[0:halo, :] = xp_ref[tt:tt + halo, :]

    xp_ref[halo:halo + tt, :] = x_ref[0]
    base = halo - (CONV_W - 1)
    y = xp_ref[base:base + tt, :] * w_ref[0:1, :]
    for j in range(1, CONV_W):
        y = y + xp_ref[base + j:base + j + tt, :] * w_ref[j:j + 1, :]
    y = _silu(y)
    qk_dim = GDN_HEADS * GDN_DK
    for h in range(GDN_HEADS):
        qh = y[:, h * GDN_DK:(h + 1) * GDN_DK]
        qn = qh * lax.rsqrt(jnp.sum(qh * qh, axis=-1, keepdims=True) + EPS)
        q_ref[0, :, h * GDN_DK:(h + 1) * GDN_DK] = qn * (GDN_DK ** -0.5)
        kh = y[:, qk_dim + h * GDN_DK:qk_dim + (h + 1) * GDN_DK]
        k_ref[0, :, h * GDN_DK:(h + 1) * GDN_DK] = kh * lax.rsqrt(jnp.sum(kh * kh, axis=-1, keepdims=True) + EPS)
    v_ref[0] = y[:, 2 * qk_dim:]


def _conv_qkv(proj3, buf8, conv_w, *, tt):
    b, t, _ = proj3.shape
    cdim = conv_w.shape[1]
    hd = GDN_HEADS * GDN_DK
    assert t % tt == 0
    shp = jax.ShapeDtypeStruct((b, t, hd), F32)
    ospec = pl.BlockSpec((1, tt, hd), lambda i, j: (i, j, 0))
    return pl.pallas_call(
        functools.partial(_conv_kernel, tt=tt),
        grid=(b, t // tt),
        in_specs=[pl.BlockSpec((1, tt, cdim), lambda i, j: (i, j, 0)),
                  pl.BlockSpec((1, SUBLANES, cdim), lambda i, j: (i, 0, 0)),
                  pl.BlockSpec((CONV_W, cdim), lambda i, j: (0, 0))],
        out_specs=[ospec, ospec, ospec],
        out_shape=[shp, shp, shp],
        scratch_shapes=[pltpu.VMEM((tt + 2 * SUBLANES, cdim), F32)],
        compiler_params=_cparams("parallel", "arbitrary"),
        name="conv_qkv",
    )(proj3, buf8, conv_w)


def _gdn_kernel(q_ref, k_ref, v_ref, ab_ref, abt_ref, prm_row_ref, prm_col_ref, s0_ref,
                o_ref, sout_ref, state_ref, *, tt, t_valid):
    tstep = pl.program_id(1)
    c = GDN_CHUNK

    @pl.when(tstep == 0)
    def _():
        state_ref[...] = s0_ref[0]

    ii = lax.broadcasted_iota(jnp.int32, (c, c), 0)
    jj = lax.broadcasted_iota(jnp.int32, (c, c), 1)
    incl = ii >= jj
    strict = ii > jj
    tri = incl.astype(BF16)
    tri_t = (jj >= ii).astype(BF16)
    eye = (ii == jj).astype(F32)

    alog_row = prm_row_ref[0:1, :]
    dtb_row = prm_row_ref[1:2, :]
    alog_col = prm_col_ref[:, 0:1]
    dtb_col = prm_col_ref[:, 1:2]

    def chunk_units(ci):
        r0 = pl.multiple_of(ci * c, c)
        tpos = tstep * tt + r0
        a_blk = ab_ref[0, pl.ds(r0, c), :]
        valid_col = (tpos + lax.broadcasted_iota(jnp.int32, (c, LANES), 0)) < t_valid
        g_col = jnp.where(valid_col, -jnp.exp(alog_row) * _softplus(a_blk + dtb_row), 0.0)
        beta_col = jnp.where(valid_col, _sigmoid(a_blk), 0.0)
        gc_col = _dot_exact_lhs(tri, g_col)
        at_blk = abt_ref[0, ci]
        valid_row = (tpos + lax.broadcasted_iota(jnp.int32, (2 * GDN_HEADS, c), 1)) < t_valid
        g_row = jnp.where(valid_row, -jnp.exp(alog_col) * _softplus(at_blk + dtb_col), 0.0)
        gc_row = _dot_exact_rhs(g_row, tri_t)
        units = []
        for h in range(GDN_HEADS):
            sl = slice(h * GDN_DK, (h + 1) * GDN_DK)
            qh = q_ref[0, pl.ds(r0, c), sl]
            kh = k_ref[0, pl.ds(r0, c), sl]
            vh = v_ref[0, pl.ds(r0, c), sl]
            gc = gc_col[:, h:h + 1]
            beta = beta_col[:, GDN_HEADS + h:GDN_HEADS + h + 1]
            gcr = gc_row[h:h + 1, :]
            gc_last = gc_col[c - 1:c, h:h + 1]
            decay = jnp.where(incl, jnp.exp(jnp.where(incl, gc - gcr, 0.0)), 0.0)
            egc = jnp.exp(gc)
            kb = kh * beta
            lmat = jnp.where(strict, _dot1(kb, kh, _NT) * decay, 0.0)
            units.append(dict(
                h=h, r0=r0, sl=sl, tinv=eye - lmat, pw=lmat,
                rhs=jnp.concatenate([vh * beta, kb * egc], axis=-1),
                attn=_dot1(qh, kh, _NT) * decay, q_dec=qh * egc,
                k_dec=kh * jnp.exp(gc_last - gc), bd=jnp.exp(gc_last)))
        return units

    def chunks_body(cp, carry):
        per_chunk = [chunk_units(cp * cpi + j) for j in range(cpi)]
        units = [u for us in per_chunk for u in us]
        for _ in range(5):
            for u in units:
                u["pw"] = _dot1(u["pw"], u["pw"])
            for u in units:
                u["tinv"] = u["tinv"] + _dot1(u["tinv"], u["pw"])
        for u in units:
            u["sol"] = _dot3(u["tinv"], u["rhs"])
        for us in per_chunk:
            for u in us:
                s = state_ref[u["h"]]
                u["s"] = s
                u["v_new"] = u["sol"][:, :GDN_DV] - _dot1(u["sol"][:, GDN_DV:], s)
            for u in us:
                o_ref[0, pl.ds(u["r0"], c), u["sl"]] = _dot1(u["q_dec"], u["s"]) + _dot1(u["attn"], u["v_new"])
                state_ref[u["h"]] = u["s"] * u["bd"] + _dot1(u["k_dec"], u["v_new"], _TN)
        return carry

    n_chunks = tt // c
    cpi = GDN_CHUNKS_PER_ITER if n_chunks % GDN_CHUNKS_PER_ITER == 0 else 1
    lax.fori_loop(0, n_chunks // cpi, chunks_body, 0)

    @pl.when(tstep == pl.num_programs(1) - 1)
    def _():
        sout_ref[0] = state_ref[...]


def _gdn(q, k, v, ab, abt, a_log, dt_bias, s0, *, tt, t_valid):
    b, t, hd = q.shape
    assert t % tt == 0 and tt % GDN_CHUNK == 0
    prm_row = jnp.zeros((SUBLANES, LANES), F32)
    prm_row = prm_row.at[0, :GDN_HEADS].set(a_log).at[1, :GDN_HEADS].set(dt_bias)
    prm_col = jnp.zeros((2 * GDN_HEADS, LANES), F32)
    prm_col = prm_col.at[:GDN_HEADS, 0].set(a_log).at[:GDN_HEADS, 1].set(dt_bias)
    seq = pl.BlockSpec((1, tt, hd), lambda i, j: (i, j, 0))
    st = pl.BlockSpec((1, GDN_HEADS, GDN_DK, GDN_DV), lambda i, j: (i, 0, 0, 0))
    return pl.pallas_call(
        functools.partial(_gdn_kernel, tt=tt, t_valid=t_valid),
        grid=(b, t // tt),
        in_specs=[seq, seq, seq,
                  pl.BlockSpec((1, tt, LANES), lambda i, j: (i, j, 0)),
                  pl.BlockSpec((1, tt // GDN_CHUNK, 2 * GDN_HEADS, GDN_CHUNK), lambda i, j: (i, j, 0, 0)),
                  pl.BlockSpec((SUBLANES, LANES), lambda i, j: (0, 0)),
                  pl.BlockSpec((2 * GDN_HEADS, LANES), lambda i, j: (0, 0)),
                  st],
        out_specs=[seq, st],
        out_shape=[jax.ShapeDtypeStruct((b, t, hd), F32),
                   jax.ShapeDtypeStruct((b, GDN_HEADS, GDN_DK, GDN_DV), F32)],
        scratch_shapes=[pltpu.VMEM((GDN_HEADS, GDN_DK, GDN_DV), F32)],
        compiler_params=_cparams("parallel", "arbitrary"),
        name="gdn_core",
    )(q, k, v, ab, abt, prm_row, prm_col, s0)


def _mlp_kernel(uv_ref, lng_ref, lnb_ref, ws_ref, bst_ref, y_ref, vn_ref, *, rows, lc):
    mdim = MLP_GROUPS * MLP_GROUP_DIM
    ii = lax.broadcasted_iota(jnp.int32, (lc, lc), 0)
    jj = lax.broadcasted_iota(jnp.int32, (lc, lc), 1)
    keep = ii >= jj
    for r in range(rows // lc):
        rs = slice(r * lc, (r + 1) * lc)
        g = _gelu(uv_ref[0, rs, :])
        u = g[:, :mdim]
        vv = g[:, mdim:]
        mu = jnp.mean(vv, axis=-1, keepdims=True)
        xc = vv - mu
        vn = xc * lax.rsqrt(jnp.mean(xc * xc, axis=-1, keepdims=True) + EPS) * lng_ref[...] + lnb_ref[...]
        vn_ref[0, rs, :] = vn
        for gi in range(MLP_GROUPS):
            gs = slice(gi * MLP_GROUP_DIM, (gi + 1) * MLP_GROUP_DIM)
            wm = jnp.where(keep, ws_ref[gi], 0.0)
            s = _dot1(wm, vn[:, gs]) + bst_ref[:, gi:gi + 1]
            y_ref[0, rs, gs] = u[:, gs] * s


def _mlp_branch(proj3, col_block, ln_g, ln_b, w_s, b_s, *, rows, lc):
    b, t, _ = proj3.shape
    mdim = MLP_GROUPS * MLP_GROUP_DIM
    assert t % rows == 0 and rows % lc == 0
    ws = w_s[:, :lc, :lc]
    bst = jnp.transpose(b_s[:, :lc])
    shp = jax.ShapeDtypeStruct((b, t, mdim), F32)
    ospec = pl.BlockSpec((1, rows, mdim), lambda i, j: (i, j, 0))
    return pl.pallas_call(
        functools.partial(_mlp_kernel, rows=rows, lc=lc),
        grid=(b, t // rows),
        in_specs=[pl.BlockSpec((1, rows, 2 * mdim), lambda i, j: (i, j, col_block)),
                  pl.BlockSpec((1, mdim), lambda i, j: (0, 0)),
                  pl.BlockSpec((1, mdim), lambda i, j: (0, 0)),
                  pl.BlockSpec((MLP_GROUPS, lc, lc), lambda i, j: (0, 0, 0)),
                  pl.BlockSpec((lc, MLP_GROUPS), lambda i, j: (0, 0))],
        out_specs=[ospec, ospec],
        out_shape=[shp, shp],
        compiler_params=_cparams("parallel", "parallel"),
        name="mlp_branch",
    )(proj3, ln_g.reshape(1, mdim), ln_b.reshape(1, mdim), ws, bst)


def _outproj_kernel(o_ref, z_ref, y_ref, h_ref, gn_ref, w_ref, out_ref):
    parts = []
    for hh in range(GDN_HEADS):
        sl = slice(hh * GDN_DV, (hh + 1) * GDN_DV)
        parts.append((_rms_rows(o_ref[:, sl], gn_ref[...]) * _silu(z_ref[:, sl])).astype(BF16))
    parts.append(y_ref[...].astype(BF16))
    cat = jnp.concatenate(parts, axis=-1)
    out_ref[...] = h_ref[...] + _dg(cat, w_ref[...], _NN)


def _outproj(o2, proj2, z_block, y2, h2, gdn_norm, w_out, *, tm):
    n, d = h2.shape
    vd = GDN_HEADS * GDN_DV
    assert n % tm == 0
    return pl.pallas_call(
        _outproj_kernel,
        grid=(n // tm,),
        in_specs=[pl.BlockSpec((tm, vd), lambda i: (i, 0)),
                  pl.BlockSpec((tm, vd), lambda i: (i, z_block)),
                  pl.BlockSpec((tm, vd), lambda i: (i, 0)),
                  pl.BlockSpec((tm, d), lambda i: (i, 0)),
                  pl.BlockSpec((1, GDN_DV), lambda i: (0, 0)),
                  pl.BlockSpec(w_out.shape, lambda i: (0, 0))],
        out_specs=pl.BlockSpec((tm, d), lambda i: (i, 0)),
        out_shape=jax.ShapeDtypeStruct((n, d), F32),
        compiler_params=_cparams("parallel"),
        name="out_proj",
    )(o2, proj2, y2, h2, gdn_norm.reshape(1, GDN_DV), w_out.astype(BF16))


def _topk_axis0(x, k, payloads=()):
    r = x.shape[0]
    iota = lax.broadcasted_iota(jnp.int32, x.shape, 0)
    vals, idxs = [], []
    outs = [[] for _ in payloads]
    for _ in range(k):
        m = jnp.max(x, axis=0, keepdims=True)
        i = jnp.min(jnp.where(x == m, iota, r), axis=0, keepdims=True)
        hit = iota == i
        vals.append(m)
        idxs.append(i)
        for p, acc in zip(payloads, outs):
            acc.append(jnp.sum(jnp.where(hit, p, 0), axis=0, keepdims=True))
        x = jnp.where(hit, -jnp.inf, x)
    cat = lambda parts: jnp.concatenate(parts, axis=0)
    return cat(vals), cat(idxs), [cat(acc) for acc in outs]


def _peer_topk_kernel(q_ref, keys_ref, eidx_ref, gate_ref):
    k = PEER_TOPK

    def one_head(h):
        sv, si = [], []
        for s in range(2):
            c0 = pl.multiple_of(h * (2 * PEER_QHALF) + s * PEER_QHALF, PEER_QHALF)
            qhs = q_ref[:, pl.ds(c0, PEER_QHALF)]
            sc_t = _dot3(keys_ref[s, h], qhs, _NT)
            v, i, _ = _topk_axis0(sc_t, k)
            sv.append(v)
            si.append(i)
        half = k // 2
        sub = lax.broadcasted_iota(jnp.int32, (half, sv[0].shape[1]), 0)
        cand_parts = [sv[0][0:1, :] + sv[1]]
        cidx_parts = [si[0][0:1, :] * N_KEYS + si[1]]
        for a in range(1, half):
            keep = sub < (k // (a + 1))
            cand_parts.append(jnp.where(keep, sv[0][a:a + 1, :] + sv[1][0:half, :], -jnp.inf))
            cidx_parts.append(si[0][a:a + 1, :] * N_KEYS + si[1][0:half, :])
        cand_parts.append(sv[0][half:, :] + sv[1][0:1, :])
        cidx_parts.append(si[0][half:, :] * N_KEYS + si[1][0:1, :])
        cand = jnp.concatenate(cand_parts, axis=0)
        cidx = jnp.concatenate(cidx_parts, axis=0)
        fv, _, (fe,) = _topk_axis0(cand, k, (cidx,))
        e = jnp.exp(fv - fv[0:1, :])
        gate = e / jnp.sum(e, axis=0, keepdims=True)
        r0 = pl.multiple_of(h * k, k)
        eidx_ref[pl.ds(r0, k), :] = fe
        gate_ref[pl.ds(r0, k), :] = gate

    def heads_body(hp, carry):
        for j in range(PEER_TOPK_HEADS_PER_ITER):
            one_head(hp * PEER_TOPK_HEADS_PER_ITER + j)
        return carry

    lax.fori_loop(0, PEER_HEADS // PEER_TOPK_HEADS_PER_ITER, heads_body, 0)


def _peer_topk(q, sub_keys, *, tk):
    n, qd = q.shape
    assert n % tk == 0
    rows = PEER_HEADS * PEER_TOPK
    return pl.pallas_call(
        _peer_topk_kernel,
        grid=(n // tk,),
        in_specs=[pl.BlockSpec((tk, qd), lambda i: (i, 0)),
                  pl.BlockSpec(sub_keys.shape, lambda i: (0, 0, 0, 0))],
        out_specs=[pl.BlockSpec((rows, tk), lambda i: (0, i)),
                   pl.BlockSpec((rows, tk), lambda i: (0, i))],
        out_shape=[jax.ShapeDtypeStruct((rows, n), jnp.int32),
                   jax.ShapeDtypeStruct((rows, n), F32)],
        compiler_params=_cparams("parallel"),
        name="peer_topk",
    )(q, sub_keys)


PEER_ROWS = PEER_HEADS * PEER_TOPK
PEER_SLOTS = 16
PEER_LOOK = PEER_SLOTS - 1
assert PEER_SLOTS % 2 == 0 and PEER_LOOK % 2 == 1
PEER_IDX_TAIL = 16
assert PEER_LOOK <= PEER_IDX_TAIL and (PEER_IDX_TAIL * PEER_ROWS) % 1024 == 0
PEER_ACCS = 4
PEER_DOT_COPIES = 5
PEER_MIX_COPIES = 3
assert (PEER_DOT_COPIES + PEER_MIX_COPIES) * (PEER_ROWS // SUBLANES) == PEER_ROWS


def _sublane_fold(parts):
    sub = lax.broadcasted_iota(jnp.int32, parts[0].shape, 0)
    for dist in (4, 2, 1):
        low = (sub & dist) == 0
        nxt = []
        for a in range(len(parts) // 2):
            lo_src, hi_src = parts[a], parts[a + len(parts) // 2]
            kept = jnp.where(low, lo_src, hi_src)
            if 2 * dist == SUBLANES:
                moved = pltpu.roll(jnp.where(low, hi_src, lo_src), dist, axis=0)
            else:
                moved = jnp.where(low, pltpu.roll(lo_src, SUBLANES - dist, axis=0), pltpu.roll(hi_src, dist, axis=0))
            nxt.append(kept + moved)
        parts = nxt
    return parts[0]


def _peer_gather_kernel(idx_ref, x_ref, gate_ref, h_ref, tab_ref, out_ref, buf_even, buf_odd, sem_ref, coef_ref, *, tb):
    rows = PEER_ROWS
    nc = x_ref.shape[1] // LANES
    half = nc // 2
    groups = rows // SUBLANES
    step = pl.program_id(0)
    hi_mask = jnp.uint32(0xFFFF0000)
    bufs = (buf_even, buf_odd)

    def slot_of(tok):
        return (tok // 2) % (PEER_SLOTS // 2)

    def start_rows(arr, tok, r0, count):
        for j in range(count):
            r = r0 + j
            pltpu.make_async_copy(tab_ref.at[idx_ref[tok * rows + r]], arr.at[slot_of(tok), r],
                                  sem_ref.at[tok % PEER_SLOTS]).start(priority=j % 2)

    def wait_token(arr, tok):
        pltpu.make_async_copy(tab_ref.at[pl.ds(0, rows)], arr.at[slot_of(tok)], sem_ref.at[tok % PEER_SLOTS]).wait()

    def load_x(t):
        xrow = x_ref[pl.ds(t, 1), :]
        x_lo = jnp.concatenate([xrow[:, c * LANES:(c + 1) * LANES] for c in range(half)], axis=0)
        x_hi = jnp.concatenate([xrow[:, c * LANES:(c + 1) * LANES] for c in range(half, nc)], axis=0)
        return x_lo, x_hi

    def dot_group(buf, x, g):
        parts = []
        for s in range(SUBLANES):
            r = g * SUBLANES + s
            u_lo = lax.bitcast_convert_type(buf[r, 0:half, :] << 16, F32)
            u_hi = lax.bitcast_convert_type(buf[r, half:nc, :] << 16, F32)
            parts.append(u_lo * x[0] + u_hi * x[1])
        return _sublane_fold(parts)

    def finish_dot(t, folded):
        act = jnp.sum(jnp.concatenate(folded, axis=0), axis=-1, keepdims=True)
        lane_tok = lax.broadcasted_iota(jnp.int32, (rows, tb), 1)
        gcol = jnp.sum(jnp.where(lane_tok == t, gate_ref[0], 0.0), axis=-1, keepdims=True)
        coef_ref[...] = jnp.broadcast_to(gcol * _gelu(act), (rows, LANES))

    def mix_group(buf, g, acc_lo, acc_hi):
        for s in range(SUBLANES):
            r = g * SUBLANES + s
            c = jnp.broadcast_to(coef_ref[r:r + 1, :], (half, LANES))
            a = r % PEER_ACCS
            acc_lo[a] = acc_lo[a] + lax.bitcast_convert_type(buf[r, 0:half, :] & hi_mask, F32) * c
            acc_hi[a] = acc_hi[a] + lax.bitcast_convert_type(buf[r, half:nc, :] & hi_mask, F32) * c

    def emit(t, acc_lo, acc_hi):
        mix_lo = (acc_lo[0] + acc_lo[1]) + (acc_lo[2] + acc_lo[3])
        mix_hi = (acc_hi[0] + acc_hi[1]) + (acc_hi[2] + acc_hi[3])
        mix_row = jnp.concatenate([mix_lo[c:c + 1, :] for c in range(half)]
                                  + [mix_hi[c:c + 1, :] for c in range(half)], axis=1)
        out_ref[pl.ds(t, 1), :] = h_ref[pl.ds(t, 1), :] + mix_row

    def token(t, parity):
        arr, arr_next = bufs[parity], bufs[(parity + PEER_LOOK) % 2]
        tok_next = t + PEER_LOOK
        wait_token(arr, t)
        buf = arr.at[slot_of(t)]
        x = load_x(t)
        folded = []
        for g in range(groups):
            start_rows(arr_next, tok_next, g * PEER_DOT_COPIES, PEER_DOT_COPIES)
            folded.append(dot_group(buf, x, g))
        finish_dot(t, folded)
        acc_lo = [jnp.zeros((half, LANES), F32) for _ in range(PEER_ACCS)]
        acc_hi = [jnp.zeros((half, LANES), F32) for _ in range(PEER_ACCS)]
        for g in range(groups):
            start_rows(arr_next, tok_next, groups * PEER_DOT_COPIES + g * PEER_MIX_COPIES, PEER_MIX_COPIES)
            mix_group(buf, g, acc_lo, acc_hi)
        emit(t, acc_lo, acc_hi)

    @pl.when(step == 0)
    def _():
        for s in range(PEER_LOOK):
            def prime(i, carry, s=s):
                start_rows(bufs[s % 2], s, i * SUBLANES, SUBLANES)
                return carry
            lax.fori_loop(0, groups, prime, 0)

    def pair(p, carry):
        token(2 * p, 0)
        token(2 * p + 1, 1)
        return carry

    lax.fori_loop(0, tb // 2, pair, 0)

    @pl.when(step == pl.num_programs(0) - 1)
    def _():
        for s in range(PEER_LOOK):
            wait_token(bufs[(tb + s) % 2], tb + s)


def _peer_gather(eidx, xn, gate_blocks, h, table, *, tb):
    n, d = h.shape
    rows = PEER_ROWS
    assert n % tb == 0 and tb % PEER_SLOTS == 0 and PEER_ACCS == 4
    nb = n // tb
    idx2 = eidx.reshape(nb, tb * rows)
    idx_ext = jnp.concatenate([idx2, jnp.roll(idx2[:, :PEER_IDX_TAIL * rows], -1, axis=0)], axis=1).reshape(-1)
    return pl.pallas_call(
        functools.partial(_peer_gather_kernel, tb=tb),
        grid=(nb,),
        in_specs=[pl.BlockSpec(((tb + PEER_IDX_TAIL) * rows,), lambda i: (i,), memory_space=pltpu.SMEM),
                  pl.BlockSpec((tb, d), lambda i: (i, 0)),
                  pl.BlockSpec((1, rows, tb), lambda i: (i, 0, 0)),
                  pl.BlockSpec((tb, d), lambda i: (i, 0)),
                  pl.BlockSpec(memory_space=pl.ANY)],
        out_specs=pl.BlockSpec((tb, d), lambda i: (i, 0)),
        out_shape=jax.ShapeDtypeStruct((n, d), F32),
        scratch_shapes=[pltpu.VMEM((PEER_SLOTS // 2, rows, d // LANES, LANES), jnp.uint32),
                        pltpu.VMEM((PEER_SLOTS // 2, rows, d // LANES, LANES), jnp.uint32),
                        pltpu.SemaphoreType.DMA((PEER_SLOTS,)),
                        pltpu.VMEM((rows, LANES), F32)],
        compiler_params=_cparams("arbitrary"),
        name="peer_gather",
    )(idx_ext, xn, gate_blocks, h, table)


def _ple_kernel(h_ref, p_ref, n3_ref, fn_ref, wg_ref, wp_ref, y_ref):
    h = h_ref[...]
    gate = _sigmoid(_dot1(_rms_rows(h, n3_ref[...]), wg_ref[...]))
    h = h + gate * _dot1(p_ref[...], wp_ref[...])
    y_ref[...] = _rms_rows(h, fn_ref[...])


def _ple_final(h, p, norm3, final_norm, w_gate, w_proj, *, tm):
    n, d = h.shape
    pd = p.shape[1]
    assert n % tm == 0
    return pl.pallas_call(
        _ple_kernel,
        grid=(n // tm,),
        in_specs=[pl.BlockSpec((tm, d), lambda i: (i, 0)),
                  pl.BlockSpec((tm, pd), lambda i: (i, 0)),
                  pl.BlockSpec((1, d), lambda i: (0, 0)),
                  pl.BlockSpec((1, d), lambda i: (0, 0)),
                  pl.BlockSpec((d, d), lambda i: (0, 0)),
                  pl.BlockSpec((pd, d), lambda i: (0, 0))],
        out_specs=pl.BlockSpec((tm, d), lambda i: (i, 0)),
        out_shape=jax.ShapeDtypeStruct((n, d), F32),
        compiler_params=_cparams("parallel"),
        name="ple_final",
    )(h, p, norm3.reshape(1, d), final_norm.reshape(1, d), w_gate.astype(BF16), w_proj.astype(BF16))


QK_DIM = GDN_HEADS * GDN_DK
V_DIM = GDN_HEADS * GDN_DV
CONV_DIM = 2 * QK_DIM + V_DIM
MLP_DIM = MLP_GROUPS * MLP_GROUP_DIM
Z_COL = CONV_DIM
UV_COL = Z_COL + V_DIM
AB_COL = UV_COL + 2 * MLP_DIM
PROJ_COLS = AB_COL + LANES
PROJ_TN = 896


def _relayout_w_in(w_in):
    c0 = CONV_DIM
    c2 = c0 + 2 * GDN_HEADS
    c3 = c2 + V_DIM
    ab = jnp.pad(w_in[:, c0:c2], ((0, 0), (0, LANES - 2 * GDN_HEADS)))
    return jnp.concatenate([w_in[:, :c0], w_in[:, c2:c3], w_in[:, c3:], ab], axis=1)


def _pack_tables(peer_u, peer_v):
    ub = lax.bitcast_convert_type(peer_u.astype(BF16), jnp.uint16).astype(jnp.uint32)
    vb = lax.bitcast_convert_type(peer_v.astype(BF16), jnp.uint16).astype(jnp.uint32)
    e, d = peer_u.shape
    return (ub | (vb << 16)).reshape(e, d // LANES, LANES)


def _row_tile(n, want):
    t = min(n, want)
    assert n % t == 0
    return t


def _layer(x, p, s0, buf, w, final_norm):
    b, t, d = x.shape
    n = b * t
    x2 = x.reshape(n, d)
    tm = _row_tile(n, 1024)

    proj = _norm_matmul(x2, w["norm1"], w["w_in"], tm=tm, tn=PROJ_TN, name="in_proj")
    proj3 = proj.reshape(b, t, PROJ_COLS)
    new_buf = jnp.concatenate([buf, proj3[:, :, :CONV_DIM]], axis=1)[:, t:]

    buf8 = jnp.pad(buf, ((0, 0), (SUBLANES - (CONV_W - 1), 0), (0, 0)))
    q, k, v = _conv_qkv(proj3, buf8, w["conv_w"], tt=_row_tile(t, 256))

    tg = -(-t // GDN_CHUNK) * GDN_CHUNK
    ab3 = proj3[:, :, AB_COL:]
    if tg != t:
        padt = ((0, 0), (0, tg - t), (0, 0))
        q, k, v, ab3 = (jnp.pad(a, padt) for a in (q, k, v, ab3))
    abt = jnp.transpose(ab3[:, :, :2 * GDN_HEADS].reshape(b, tg // GDN_CHUNK, GDN_CHUNK, 2 * GDN_HEADS),
                        (0, 1, 3, 2))
    o, s_new = _gdn(q, k, v, ab3, abt, w["a_log"], w["dt_bias"], s0, tt=_row_tile(tg, 512), t_valid=t)
    o = o[:, :t]

    lc = min(t, MLP_CHUNK)
    y_b, vn = _mlp_branch(proj3, UV_COL // (2 * MLP_DIM), w["ln_g"], w["ln_b"], w["w_s"], w["b_s"],
                          rows=_row_tile(t, 512), lc=lc)

    h1 = _outproj(o.reshape(n, V_DIM), proj, Z_COL // V_DIM, y_b.reshape(n, MLP_DIM), x2,
                  w["gdn_norm"], w["w_out"], tm=_row_tile(n, 256))

    qp, xn2 = _norm_matmul(h1, w["norm2"], w["w_q"], tm=tm, tn=512, passes=1, emit_xn=True, name="peer_query")
    tb = 128
    eidx_t, gate_t = _peer_topk(qp, w["sub_keys"], tk=_row_tile(n, 512))
    rows = PEER_HEADS * PEER_TOPK
    gate_blocks = jnp.transpose(gate_t.reshape(rows, n // tb, tb), (1, 0, 2))
    h2 = _peer_gather(jnp.transpose(eidx_t), xn2, gate_blocks, h1, w["table"], tb=tb)

    y = _ple_final(h2, p.reshape(n, -1), w["norm3"], final_norm, w["w_ple_gate"], w["w_ple_proj"],
                   tm=_row_tile(n, 256))
    return y.reshape(b, t, d), s_new, new_buf, vn


def kernel(x_prompt, x_sample, state_gdn, cache_conv, p_prompt, p_sample, norm1, w_in, conv_w, a_log, dt_bias,
           gdn_norm, ln_g, ln_b, w_s, b_s, w_out, norm2, w_q, sub_keys, peer_u, peer_v, norm3, w_ple_gate,
           w_ple_proj, final_norm):
    assert norm1.shape[0] == 1, "single layer"
    w = dict(norm1=norm1[0], w_in=_relayout_w_in(w_in[0]), conv_w=conv_w[0], a_log=a_log[0], dt_bias=dt_bias[0],
             gdn_norm=gdn_norm[0], ln_g=ln_g[0], ln_b=ln_b[0], w_s=w_s[0], b_s=b_s[0], w_out=w_out[0],
             norm2=norm2[0], w_q=w_q[0], sub_keys=sub_keys[0], table=_pack_tables(peer_u[0], peer_v[0]),
             norm3=norm3[0], w_ple_gate=w_ple_gate[0], w_ple_proj=w_ple_proj[0])
    bp = x_prompt.shape[0]
    s0p = jnp.zeros((bp, GDN_HEADS, GDN_DK, GDN_DV), F32)
    b0p = jnp.zeros((bp, CONV_W - 1, CONV_DIM), F32)
    y_s, s_s, b_s_new, v_s = _layer(x_sample, p_sample[0], state_gdn[0], cache_conv[0], w, final_norm)
    y_p, s_p, b_p, _ = _layer(x_prompt, p_prompt[0], s0p, b0p, w, final_norm)
    return (y_p, y_s, s_p[None], b_p[None], s_s[None], b_s_new[None], v_s[None])
```

```python
import functools
import math

import jax
import jax.numpy as jnp
from jax import lax
from jax.experimental import pallas as pl
from jax.experimental.pallas import tpu as pltpu

F32 = jnp.float32
BF16 = jnp.bfloat16

EPS = 1e-6
GDN_CHUNK = 64
GDN_HEADS = 8
GDN_CHUNKS_PER_ITER = 4
GDN_DK = 128
GDN_DV = 128
CONV_W = 4
MLP_CHUNK = 128
MLP_GROUPS = 8
MLP_GROUP_DIM = 128
PEER_HEADS = 8
PEER_QHALF = 128
N_KEYS = 128
PEER_TOPK = 16
PEER_TOPK_HEADS_PER_ITER = 8

LANES = 128
SUBLANES = 8
VMEM_LIMIT_BYTES = 56 * 1024 * 1024


def _cparams(*sem):
    return pltpu.CompilerParams(dimension_semantics=sem, vmem_limit_bytes=VMEM_LIMIT_BYTES)


def _split3(x):
    hi = x.astype(BF16)
    r1 = x - hi.astype(F32)
    mid = r1.astype(BF16)
    lo = (r1 - mid.astype(F32)).astype(BF16)
    return hi, mid, lo


def _dg(a, b, dims):
    return lax.dot_general(a, b, (dims, ((), ())), preferred_element_type=F32)


_NN = ((1,), (0,))
_NT = ((1,), (1,))
_TN = ((0,), (0,))


def _dot1(a, b, dims=_NN):
    return _dg(a.astype(BF16), b.astype(BF16), dims)


def _dot3(a, b, dims=_NN):
    ah = a.astype(BF16)
    al = (a - ah.astype(F32)).astype(BF16)
    bh = b.astype(BF16)
    bl = (b - bh.astype(F32)).astype(BF16)
    return _dg(ah, bh, dims) + (_dg(ah, bl, dims) + _dg(al, bh, dims))


def _dot_exact_lhs(a_exact_bf16, b, dims=_NN):
    b0, b1, b2 = _split3(b)
    return _dg(a_exact_bf16, b0, dims) + (_dg(a_exact_bf16, b1, dims) + _dg(a_exact_bf16, b2, dims))


def _dot_exact_rhs(a, b_exact_bf16, dims=_NN):
    a0, a1, a2 = _split3(a)
    return _dg(a0, b_exact_bf16, dims) + (_dg(a1, b_exact_bf16, dims) + _dg(a2, b_exact_bf16, dims))


def _rms_rows(x, gain):
    ms = jnp.mean(x * x, axis=-1, keepdims=True)
    return x * lax.rsqrt(ms + EPS) * gain


def _sigmoid(x):
    return 1.0 / (1.0 + jnp.exp(-x))


def _silu(x):
    return x * _sigmoid(x)


def _gelu(x):
    return 0.5 * x * (1.0 + lax.erf(x * (1.0 / math.sqrt(2.0))))


def _softplus(x):
    return jnp.maximum(x, 0.0) + jnp.log1p(jnp.exp(-jnp.abs(x)))


def _norm_matmul_kernel(x_ref, g_ref, *refs, passes, emit_xn):
    if passes == 1:
        w_refs, rest = refs[:1], refs[1:]
    else:
        w_refs, rest = refs[:2], refs[2:]
    if emit_xn:
        o_ref, xn_out_ref = rest[0], rest[1]
        scr = rest[2:]
    else:
        o_ref = rest[0]
        scr = rest[1:]
    j = pl.program_id(1)

    @pl.when(j == 0)
    def _():
        xn = _rms_rows(x_ref[...], g_ref[...])
        if emit_xn:
            xn_out_ref[...] = xn
        hi = xn.astype(BF16)
        scr[0][...] = hi
        if passes == 3:
            scr[1][...] = (xn - hi.astype(F32)).astype(BF16)

    if passes == 1:
        o_ref[...] = _dg(scr[0][...], w_refs[0][...], _NN)
    else:
        xh = scr[0][...]
        wh = w_refs[0][...]
        o_ref[...] = _dg(xh, wh, _NN) + (_dg(xh, w_refs[1][...], _NN) + _dg(scr[1][...], wh, _NN))


def _norm_matmul(x, gain, w, *, tm, tn, passes=1, emit_xn=False, name):
    n, d = x.shape
    m = w.shape[1]
    assert n % tm == 0 and m % tn == 0
    wh = w.astype(BF16)
    ws = [wh] if passes == 1 else [wh, (w - wh.astype(F32)).astype(BF16)]
    w_specs = [pl.BlockSpec((d, tn), lambda i, j: (0, j)) for _ in ws]
    out_shape = [jax.ShapeDtypeStruct((n, m), F32)]
    out_specs = [pl.BlockSpec((tm, tn), lambda i, j: (i, j))]
    if emit_xn:
        out_shape.append(jax.ShapeDtypeStruct((n, d), F32))
        out_specs.append(pl.BlockSpec((tm, d), lambda i, j: (i, 0)))
    scratch = [pltpu.VMEM((tm, d), BF16) for _ in range(1 if passes == 1 else 2)]
    res = pl.pallas_call(
        functools.partial(_norm_matmul_kernel, passes=passes, emit_xn=emit_xn),
        grid=(n // tm, m // tn),
        in_specs=[pl.BlockSpec((tm, d), lambda i, j: (i, 0)),
                  pl.BlockSpec((1, d), lambda i, j: (0, 0))] + w_specs,
        out_specs=out_specs,
        out_shape=out_shape,
        scratch_shapes=scratch,
        compiler_params=_cparams("parallel", "arbitrary"),
        name=name,
    )(x, gain.reshape(1, d), *ws)
    return res if emit_xn else res[0]


def _norm_matmul_resident_kernel(x_ref, g_ref, w_ref, o_ref, *xn_out, emit_xn):
    xn = _rms_rows(x_ref[...], g_ref[...])
    if emit_xn:
        xn_out[0][...] = xn
    o_ref[...] = _dg(xn.astype(BF16), w_ref[...], _NN)


def _norm_matmul_resident(x, gain, w, *, tm, emit_xn=False, name):
    n, d = x.shape
    m = w.shape[1]
    assert n % tm == 0
    out_shape = [jax.ShapeDtypeStruct((n, m), F32)]
    out_specs = [pl.BlockSpec((tm, m), lambda i: (i, 0))]
    if emit_xn:
        out_shape.append(jax.ShapeDtypeStruct((n, d), F32))
        out_specs.append(pl.BlockSpec((tm, d), lambda i: (i, 0)))
    res = pl.pallas_call(
        functools.partial(_norm_matmul_resident_kernel, emit_xn=emit_xn),
        grid=(n // tm,),
        in_specs=[pl.BlockSpec((tm, d), lambda i: (i, 0)),
                  pl.BlockSpec((1, d), lambda i: (0, 0)),
                  pl.BlockSpec((d, m), lambda i: (0, 0), pipeline_mode=pl.Buffered(1))],
        out_specs=out_specs,
        out_shape=out_shape,
        compiler_params=_cparams("parallel"),
        name=name,
    )(x, gain.reshape(1, d), w.astype(BF16))
    return res if emit_xn else res[0]


def _conv_kernel(x_ref, buf_ref, w_ref, q_ref, k_ref, v_ref, xp_ref, *, tt):
    t = pl.program_id(1)
    halo = SUBLANES

    @pl.when(t == 0)
    def _():
        xp_ref[0:halo, :] = buf_ref[0]

    @pl.when(t > 0)
    def _():
        xp_ref[0:halo, :] = xp_ref[tt:tt + halo, :]

    xp_ref[halo:halo + tt, :] = x_ref[0]
    base = halo - (CONV_W - 1)
    y = xp_ref[base:base + tt, :] * w_ref[0:1, :]
    for j in range(1, CONV_W):
        y = y + xp_ref[base + j:base + j + tt, :] * w_ref[j:j + 1, :]
    y = _silu(y)
    qk_dim = GDN_HEADS * GDN_DK
    for h in range(GDN_HEADS):
        qh = y[:, h * GDN_DK:(h + 1) * GDN_DK]
        qn = qh * lax.rsqrt(jnp.sum(qh * qh, axis=-1, keepdims=True) + EPS)
        q_ref[0, :, h * GDN_DK:(h + 1) * GDN_DK] = qn * (GDN_DK ** -0.5)
        kh = y[:, qk_dim + h * GDN_DK:qk_dim + (h + 1) * GDN_DK]
        k_ref[0, :, h * GDN_DK:(h + 1) * GDN_DK] = kh * lax.rsqrt(jnp.sum(kh * kh, axis=-1, keepdims=True) + EPS)
    v_ref[0] = y[:, 2 * qk_dim:]


def _conv_qkv(proj3, buf8, conv_w, *, tt):
    b, t, _ = proj3.shape
    cdim = conv_w.shape[1]
    hd = GDN_HEADS * GDN_DK
    assert t % tt == 0
    shp = jax.ShapeDtypeStruct((b, t, hd), F32)
    ospec = pl.BlockSpec((1, tt, hd), lambda i, j: (i, j, 0))
    return pl.pallas_call(
        functools.partial(_conv_kernel, tt=tt),
        grid=(b, t // tt),
        in_specs=[pl.BlockSpec((1, tt, cdim), lambda i, j: (i, j, 0)),
                  pl.BlockSpec((1, SUBLANES, cdim), lambda i, j: (i, 0, 0)),
                  pl.BlockSpec((CONV_W, cdim), lambda i, j: (0, 0))],
        out_specs=[ospec, ospec, ospec],
        out_shape=[shp, shp, shp],
        scratch_shapes=[pltpu.VMEM((tt + 2 * SUBLANES, cdim), F32)],
        compiler_params=_cparams("parallel", "arbitrary"),
        name="conv_qkv",
    )(proj3, buf8, conv_w)


def _gdn_kernel(q_ref, k_ref, v_ref, ab_ref, abt_ref, prm_row_ref, prm_col_ref, s0_ref,
                o_ref, sout_ref, state_ref, *, tt, t_valid):
    tstep = pl.program_id(1)
    c = GDN_CHUNK

    @pl.when(tstep == 0)
    def _():
        state_ref[...] = s0_ref[0]

    ii = lax.broadcasted_iota(jnp.int32, (c, c), 0)
    jj = lax.broadcasted_iota(jnp.int32, (c, c), 1)
    incl = ii >= jj
    strict = ii > jj
    tri = incl.astype(BF16)
    tri_t = (jj >= ii).astype(BF16)
    eye = (ii == jj).astype(F32)

    alog_row = prm_row_ref[0:1, :]
    dtb_row = prm_row_ref[1:2, :]
    alog_col = prm_col_ref[:, 0:1]
    dtb_col = prm_col_ref[:, 1:2]

    def chunk_units(ci):
        r0 = pl.multiple_of(ci * c, c)
        tpos = tstep * tt + r0
        a_blk = ab_ref[0, pl.ds(r0, c), :]
        valid_col = (tpos + lax.broadcasted_iota(jnp.int32, (c, LANES), 0)) < t_valid
        g_col = jnp.where(valid_col, -jnp.exp(alog_row) * _softplus(a_blk + dtb_row), 0.0)
        beta_col = jnp.where(valid_col, _sigmoid(a_blk), 0.0)
        gc_col = _dot_exact_lhs(tri, g_col)
        at_blk = abt_ref[0, ci]
        valid_row = (tpos + lax.broadcasted_iota(jnp.int32, (2 * GDN_HEADS, c), 1)) < t_valid
        g_row = jnp.where(valid_row, -jnp.exp(alog_col) * _softplus(at_blk + dtb_col), 0.0)
        gc_row = _dot_exact_rhs(g_row, tri_t)
        units = []
        for h in range(GDN_HEADS):
            sl = slice(h * GDN_DK, (h + 1) * GDN_DK)
            qh = q_ref[0, pl.ds(r0, c), sl]
            kh = k_ref[0, pl.ds(r0, c), sl]
            vh = v_ref[0, pl.ds(r0, c), sl]
            gc = gc_col[:, h:h + 1]
            beta = beta_col[:, GDN_HEADS + h:GDN_HEADS + h + 1]
            gcr = gc_row[h:h + 1, :]
            gc_last = gc_col[c - 1:c, h:h + 1]
            decay = jnp.where(incl, jnp.exp(jnp.where(incl, gc - gcr, 0.0)), 0.0)
            egc = jnp.exp(gc)
            kb = kh * beta
            lmat = jnp.where(strict, _dot1(kb, kh, _NT) * decay, 0.0)
            units.append(dict(
                h=h, r0=r0, sl=sl, tinv=eye - lmat, pw=lmat,
                rhs=jnp.concatenate([vh * beta, kb * egc], axis=-1),
                attn=_dot1(qh, kh, _NT) * decay, q_dec=qh * egc,
                k_dec=kh * jnp.exp(gc_last - gc), bd=jnp.exp(gc_last)))
        return units

    def chunks_body(cp, carry):
        per_chunk = [chunk_units(cp * cpi + j) for j in range(cpi)]
        units = [u for us in per_chunk for u in us]
        for _ in range(5):
            for u in units:
                u["pw"] = _dot1(u["pw"], u["pw"])
            for u in units:
                u["tinv"] = u["tinv"] + _dot1(u["tinv"], u["pw"])
        for u in units:
            u["sol"] = _dot3(u["tinv"], u["rhs"])
        for us in per_chunk:
            for u in us:
                s = state_ref[u["h"]]
                u["s"] = s
                u["v_new"] = u["sol"][:, :GDN_DV] - _dot1(u["sol"][:, GDN_DV:], s)
            for u in us:
                o_ref[0, pl.ds(u["r0"], c), u["sl"]] = _dot1(u["q_dec"], u["s"]) + _dot1(u["attn"], u["v_new"])
                state_ref[u["h"]] = u["s"] * u["bd"] + _dot1(u["k_dec"], u["v_new"], _TN)
        return carry

    n_chunks = tt // c
    cpi = GDN_CHUNKS_PER_ITER if n_chunks % GDN_CHUNKS_PER_ITER == 0 else 1
    lax.fori_loop(0, n_chunks // cpi, chunks_body, 0)

    @pl.when(tstep == pl.num_programs(1) - 1)
    def _():
        sout_ref[0] = state_ref[...]


def _gdn(q, k, v, ab, abt, a_log, dt_bias, s0, *, tt, t_valid):
    b, t, hd = q.shape
    assert t % tt == 0 and tt % GDN_CHUNK == 0
    prm_row = jnp.zeros((SUBLANES, LANES), F32)
    prm_row = prm_row.at[0, :GDN_HEADS].set(a_log).at[1, :GDN_HEADS].set(dt_bias)
    prm_col = jnp.zeros((2 * GDN_HEADS, LANES), F32)
    prm_col = prm_col.at[:GDN_HEADS, 0].set(a_log).at[:GDN_HEADS, 1].set(dt_bias)
    seq = pl.BlockSpec((1, tt, hd), lambda i, j: (i, j, 0))
    st = pl.BlockSpec((1, GDN_HEADS, GDN_DK, GDN_DV), lambda i, j: (i, 0, 0, 0))
    return pl.pallas_call(
        functools.partial(_gdn_kernel, tt=tt, t_valid=t_valid),
        grid=(b, t // tt),
        in_specs=[seq, seq, seq,
                  pl.BlockSpec((1, tt, LANES), lambda i, j: (i, j, 0)),
                  pl.BlockSpec((1, tt // GDN_CHUNK, 2 * GDN_HEADS, GDN_CHUNK), lambda i, j: (i, j, 0, 0)),
                  pl.BlockSpec((SUBLANES, LANES), lambda i, j: (0, 0)),
                  pl.BlockSpec((2 * GDN_HEADS, LANES), lambda i, j: (0, 0)),
                  st],
        out_specs=[seq, st],
        out_shape=[jax.ShapeDtypeStruct((b, t, hd), F32),
                   jax.ShapeDtypeStruct((b, GDN_HEADS, GDN_DK, GDN_DV), F32)],
        scratch_shapes=[pltpu.VMEM((GDN_HEADS, GDN_DK, GDN_DV), F32)],
        compiler_params=_cparams("parallel", "arbitrary"),
        name="gdn_core",
    )(q, k, v, ab, abt, prm_row, prm_col, s0)


def _mlp_kernel(uv_ref, lng_ref, lnb_ref, ws_ref, bst_ref, y_ref, vn_ref, *, rows, lc):
    mdim = MLP_GROUPS * MLP_GROUP_DIM
    ii = lax.broadcasted_iota(jnp.int32, (lc, lc), 0)
    jj = lax.broadcasted_iota(jnp.int32, (lc, lc), 1)
    keep = ii >= jj
    for r in range(rows // lc):
        rs = slice(r * lc, (r + 1) * lc)
        g = _gelu(uv_ref[0, rs, :])
        u = g[:, :mdim]
        vv = g[:, mdim:]
        mu = jnp.mean(vv, axis=-1, keepdims=True)
        xc = vv - mu
        vn = xc * lax.rsqrt(jnp.mean(xc * xc, axis=-1, keepdims=True) + EPS) * lng_ref[...] + lnb_ref[...]
        vn_ref[0, rs, :] = vn
        for gi in range(MLP_GROUPS):
            gs = slice(gi * MLP_GROUP_DIM, (gi + 1) * MLP_GROUP_DIM)
            wm = jnp.where(keep, ws_ref[gi], 0.0)
            s = _dot1(wm, vn[:, gs]) + bst_ref[:, gi:gi + 1]
            y_ref[0, rs, gs] = u[:, gs] * s


def _mlp_branch(proj3, col_block, ln_g, ln_b, w_s, b_s, *, rows, lc):
    b, t, _ = proj3.shape
    mdim = MLP_GROUPS * MLP_GROUP_DIM
    assert t % rows == 0 and rows % lc == 0
    ws = w_s[:, :lc, :lc]
    bst = jnp.transpose(b_s[:, :lc])
    shp = jax.ShapeDtypeStruct((b, t, mdim), F32)
    ospec = pl.BlockSpec((1, rows, mdim), lambda i, j: (i, j, 0))
    return pl.pallas_call(
        functools.partial(_mlp_kernel, rows=rows, lc=lc),
        grid=(b, t // rows),
        in_specs=[pl.BlockSpec((1, rows, 2 * mdim), lambda i, j: (i, j, col_block)),
                  pl.BlockSpec((1, mdim), lambda i, j: (0, 0)),
                  pl.BlockSpec((1, mdim), lambda i, j: (0, 0)),
                  pl.BlockSpec((MLP_GROUPS, lc, lc), lambda i, j: (0, 0, 0)),
                  pl.BlockSpec((lc, MLP_GROUPS), lambda i, j: (0, 0))],
        out_specs=[ospec, ospec],
        out_shape=[shp, shp],
        compiler_params=_cparams("parallel", "parallel"),
        name="mlp_branch",
    )(proj3, ln_g.reshape(1, mdim), ln_b.reshape(1, mdim), ws, bst)


def _outproj_kernel(o_ref, z_ref, y_ref, h_ref, gn_ref, w_ref, out_ref):
    parts = []
    for hh in range(GDN_HEADS):
        sl = slice(hh * GDN_DV, (hh + 1) * GDN_DV)
        parts.append((_rms_rows(o_ref[:, sl], gn_ref[...]) * _silu(z_ref[:, sl])).astype(BF16))
    parts.append(y_ref[...].astype(BF16))
    cat = jnp.concatenate(parts, axis=-1)
    out_ref[...] = h_ref[...] + _dg(cat, w_ref[...], _NN)


def _outproj(o2, proj2, z_block, y2, h2, gdn_norm, w_out, *, tm):
    n, d = h2.shape
    vd = GDN_HEADS * GDN_DV
    assert n % tm == 0
    return pl.pallas_call(
        _outproj_kernel,
        grid=(n // tm,),
        in_specs=[pl.BlockSpec((tm, vd), lambda i: (i, 0)),
                  pl.BlockSpec((tm, vd), lambda i: (i, z_block)),
                  pl.BlockSpec((tm, vd), lambda i: (i, 0)),
                  pl.BlockSpec((tm, d), lambda i: (i, 0)),
                  pl.BlockSpec((1, GDN_DV), lambda i: (0, 0)),
                  pl.BlockSpec(w_out.shape, lambda i: (0, 0))],
        out_specs=pl.BlockSpec((tm, d), lambda i: (i, 0)),
        out_shape=jax.ShapeDtypeStruct((n, d), F32),
        compiler_params=_cparams("parallel"),
        name="out_proj",
    )(o2, proj2, y2, h2, gdn_norm.reshape(1, GDN_DV), w_out.astype(BF16))


def _topk_axis0(x, k, payloads=()):
    r = x.shape[0]
    iota = lax.broadcasted_iota(jnp.int32, x.shape, 0)
    vals, idxs = [], []
    outs = [[] for _ in payloads]
    for _ in range(k):
        m = jnp.max(x, axis=0, keepdims=True)
        i = jnp.min(jnp.where(x == m, iota, r), axis=0, keepdims=True)
        hit = iota == i
        vals.append(m)
        idxs.append(i)
        for p, acc in zip(payloads, outs):
            acc.append(jnp.sum(jnp.where(hit, p, 0), axis=0, keepdims=True))
        x = jnp.where(hit, -jnp.inf, x)
    cat = lambda parts: jnp.concatenate(parts, axis=0)
    return cat(vals), cat(idxs), [cat(acc) for acc in outs]


def _peer_topk_kernel(q_ref, keys_ref, eidx_ref, gate_ref):
    k = PEER_TOPK

    def one_head(h):
        sv, si = [], []
        for s in range(2):
            c0 = pl.multiple_of(h * (2 * PEER_QHALF) + s * PEER_QHALF, PEER_QHALF)
            qhs = q_ref[:, pl.ds(c0, PEER_QHALF)]
            sc_t = _dot3(keys_ref[s, h], qhs, _NT)
            v, i, _ = _topk_axis0(sc_t, k)
            sv.append(v)
            si.append(i)
        half = k // 2
        sub = lax.broadcasted_iota(jnp.int32, (half, sv[0].shape[1]), 0)
        cand_parts = [sv[0][0:1, :] + sv[1]]
        cidx_parts = [si[0][0:1, :] * N_KEYS + si[1]]
        for a in range(1, half):
            keep = sub < (k // (a + 1))
            cand_parts.append(jnp.where(keep, sv[0][a:a + 1, :] + sv[1][0:half, :], -jnp.inf))
            cidx_parts.append(si[0][a:a + 1, :] * N_KEYS + si[1][0:half, :])
        cand_parts.append(sv[0][half:, :] + sv[1][0:1, :])
        cidx_parts.append(si[0][half:, :] * N_KEYS + si[1][0:1, :])
        cand = jnp.concatenate(cand_parts, axis=0)
        cidx = jnp.concatenate(cidx_parts, axis=0)
        fv, _, (fe,) = _topk_axis0(cand, k, (cidx,))
        e = jnp.exp(fv - fv[0:1, :])
        gate = e / jnp.sum(e, axis=0, keepdims=True)
        r0 = pl.multiple_of(h * k, k)
        eidx_ref[pl.ds(r0, k), :] = fe
        gate_ref[pl.ds(r0, k), :] = gate

    def heads_body(hp, carry):
        for j in range(PEER_TOPK_HEADS_PER_ITER):
            one_head(hp * PEER_TOPK_HEADS_PER_ITER + j)
        return carry

    lax.fori_loop(0, PEER_HEADS // PEER_TOPK_HEADS_PER_ITER, heads_body, 0)


def _peer_topk(q, sub_keys, *, tk):
    n, qd = q.shape
    assert n % tk == 0
    rows = PEER_HEADS * PEER_TOPK
    return pl.pallas_call(
        _peer_topk_kernel,
        grid=(n // tk,),
        in_specs=[pl.BlockSpec((tk, qd), lambda i: (i, 0)),
                  pl.BlockSpec(sub_keys.shape, lambda i: (0, 0, 0, 0))],
        out_specs=[pl.BlockSpec((rows, tk), lambda i: (0, i)),
                   pl.BlockSpec((rows, tk), lambda i: (0, i))],
        out_shape=[jax.ShapeDtypeStruct((rows, n), jnp.int32),
                   jax.ShapeDtypeStruct((rows, n), F32)],
        compiler_params=_cparams("parallel"),
        name="peer_topk",
    )(q, sub_keys)


PEER_ROWS = PEER_HEADS * PEER_TOPK
PEER_SLOTS = 16
PEER_LOOK = PEER_SLOTS - 1
assert PEER_SLOTS % 2 == 0 and PEER_LOOK % 2 == 1
PEER_IDX_TAIL = 16
assert PEER_LOOK <= PEER_IDX_TAIL and (PEER_IDX_TAIL * PEER_ROWS) % 1024 == 0
PEER_ACCS = 4
PEER_DOT_COPIES = 5
PEER_MIX_COPIES = 3
assert (PEER_DOT_COPIES + PEER_MIX_COPIES) * (PEER_ROWS // SUBLANES) == PEER_ROWS


def _sublane_fold(parts):
    sub = lax.broadcasted_iota(jnp.int32, parts[0].shape, 0)
    for dist in (4, 2, 1):
        low = (sub & dist) == 0
        nxt = []
        for a in range(len(parts) // 2):
            lo_src, hi_src = parts[a], parts[a + len(parts) // 2]
            kept = jnp.where(low, lo_src, hi_src)
            if 2 * dist == SUBLANES:
                moved = pltpu.roll(jnp.where(low, hi_src, lo_src), dist, axis=0)
            else:
                moved = jnp.where(low, pltpu.roll(lo_src, SUBLANES - dist, axis=0), pltpu.roll(hi_src, dist, axis=0))
            nxt.append(kept + moved)
        parts = nxt
    return parts[0]


def _peer_gather_kernel(idx_ref, x_ref, gate_ref, h_ref, tab_ref, out_ref, buf_even, buf_odd, sem_ref, coef_ref, *, tb):
    rows = PEER_ROWS
    nc = x_ref.shape[1] // LANES
    half = nc // 2
    groups = rows // SUBLANES
    step = pl.program_id(0)
    hi_mask = jnp.uint32(0xFFFF0000)
    bufs = (buf_even, buf_odd)

    def slot_of(tok):
        return (tok // 2) % (PEER_SLOTS // 2)

    def start_rows(arr, tok, r0, count):
        for j in range(count):
            r = r0 + j
            pltpu.make_async_copy(tab_ref.at[idx_ref[tok * rows + r]], arr.at[slot_of(tok), r],
                                  sem_ref.at[tok % PEER_SLOTS]).start(priority=j % 2)

    def wait_token(arr, tok):
        pltpu.make_async_copy(tab_ref.at[pl.ds(0, rows)], arr.at[slot_of(tok)], sem_ref.at[tok % PEER_SLOTS]).wait()

    def load_x(t):
        xrow = x_ref[pl.ds(t, 1), :]
        x_lo = jnp.concatenate([xrow[:, c * LANES:(c + 1) * LANES] for c in range(half)], axis=0)
        x_hi = jnp.concatenate([xrow[:, c * LANES:(c + 1) * LANES] for c in range(half, nc)], axis=0)
        return x_lo, x_hi

    def dot_group(buf, x, g):
        parts = []
        for s in range(SUBLANES):
            r = g * SUBLANES + s
            u_lo = lax.bitcast_convert_type(buf[r, 0:half, :] << 16, F32)
            u_hi = lax.bitcast_convert_type(buf[r, half:nc, :] << 16, F32)
            parts.append(u_lo * x[0] + u_hi * x[1])
        return _sublane_fold(parts)

    def finish_dot(t, folded):
        act = jnp.sum(jnp.concatenate(folded, axis=0), axis=-1, keepdims=True)
        lane_tok = lax.broadcasted_iota(jnp.int32, (rows, tb), 1)
        gcol = jnp.sum(jnp.where(lane_tok == t, gate_ref[0], 0.0), axis=-1, keepdims=True)
        coef_ref[...] = jnp.broadcast_to(gcol * _gelu(act), (rows, LANES))

    def mix_group(buf, g, acc_lo, acc_hi):
        for s in range(SUBLANES):
            r = g * SUBLANES + s
            c = jnp.broadcast_to(coef_ref[r:r + 1, :], (half, LANES))
            a = r % PEER_ACCS
            acc_lo[a] = acc_lo[a] + lax.bitcast_convert_type(buf[r, 0:half, :] & hi_mask, F32) * c
            acc_hi[a] = acc_hi[a] + lax.bitcast_convert_type(buf[r, half:nc, :] & hi_mask, F32) * c

    def emit(t, acc_lo, acc_hi):
        mix_lo = (acc_lo[0] + acc_lo[1]) + (acc_lo[2] + acc_lo[3])
        mix_hi = (acc_hi[0] + acc_hi[1]) + (acc_hi[2] + acc_hi[3])
        mix_row = jnp.concatenate([mix_lo[c:c + 1, :] for c in range(half)]
                                  + [mix_hi[c:c + 1, :] for c in range(half)], axis=1)
        out_ref[pl.ds(t, 1), :] = h_ref[pl.ds(t, 1), :] + mix_row

    def token(t, parity):
        arr, arr_next = bufs[parity], bufs[(parity + PEER_LOOK) % 2]
        tok_next = t + PEER_LOOK
        wait_token(arr, t)
        buf = arr.at[slot_of(t)]
        x = load_x(t)
        folded = []
        for g in range(groups):
            start_rows(arr_next, tok_next, g * PEER_DOT_COPIES, PEER_DOT_COPIES)
            folded.append(dot_group(buf, x, g))
        finish_dot(t, folded)
        acc_lo = [jnp.zeros((half, LANES), F32) for _ in range(PEER_ACCS)]
        acc_hi = [jnp.zeros((half, LANES), F32) for _ in range(PEER_ACCS)]
        for g in range(groups):
            start_rows(arr_next, tok_next, groups * PEER_DOT_COPIES + g * PEER_MIX_COPIES, PEER_MIX_COPIES)
            mix_group(buf, g, acc_lo, acc_hi)
        emit(t, acc_lo, acc_hi)

    @pl.when(step == 0)
    def _():
        for s in range(PEER_LOOK):
            def prime(i, carry, s=s):
                start_rows(bufs[s % 2], s, i * SUBLANES, SUBLANES)
                return carry
            lax.fori_loop(0, groups, prime, 0)

    def pair(p, carry):
        token(2 * p, 0)
        token(2 * p + 1, 1)
        return carry

    lax.fori_loop(0, tb // 2, pair, 0)

    @pl.when(step == pl.num_programs(0) - 1)
    def _():
        for s in range(PEER_LOOK):
            wait_token(bufs[(tb + s) % 2], tb + s)


def _peer_gather(eidx, xn, gate_blocks, h, table, *, tb):
    n, d = h.shape
    rows = PEER_ROWS
    assert n % tb == 0 and tb % PEER_SLOTS == 0 and PEER_ACCS == 4
    nb = n // tb
    idx2 = eidx.reshape(nb, tb * rows)
    idx_ext = jnp.concatenate([idx2, jnp.roll(idx2[:, :PEER_IDX_TAIL * rows], -1, axis=0)], axis=1).reshape(-1)
    return pl.pallas_call(
        functools.partial(_peer_gather_kernel, tb=tb),
        grid=(nb,),
        in_specs=[pl.BlockSpec(((tb + PEER_IDX_TAIL) * rows,), lambda i: (i,), memory_space=pltpu.SMEM),
                  pl.BlockSpec((tb, d), lambda i: (i, 0)),
                  pl.BlockSpec((1, rows, tb), lambda i: (i, 0, 0)),
                  pl.BlockSpec((tb, d), lambda i: (i, 0)),
                  pl.BlockSpec(memory_space=pl.ANY)],
        out_specs=pl.BlockSpec((tb, d), lambda i: (i, 0)),
        out_shape=jax.ShapeDtypeStruct((n, d), F32),
        scratch_shapes=[pltpu.VMEM((PEER_SLOTS // 2, rows, d // LANES, LANES), jnp.uint32),
                        pltpu.VMEM((PEER_SLOTS // 2, rows, d // LANES, LANES), jnp.uint32),
                        pltpu.SemaphoreType.DMA((PEER_SLOTS,)),
                        pltpu.VMEM((rows, LANES), F32)],
        compiler_params=_cparams("arbitrary"),
        name="peer_gather",
    )(idx_ext, xn, gate_blocks, h, table)


def _ple_kernel(h_ref, p_ref, n3_ref, fn_ref, wg_ref, wp_ref, y_ref):
    h = h_ref[...]
    gate = _sigmoid(_dot1(_rms_rows(h, n3_ref[...]), wg_ref[...]))
    h = h + gate * _dot1(p_ref[...], wp_ref[...])
    y_ref[...] = _rms_rows(h, fn_ref[...])


def _ple_final(h, p, norm3, final_norm, w_gate, w_proj, *, tm):
    n, d = h.shape
    pd = p.shape[1]
    assert n % tm == 0
    return pl.pallas_call(
        _ple_kernel,
        grid=(n // tm,),
        in_specs=[pl.BlockSpec((tm, d), lambda i: (i, 0)),
                  pl.BlockSpec((tm, pd), lambda i: (i, 0)),
                  pl.BlockSpec((1, d), lambda i: (0, 0)),
                  pl.BlockSpec((1, d), lambda i: (0, 0)),
                  pl.BlockSpec((d, d), lambda i: (0, 0)),
                  pl.BlockSpec((pd, d), lambda i: (0, 0))],
        out_specs=pl.BlockSpec((tm, d), lambda i: (i, 0)),
        out_shape=jax.ShapeDtypeStruct((n, d), F32),
        compiler_params=_cparams("parallel"),
        name="ple_final",
    )(h, p, norm3.reshape(1, d), final_norm.reshape(1, d), w_gate.astype(BF16), w_proj.astype(BF16))


QK_DIM = GDN_HEADS * GDN_DK
V_DIM = GDN_HEADS * GDN_DV
CONV_DIM = 2 * QK_DIM + V_DIM
MLP_DIM = MLP_GROUPS * MLP_GROUP_DIM
Z_COL = CONV_DIM
UV_COL = Z_COL + V_DIM
AB_COL = UV_COL + 2 * MLP_DIM
PROJ_COLS = AB_COL + LANES
PROJ_TN = 896


def _relayout_w_in(w_in):
    c0 = CONV_DIM
    c2 = c0 + 2 * GDN_HEADS
    c3 = c2 + V_DIM
    ab = jnp.pad(w_in[:, c0:c2], ((0, 0), (0, LANES - 2 * GDN_HEADS)))
    return jnp.concatenate([w_in[:, :c0], w_in[:, c2:c3], w_in[:, c3:], ab], axis=1)


def _pack_tables(peer_u, peer_v):
    ub = lax.bitcast_convert_type(peer_u.astype(BF16), jnp.uint16).astype(jnp.uint32)
    vb = lax.bitcast_convert_type(peer_v.astype(BF16), jnp.uint16).astype(jnp.uint32)
    e, d = peer_u.shape
    return (ub | (vb << 16)).reshape(e, d // LANES, LANES)


def _row_tile(n, want):
    t = min(n, want)
    assert n % t == 0
    return t


def _layer(x, p, s0, buf, w, final_norm):
    b, t, d = x.shape
    n = b * t
    x2 = x.reshape(n, d)
    tm = _row_tile(n, 1024)

    proj = _norm_matmul_resident(x2, w["norm1"], w["w_in"], tm=_row_tile(n, 256), name="in_proj")
    proj3 = proj.reshape(b, t, PROJ_COLS)
    new_buf = jnp.concatenate([buf, proj3[:, :, :CONV_DIM]], axis=1)[:, t:]

    buf8 = jnp.pad(buf, ((0, 0), (SUBLANES - (CONV_W - 1), 0), (0, 0)))
    q, k, v = _conv_qkv(proj3, buf8, w["conv_w"], tt=_row_tile(t, 256))

    tg = -(-t // GDN_CHUNK) * GDN_CHUNK
    ab3 = proj3[:, :, AB_COL:]
    if tg != t:
        padt = ((0, 0), (0, tg - t), (0, 0))
        q, k, v, ab3 = (jnp.pad(a, padt) for a in (q, k, v, ab3))
    abt = jnp.transpose(ab3[:, :, :2 * GDN_HEADS].reshape(b, tg // GDN_CHUNK, GDN_CHUNK, 2 * GDN_HEADS),
                        (0, 1, 3, 2))
    o, s_new = _gdn(q, k, v, ab3, abt, w["a_log"], w["dt_bias"], s0, tt=_row_tile(tg, 512), t_valid=t)
    o = o[:, :t]

    lc = min(t, MLP_CHUNK)
    y_b, vn = _mlp_branch(proj3, UV_COL // (2 * MLP_DIM), w["ln_g"], w["ln_b"], w["w_s"], w["b_s"],
                          rows=_row_tile(t, 512), lc=lc)

    h1 = _outproj(o.reshape(n, V_DIM), proj, Z_COL // V_DIM, y_b.reshape(n, MLP_DIM), x2,
                  w["gdn_norm"], w["w_out"], tm=_row_tile(n, 256))

    qp, xn2 = _norm_matmul_resident(h1, w["norm2"], w["w_q"], tm=_row_tile(n, 512), emit_xn=True, name="peer_query")
    tb = 128
    eidx_t, gate_t = _peer_topk(qp, w["sub_keys"], tk=_row_tile(n, 512))
    rows = PEER_HEADS * PEER_TOPK
    gate_blocks = jnp.transpose(gate_t.reshape(rows, n // tb, tb), (1, 0, 2))
    h2 = _peer_gather(jnp.transpose(eidx_t), xn2, gate_blocks, h1, w["table"], tb=tb)

    y = _ple_final(h2, p.reshape(n, -1), w["norm3"], final_norm, w["w_ple_gate"], w["w_ple_proj"],
                   tm=_row_tile(n, 256))
    return y.reshape(b, t, d), s_new, new_buf, vn


def kernel(x_prompt, x_sample, state_gdn, cache_conv, p_prompt, p_sample, norm1, w_in, conv_w, a_log, dt_bias,
           gdn_norm, ln_g, ln_b, w_s, b_s, w_out, norm2, w_q, sub_keys, peer_u, peer_v, norm3, w_ple_gate,
           w_ple_proj, final_norm):
    assert norm1.shape[0] == 1, "single layer"
    w = dict(norm1=norm1[0], w_in=_relayout_w_in(w_in[0]), conv_w=conv_w[0], a_log=a_log[0], dt_bias=dt_bias[0],
             gdn_norm=gdn_norm[0], ln_g=ln_g[0], ln_b=ln_b[0], w_s=w_s[0], b_s=b_s[0], w_out=w_out[0],
             norm2=norm2[0], w_q=w_q[0], sub_keys=sub_keys[0], table=_pack_tables(peer_u[0], peer_v[0]),
             norm3=norm3[0], w_ple_gate=w_ple_gate[0], w_ple_proj=w_ple_proj[0])
    bp = x_prompt.shape[0]
    s0p = jnp.zeros((bp, GDN_HEADS, GDN_DK, GDN_DV), F32)
    b0p = jnp.zeros((bp, CONV_W - 1, CONV_DIM), F32)
    y_s, s_s, b_s_new, v_s = _layer(x_sample, p_sample[0], state_gdn[0], cache_conv[0], w, final_norm)
    y_p, s_p, b_p, _ = _layer(x_prompt, p_prompt[0], s0p, b0p, w, final_norm)
    return (y_p, y_s, s_p[None], b_p[None], s_s[None], b_s_new[None], v_s[None])
```

```python
import functools
import math

import jax
import jax.numpy as jnp
from jax import lax
from jax.experimental import pallas as pl
from jax.experimental.pallas import tpu as pltpu

F32 = jnp.float32
BF16 = jnp.bfloat16

EPS = 1e-6
GDN_CHUNK = 64
GDN_HEADS = 8
GDN_CHUNKS_PER_ITER = 4
GDN_DK = 128
GDN_DV = 128
CONV_W = 4
MLP_CHUNK = 128
MLP_GROUPS = 8
MLP_GROUP_DIM = 128
PEER_HEADS = 8
PEER_QHALF = 128
N_KEYS = 128
PEER_TOPK = 16
PEER_TOPK_HEADS_PER_ITER = 8

LANES = 128
SUBLANES = 8
VMEM_LIMIT_BYTES = 56 * 1024 * 1024


def _cparams(*sem):
    return pltpu.CompilerParams(dimension_semantics=sem, vmem_limit_bytes=VMEM_LIMIT_BYTES)


def _split3(x):
    hi = x.astype(BF16)
    r1 = x - hi.astype(F32)
    mid = r1.astype(BF16)
    lo = (r1 - mid.astype(F32)).astype(BF16)
    return hi, mid, lo


def _dg(a, b, dims):
    return lax.dot_general(a, b, (dims, ((), ())), preferred_element_type=F32)


_NN = ((1,), (0,))
_NT = ((1,), (1,))
_TN = ((0,), (0,))


def _dot1(a, b, dims=_NN):
    return _dg(a.astype(BF16), b.astype(BF16), dims)


def _dot3(a, b, dims=_NN):
    ah = a.astype(BF16)
    al = (a - ah.astype(F32)).astype(BF16)
    bh = b.astype(BF16)
    bl = (b - bh.astype(F32)).astype(BF16)
    return _dg(ah, bh, dims) + (_dg(ah, bl, dims) + _dg(al, bh, dims))


def _dot_exact_lhs(a_exact_bf16, b, dims=_NN):
    b0, b1, b2 = _split3(b)
    return _dg(a_exact_bf16, b0, dims) + (_dg(a_exact_bf16, b1, dims) + _dg(a_exact_bf16, b2, dims))


def _dot_exact_rhs(a, b_exact_bf16, dims=_NN):
    a0, a1, a2 = _split3(a)
    return _dg(a0, b_exact_bf16, dims) + (_dg(a1, b_exact_bf16, dims) + _dg(a2, b_exact_bf16, dims))


def _rms_rows(x, gain):
    ms = jnp.mean(x * x, axis=-1, keepdims=True)
    return x * lax.rsqrt(ms + EPS) * gain


def _sigmoid(x):
    return 1.0 / (1.0 + jnp.exp(-x))


def _silu(x):
    return x * _sigmoid(x)


def _gelu(x):
    return 0.5 * x * (1.0 + lax.erf(x * (1.0 / math.sqrt(2.0))))


def _softplus(x):
    return jnp.maximum(x, 0.0) + jnp.log1p(jnp.exp(-jnp.abs(x)))


def _norm_matmul_kernel(x_ref, g_ref, *refs, passes, emit_xn):
    if passes == 1:
        w_refs, rest = refs[:1], refs[1:]
    else:
        w_refs, rest = refs[:2], refs[2:]
    if emit_xn:
        o_ref, xn_out_ref = rest[0], rest[1]
        scr = rest[2:]
    else:
        o_ref = rest[0]
        scr = rest[1:]
    j = pl.program_id(1)

    @pl.when(j == 0)
    def _():
        xn = _rms_rows(x_ref[...], g_ref[...])
        if emit_xn:
            xn_out_ref[...] = xn
        hi = xn.astype(BF16)
        scr[0][...] = hi
        if passes == 3:
            scr[1][...] = (xn - hi.astype(F32)).astype(BF16)

    if passes == 1:
        o_ref[...] = _dg(scr[0][...], w_refs[0][...], _NN)
    else:
        xh = scr[0][...]
        wh = w_refs[0][...]
        o_ref[...] = _dg(xh, wh, _NN) + (_dg(xh, w_refs[1][...], _NN) + _dg(scr[1][...], wh, _NN))


def _norm_matmul(x, gain, w, *, tm, tn, passes=1, emit_xn=False, name):
    n, d = x.shape
    m = w.shape[1]
    assert n % tm == 0 and m % tn == 0
    wh = w.astype(BF16)
    ws = [wh] if passes == 1 else [wh, (w - wh.astype(F32)).astype(BF16)]
    w_specs = [pl.BlockSpec((d, tn), lambda i, j: (0, j)) for _ in ws]
    out_shape = [jax.ShapeDtypeStruct((n, m), F32)]
    out_specs = [pl.BlockSpec((tm, tn), lambda i, j: (i, j))]
    if emit_xn:
        out_shape.append(jax.ShapeDtypeStruct((n, d), F32))
        out_specs.append(pl.BlockSpec((tm, d), lambda i, j: (i, 0)))
    scratch = [pltpu.VMEM((tm, d), BF16) for _ in range(1 if passes == 1 else 2)]
    res = pl.pallas_call(
        functools.partial(_norm_matmul_kernel, passes=passes, emit_xn=emit_xn),
        grid=(n // tm, m // tn),
        in_specs=[pl.BlockSpec((tm, d), lambda i, j: (i, 0)),
                  pl.BlockSpec((1, d), lambda i, j: (0, 0))] + w_specs,
        out_specs=out_specs,
        out_shape=out_shape,
        scratch_shapes=scratch,
        compiler_params=_cparams("parallel", "arbitrary"),
        name=name,
    )(x, gain.reshape(1, d), *ws)
    return res if emit_xn else res[0]


def _norm_matmul_resident_kernel(x_ref, g_ref, w_ref, o_ref, *xn_out, emit_xn):
    xn = _rms_rows(x_ref[...], g_ref[...])
    if emit_xn:
        xn_out[0][...] = xn
    o_ref[...] = _dg(xn.astype(BF16), w_ref[...], _NN)


def _norm_matmul_resident(x, gain, w, *, tm, emit_xn=False, name):
    n, d = x.shape
    m = w.shape[1]
    assert n % tm == 0
    out_shape = [jax.ShapeDtypeStruct((n, m), F32)]
    out_specs = [pl.BlockSpec((tm, m), lambda i: (i, 0))]
    if emit_xn:
        out_shape.append(jax.ShapeDtypeStruct((n, d), F32))
        out_specs.append(pl.BlockSpec((tm, d), lambda i: (i, 0)))
    res = pl.pallas_call(
        functools.partial(_norm_matmul_resident_kernel, emit_xn=emit_xn),
        grid=(n // tm,),
        in_specs=[pl.BlockSpec((tm, d), lambda i: (i, 0)),
                  pl.BlockSpec((1, d), lambda i: (0, 0)),
                  pl.BlockSpec((d, m), lambda i: (0, 0), pipeline_mode=pl.Buffered(1))],
        out_specs=out_specs,
        out_shape=out_shape,
        compiler_params=_cparams("parallel"),
        name=name,
    )(x, gain.reshape(1, d), w.astype(BF16))
    return res if emit_xn else res[0]


def _conv_kernel(x_ref, buf_ref, w_ref, q_ref, k_ref, v_ref, xp_ref, *, tt):
    t = pl.program_id(1)
    halo = SUBLANES

    @pl.when(t == 0)
    def _():
        xp_ref[0:halo, :] = buf_ref[0]

    @pl.when(t > 0)
    def _():
        xp_ref[0:halo, :] = xp_ref[tt:tt + halo, :]

    xp_ref[halo:halo + tt, :] = x_ref[0]
    base = halo - (CONV_W - 1)
    y = xp_ref[base:base + tt, :] * w_ref[0:1, :]
    for j in range(1, CONV_W):
        y = y + xp_ref[base + j:base + j + tt, :] * w_ref[j:j + 1, :]
    y = _silu(y)
    qk_dim = GDN_HEADS * GDN_DK
    for h in range(GDN_HEADS):
        qh = y[:, h * GDN_DK:(h + 1) * GDN_DK]
        qn = qh * lax.rsqrt(jnp.sum(qh * qh, axis=-1, keepdims=True) + EPS)
        q_ref[0, :, h * GDN_DK:(h + 1) * GDN_DK] = qn * (GDN_DK ** -0.5)
        kh = y[:, qk_dim + h * GDN_DK:qk_dim + (h + 1) * GDN_DK]
        k_ref[0, :, h * GDN_DK:(h + 1) * GDN_DK] = kh * lax.rsqrt(jnp.sum(kh * kh, axis=-1, keepdims=True) + EPS)
    v_ref[0] = y[:, 2 * qk_dim:]


def _conv_qkv(proj3, buf8, conv_w, *, tt):
    b, t, _ = proj3.shape
    cdim = conv_w.shape[1]
    hd = GDN_HEADS * GDN_DK
    assert t % tt == 0
    shp = jax.ShapeDtypeStruct((b, t, hd), F32)
    ospec = pl.BlockSpec((1, tt, hd), lambda i, j: (i, j, 0))
    return pl.pallas_call(
        functools.partial(_conv_kernel, tt=tt),
        grid=(b, t // tt),
        in_specs=[pl.BlockSpec((1, tt, cdim), lambda i, j: (i, j, 0)),
                  pl.BlockSpec((1, SUBLANES, cdim), lambda i, j: (i, 0, 0)),
                  pl.BlockSpec((CONV_W, cdim), lambda i, j: (0, 0))],
        out_specs=[ospec, ospec, ospec],
        out_shape=[shp, shp, shp],
        scratch_shapes=[pltpu.VMEM((tt + 2 * SUBLANES, cdim), F32)],
        compiler_params=_cparams("parallel", "arbitrary"),
        name="conv_qkv",
    )(proj3, buf8, conv_w)


def _gdn_kernel(q_ref, k_ref, v_ref, ab_ref, abt_ref, prm_row_ref, prm_col_ref, s0_ref,
                o_ref, sout_ref, state_ref, *, tt, t_valid):
    tstep = pl.program_id(1)
    c = GDN_CHUNK

    @pl.when(tstep == 0)
    def _():
        state_ref[...] = s0_ref[0]

    ii = lax.broadcasted_iota(jnp.int32, (c, c), 0)
    jj = lax.broadcasted_iota(jnp.int32, (c, c), 1)
    incl = ii >= jj
    strict = ii > jj
    tri = incl.astype(BF16)
    tri_t = (jj >= ii).astype(BF16)
    eye = (ii == jj).astype(F32)

    alog_row = prm_row_ref[0:1, :]
    dtb_row = prm_row_ref[1:2, :]
    alog_col = prm_col_ref[:, 0:1]
    dtb_col = prm_col_ref[:, 1:2]

    def chunk_units(ci):
        r0 = pl.multiple_of(ci * c, c)
        tpos = tstep * tt + r0
        a_blk = ab_ref[0, pl.ds(r0, c), :]
        valid_col = (tpos + lax.broadcasted_iota(jnp.int32, (c, LANES), 0)) < t_valid
        g_col = jnp.where(valid_col, -jnp.exp(alog_row) * _softplus(a_blk + dtb_row), 0.0)
        beta_col = jnp.where(valid_col, _sigmoid(a_blk), 0.0)
        gc_col = _dot_exact_lhs(tri, g_col)
        at_blk = abt_ref[0, ci]
        valid_row = (tpos + lax.broadcasted_iota(jnp.int32, (2 * GDN_HEADS, c), 1)) < t_valid
        g_row = jnp.where(valid_row, -jnp.exp(alog_col) * _softplus(at_blk + dtb_col), 0.0)
        gc_row = _dot_exact_rhs(g_row, tri_t)
        units = []
        for h in range(GDN_HEADS):
            sl = slice(h * GDN_DK, (h + 1) * GDN_DK)
            qh = q_ref[0, pl.ds(r0, c), sl]
            kh = k_ref[0, pl.ds(r0, c), sl]
            vh = v_ref[0, pl.ds(r0, c), sl]
            gc = gc_col[:, h:h + 1]
            beta = beta_col[:, GDN_HEADS + h:GDN_HEADS + h + 1]
            gcr = gc_row[h:h + 1, :]
            gc_last = gc_col[c - 1:c, h:h + 1]
            decay = jnp.where(incl, jnp.exp(jnp.where(incl, gc - gcr, 0.0)), 0.0)
            egc = jnp.exp(gc)
            kb = kh * beta
            lmat = jnp.where(strict, _dot1(kb, kh, _NT) * decay, 0.0)
            units.append(dict(
                h=h, r0=r0, sl=sl, tinv=eye - lmat, pw=lmat,
                rhs=jnp.concatenate([vh * beta, kb * egc], axis=-1),
                attn=_dot1(qh, kh, _NT) * decay, q_dec=qh * egc,
                k_dec=kh * jnp.exp(gc_last - gc), bd=jnp.exp(gc_last)))
        return units

    def chunks_body(cp, carry):
        per_chunk = [chunk_units(cp * cpi + j) for j in range(cpi)]
        units = [u for us in per_chunk for u in us]
        for _ in range(5):
            for u in units:
                u["pw"] = _dot1(u["pw"], u["pw"])
            for u in units:
                u["tinv"] = u["tinv"] + _dot1(u["tinv"], u["pw"])
        for u in units:
            u["sol"] = _dot3(u["tinv"], u["rhs"])
        for us in per_chunk:
            for u in us:
                s = state_ref[u["h"]]
                u["s"] = s
                u["v_new"] = u["sol"][:, :GDN_DV] - _dot1(u["sol"][:, GDN_DV:], s)
            for u in us:
                o_ref[0, pl.ds(u["r0"], c), u["sl"]] = _dot1(u["q_dec"], u["s"]) + _dot1(u["attn"], u["v_new"])
                state_ref[u["h"]] = u["s"] * u["bd"] + _dot1(u["k_dec"], u["v_new"], _TN)
        return carry

    n_chunks = tt // c
    cpi = GDN_CHUNKS_PER_ITER if n_chunks % GDN_CHUNKS_PER_ITER == 0 else 1
    lax.fori_loop(0, n_chunks // cpi, chunks_body, 0)

    @pl.when(tstep == pl.num_programs(1) - 1)
    def _():
        sout_ref[0] = state_ref[...]


def _gdn(q, k, v, ab, abt, a_log, dt_bias, s0, *, tt, t_valid):
    b, t, hd = q.shape
    assert t % tt == 0 and tt % GDN_CHUNK == 0
    prm_row = jnp.zeros((SUBLANES, LANES), F32)
    prm_row = prm_row.at[0, :GDN_HEADS].set(a_log).at[1, :GDN_HEADS].set(dt_bias)
    prm_col = jnp.zeros((2 * GDN_HEADS, LANES), F32)
    prm_col = prm_col.at[:GDN_HEADS, 0].set(a_log).at[:GDN_HEADS, 1].set(dt_bias)
    seq = pl.BlockSpec((1, tt, hd), lambda i, j: (i, j, 0))
    st = pl.BlockSpec((1, GDN_HEADS, GDN_DK, GDN_DV), lambda i, j: (i, 0, 0, 0))
    return pl.pallas_call(
        functools.partial(_gdn_kernel, tt=tt, t_valid=t_valid),
        grid=(b, t // tt),
        in_specs=[seq, seq, seq,
                  pl.BlockSpec((1, tt, LANES), lambda i, j: (i, j, 0)),
                  pl.BlockSpec((1, tt // GDN_CHUNK, 2 * GDN_HEADS, GDN_CHUNK), lambda i, j: (i, j, 0, 0)),
                  pl.BlockSpec((SUBLANES, LANES), lambda i, j: (0, 0)),
                  pl.BlockSpec((2 * GDN_HEADS, LANES), lambda i, j: (0, 0)),
                  st],
        out_specs=[seq, st],
        out_shape=[jax.ShapeDtypeStruct((b, t, hd), F32),
                   jax.ShapeDtypeStruct((b, GDN_HEADS, GDN_DK, GDN_DV), F32)],
        scratch_shapes=[pltpu.VMEM((GDN_HEADS, GDN_DK, GDN_DV), F32)],
        compiler_params=_cparams("parallel", "arbitrary"),
        name="gdn_core",
    )(q, k, v, ab, abt, prm_row, prm_col, s0)


def _mlp_kernel(uv_ref, lng_ref, lnb_ref, ws_ref, bst_ref, y_ref, vn_ref, *, rows, lc):
    mdim = MLP_GROUPS * MLP_GROUP_DIM
    ii = lax.broadcasted_iota(jnp.int32, (lc, lc), 0)
    jj = lax.broadcasted_iota(jnp.int32, (lc, lc), 1)
    keep = ii >= jj
    for r in range(rows // lc):
        rs = slice(r * lc, (r + 1) * lc)
        g = _gelu(uv_ref[0, rs, :])
        u = g[:, :mdim]
        vv = g[:, mdim:]
        mu = jnp.mean(vv, axis=-1, keepdims=True)
        xc = vv - mu
        vn = xc * lax.rsqrt(jnp.mean(xc * xc, axis=-1, keepdims=True) + EPS) * lng_ref[...] + lnb_ref[...]
        vn_ref[0, rs, :] = vn
        for gi in range(MLP_GROUPS):
            gs = slice(gi * MLP_GROUP_DIM, (gi + 1) * MLP_GROUP_DIM)
            wm = jnp.where(keep, ws_ref[gi], 0.0)
            s = _dot1(wm, vn[:, gs]) + bst_ref[:, gi:gi + 1]
            y_ref[0, rs, gs] = u[:, gs] * s


def _mlp_branch(proj3, col_block, ln_g, ln_b, w_s, b_s, *, rows, lc):
    b, t, _ = proj3.shape
    mdim = MLP_GROUPS * MLP_GROUP_DIM
    assert t % rows == 0 and rows % lc == 0
    ws = w_s[:, :lc, :lc]
    bst = jnp.transpose(b_s[:, :lc])
    shp = jax.ShapeDtypeStruct((b, t, mdim), F32)
    ospec = pl.BlockSpec((1, rows, mdim), lambda i, j: (i, j, 0))
    return pl.pallas_call(
        functools.partial(_mlp_kernel, rows=rows, lc=lc),
        grid=(b, t // rows),
        in_specs=[pl.BlockSpec((1, rows, 2 * mdim), lambda i, j: (i, j, col_block)),
                  pl.BlockSpec((1, mdim), lambda i, j: (0, 0)),
                  pl.BlockSpec((1, mdim), lambda i, j: (0, 0)),
                  pl.BlockSpec((MLP_GROUPS, lc, lc), lambda i, j: (0, 0, 0)),
                  pl.BlockSpec((lc, MLP_GROUPS), lambda i, j: (0, 0))],
        out_specs=[ospec, ospec],
        out_shape=[shp, shp],
        compiler_params=_cparams("parallel", "parallel"),
        name="mlp_branch",
    )(proj3, ln_g.reshape(1, mdim), ln_b.reshape(1, mdim), ws, bst)


def _outproj_kernel(o_ref, z_ref, y_ref, h_ref, gn_ref, w_ref, out_ref):
    parts = []
    for hh in range(GDN_HEADS):
        sl = slice(hh * GDN_DV, (hh + 1) * GDN_DV)
        parts.append((_rms_rows(o_ref[:, sl], gn_ref[...]) * _silu(z_ref[:, sl])).astype(BF16))
    parts.append(y_ref[...].astype(BF16))
    cat = jnp.concatenate(parts, axis=-1)
    out_ref[...] = h_ref[...] + _dg(cat, w_ref[...], _NN)


def _outproj(o2, proj2, z_block, y2, h2, gdn_norm, w_out, *, tm):
    n, d = h2.shape
    vd = GDN_HEADS * GDN_DV
    assert n % tm == 0
    return pl.pallas_call(
        _outproj_kernel,
        grid=(n // tm,),
        in_specs=[pl.BlockSpec((tm, vd), lambda i: (i, 0)),
                  pl.BlockSpec((tm, vd), lambda i: (i, z_block)),
                  pl.BlockSpec((tm, vd), lambda i: (i, 0)),
                  pl.BlockSpec((tm, d), lambda i: (i, 0)),
                  pl.BlockSpec((1, GDN_DV), lambda i: (0, 0)),
                  pl.BlockSpec(w_out.shape, lambda i: (0, 0), pipeline_mode=pl.Buffered(1))],
        out_specs=pl.BlockSpec((tm, d), lambda i: (i, 0)),
        out_shape=jax.ShapeDtypeStruct((n, d), F32),
        compiler_params=_cparams("parallel"),
        name="out_proj",
    )(o2, proj2, y2, h2, gdn_norm.reshape(1, GDN_DV), w_out.astype(BF16))


def _topk_axis0(x, k, payloads=()):
    r = x.shape[0]
    iota = lax.broadcasted_iota(jnp.int32, x.shape, 0)
    vals, idxs = [], []
    outs = [[] for _ in payloads]
    for _ in range(k):
        m = jnp.max(x, axis=0, keepdims=True)
        i = jnp.min(jnp.where(x == m, iota, r), axis=0, keepdims=True)
        hit = iota == i
        vals.append(m)
        idxs.append(i)
        for p, acc in zip(payloads, outs):
            acc.append(jnp.sum(jnp.where(hit, p, 0), axis=0, keepdims=True))
        x = jnp.where(hit, -jnp.inf, x)
    cat = lambda parts: jnp.concatenate(parts, axis=0)
    return cat(vals), cat(idxs), [cat(acc) for acc in outs]


def _peer_topk_kernel(q_ref, keys_ref, eidx_ref, gate_ref):
    k = PEER_TOPK

    def one_head(h):
        sv, si = [], []
        for s in range(2):
            c0 = pl.multiple_of(h * (2 * PEER_QHALF) + s * PEER_QHALF, PEER_QHALF)
            qhs = q_ref[:, pl.ds(c0, PEER_QHALF)]
            sc_t = _dot3(keys_ref[s, h], qhs, _NT)
            v, i, _ = _topk_axis0(sc_t, k)
            sv.append(v)
            si.append(i)
        half = k // 2
        sub = lax.broadcasted_iota(jnp.int32, (half, sv[0].shape[1]), 0)
        cand_parts = [sv[0][0:1, :] + sv[1]]
        cidx_parts = [si[0][0:1, :] * N_KEYS + si[1]]
        for a in range(1, half):
            keep = sub < (k // (a + 1))
            cand_parts.append(jnp.where(keep, sv[0][a:a + 1, :] + sv[1][0:half, :], -jnp.inf))
            cidx_parts.append(si[0][a:a + 1, :] * N_KEYS + si[1][0:half, :])
        cand_parts.append(sv[0][half:, :] + sv[1][0:1, :])
        cidx_parts.append(si[0][half:, :] * N_KEYS + si[1][0:1, :])
        cand = jnp.concatenate(cand_parts, axis=0)
        cidx = jnp.concatenate(cidx_parts, axis=0)
        fv, _, (fe,) = _topk_axis0(cand, k, (cidx,))
        e = jnp.exp(fv - fv[0:1, :])
        gate = e / jnp.sum(e, axis=0, keepdims=True)
        r0 = pl.multiple_of(h * k, k)
        eidx_ref[pl.ds(r0, k), :] = fe
        gate_ref[pl.ds(r0, k), :] = gate

    def heads_body(hp, carry):
        for j in range(PEER_TOPK_HEADS_PER_ITER):
            one_head(hp * PEER_TOPK_HEADS_PER_ITER + j)
        return carry

    lax.fori_loop(0, PEER_HEADS // PEER_TOPK_HEADS_PER_ITER, heads_body, 0)


def _peer_topk(q, sub_keys, *, tk):
    n, qd = q.shape
    assert n % tk == 0
    rows = PEER_HEADS * PEER_TOPK
    return pl.pallas_call(
        _peer_topk_kernel,
        grid=(n // tk,),
        in_specs=[pl.BlockSpec((tk, qd), lambda i: (i, 0)),
                  pl.BlockSpec(sub_keys.shape, lambda i: (0, 0, 0, 0))],
        out_specs=[pl.BlockSpec((rows, tk), lambda i: (0, i)),
                   pl.BlockSpec((rows, tk), lambda i: (0, i))],
        out_shape=[jax.ShapeDtypeStruct((rows, n), jnp.int32),
                   jax.ShapeDtypeStruct((rows, n), F32)],
        compiler_params=_cparams("parallel"),
        name="peer_topk",
    )(q, sub_keys)


PEER_ROWS = PEER_HEADS * PEER_TOPK
PEER_SLOTS = 16
PEER_LOOK = PEER_SLOTS - 1
assert PEER_SLOTS % 2 == 0 and PEER_LOOK % 2 == 1
PEER_IDX_TAIL = 16
assert PEER_LOOK <= PEER_IDX_TAIL and (PEER_IDX_TAIL * PEER_ROWS) % 1024 == 0
PEER_ACCS = 4
PEER_DOT_COPIES = 5
PEER_MIX_COPIES = 3
assert (PEER_DOT_COPIES + PEER_MIX_COPIES) * (PEER_ROWS // SUBLANES) == PEER_ROWS


def _sublane_fold(parts):
    sub = lax.broadcasted_iota(jnp.int32, parts[0].shape, 0)
    for dist in (4, 2, 1):
        low = (sub & dist) == 0
        nxt = []
        for a in range(len(parts) // 2):
            lo_src, hi_src = parts[a], parts[a + len(parts) // 2]
            kept = jnp.where(low, lo_src, hi_src)
            if 2 * dist == SUBLANES:
                moved = pltpu.roll(jnp.where(low, hi_src, lo_src), dist, axis=0)
            else:
                moved = jnp.where(low, pltpu.roll(lo_src, SUBLANES - dist, axis=0), pltpu.roll(hi_src, dist, axis=0))
            nxt.append(kept + moved)
        parts = nxt
    return parts[0]


def _peer_gather_kernel(idx_ref, x_ref, gate_ref, h_ref, tab_ref, out_ref, buf_even, buf_odd, sem_ref, coef_ref, *, tb):
    rows = PEER_ROWS
    nc = x_ref.shape[1] // LANES
    half = nc // 2
    groups = rows // SUBLANES
    step = pl.program_id(0)
    hi_mask = jnp.uint32(0xFFFF0000)
    bufs = (buf_even, buf_odd)

    def slot_of(tok):
        return (tok // 2) % (PEER_SLOTS // 2)

    def start_rows(arr, tok, r0, count):
        for j in range(count):
            r = r0 + j
            pltpu.make_async_copy(tab_ref.at[idx_ref[tok * rows + r]], arr.at[slot_of(tok), r],
                                  sem_ref.at[tok % PEER_SLOTS]).start(priority=j % 2)

    def wait_token(arr, tok):
        pltpu.make_async_copy(tab_ref.at[pl.ds(0, rows)], arr.at[slot_of(tok)], sem_ref.at[tok % PEER_SLOTS]).wait()

    def load_x(t):
        xrow = x_ref[pl.ds(t, 1), :]
        x_lo = jnp.concatenate([xrow[:, c * LANES:(c + 1) * LANES] for c in range(half)], axis=0)
        x_hi = jnp.concatenate([xrow[:, c * LANES:(c + 1) * LANES] for c in range(half, nc)], axis=0)
        return x_lo, x_hi

    def dot_group(buf, x, g):
        parts = []
        for s in range(SUBLANES):
            r = g * SUBLANES + s
            u_lo = lax.bitcast_convert_type(buf[r, 0:half, :] << 16, F32)
            u_hi = lax.bitcast_convert_type(buf[r, half:nc, :] << 16, F32)
            parts.append(u_lo * x[0] + u_hi * x[1])
        return _sublane_fold(parts)

    def finish_dot(t, folded):
        act = jnp.sum(jnp.concatenate(folded, axis=0), axis=-1, keepdims=True)
        lane_tok = lax.broadcasted_iota(jnp.int32, (rows, tb), 1)
        gcol = jnp.sum(jnp.where(lane_tok == t, gate_ref[0], 0.0), axis=-1, keepdims=True)
        coef_ref[...] = jnp.broadcast_to(gcol * _gelu(act), (rows, LANES))

    def mix_group(buf, g, acc_lo, acc_hi):
        for s in range(SUBLANES):
            r = g * SUBLANES + s
            c = jnp.broadcast_to(coef_ref[r:r + 1, :], (half, LANES))
            a = r % PEER_ACCS
            acc_lo[a] = acc_lo[a] + lax.bitcast_convert_type(buf[r, 0:half, :] & hi_mask, F32) * c
            acc_hi[a] = acc_hi[a] + lax.bitcast_convert_type(buf[r, half:nc, :] & hi_mask, F32) * c

    def emit(t, acc_lo, acc_hi):
        mix_lo = (acc_lo[0] + acc_lo[1]) + (acc_lo[2] + acc_lo[3])
        mix_hi = (acc_hi[0] + acc_hi[1]) + (acc_hi[2] + acc_hi[3])
        mix_row = jnp.concatenate([mix_lo[c:c + 1, :] for c in range(half)]
                                  + [mix_hi[c:c + 1, :] for c in range(half)], axis=1)
        out_ref[pl.ds(t, 1), :] = h_ref[pl.ds(t, 1), :] + mix_row

    def token(t, parity):
        arr, arr_next = bufs[parity], bufs[(parity + PEER_LOOK) % 2]
        tok_next = t + PEER_LOOK
        wait_token(arr, t)
        buf = arr.at[slot_of(t)]
        x = load_x(t)
        folded = []
        for g in range(groups):
            start_rows(arr_next, tok_next, g * PEER_DOT_COPIES, PEER_DOT_COPIES)
            folded.append(dot_group(buf, x, g))
        finish_dot(t, folded)
        acc_lo = [jnp.zeros((half, LANES), F32) for _ in range(PEER_ACCS)]
        acc_hi = [jnp.zeros((half, LANES), F32) for _ in range(PEER_ACCS)]
        for g in range(groups):
            start_rows(arr_next, tok_next, groups * PEER_DOT_COPIES + g * PEER_MIX_COPIES, PEER_MIX_COPIES)
            mix_group(buf, g, acc_lo, acc_hi)
        emit(t, acc_lo, acc_hi)

    @pl.when(step == 0)
    def _():
        for s in range(PEER_LOOK):
            def prime(i, carry, s=s):
                start_rows(bufs[s % 2], s, i * SUBLANES, SUBLANES)
                return carry
            lax.fori_loop(0, groups, prime, 0)

    def pair(p, carry):
        token(2 * p, 0)
        token(2 * p + 1, 1)
        return carry

    lax.fori_loop(0, tb // 2, pair, 0)

    @pl.when(step == pl.num_programs(0) - 1)
    def _():
        for s in range(PEER_LOOK):
            wait_token(bufs[(tb + s) % 2], tb + s)


def _peer_gather(eidx, xn, gate_blocks, h, table, *, tb):
    n, d = h.shape
    rows = PEER_ROWS
    assert n % tb == 0 and tb % PEER_SLOTS == 0 and PEER_ACCS == 4
    nb = n // tb
    idx2 = eidx.reshape(nb, tb * rows)
    idx_ext = jnp.concatenate([idx2, jnp.roll(idx2[:, :PEER_IDX_TAIL * rows], -1, axis=0)], axis=1).reshape(-1)
    return pl.pallas_call(
        functools.partial(_peer_gather_kernel, tb=tb),
        grid=(nb,),
        in_specs=[pl.BlockSpec(((tb + PEER_IDX_TAIL) * rows,), lambda i: (i,), memory_space=pltpu.SMEM),
                  pl.BlockSpec((tb, d), lambda i: (i, 0)),
                  pl.BlockSpec((1, rows, tb), lambda i: (i, 0, 0)),
                  pl.BlockSpec((tb, d), lambda i: (i, 0)),
                  pl.BlockSpec(memory_space=pl.ANY)],
        out_specs=pl.BlockSpec((tb, d), lambda i: (i, 0)),
        out_shape=jax.ShapeDtypeStruct((n, d), F32),
        scratch_shapes=[pltpu.VMEM((PEER_SLOTS // 2, rows, d // LANES, LANES), jnp.uint32),
                        pltpu.VMEM((PEER_SLOTS // 2, rows, d // LANES, LANES), jnp.uint32),
                        pltpu.SemaphoreType.DMA((PEER_SLOTS,)),
                        pltpu.VMEM((rows, LANES), F32)],
        compiler_params=_cparams("arbitrary"),
        name="peer_gather",
    )(idx_ext, xn, gate_blocks, h, table)


def _ple_kernel(h_ref, p_ref, n3_ref, fn_ref, wg_ref, wp_ref, y_ref):
    h = h_ref[...]
    gate = _sigmoid(_dot1(_rms_rows(h, n3_ref[...]), wg_ref[...]))
    h = h + gate * _dot1(p_ref[...], wp_ref[...])
    y_ref[...] = _rms_rows(h, fn_ref[...])


def _ple_final(h, p, norm3, final_norm, w_gate, w_proj, *, tm):
    n, d = h.shape
    pd = p.shape[1]
    assert n % tm == 0
    return pl.pallas_call(
        _ple_kernel,
        grid=(n // tm,),
        in_specs=[pl.BlockSpec((tm, d), lambda i: (i, 0)),
                  pl.BlockSpec((tm, pd), lambda i: (i, 0)),
                  pl.BlockSpec((1, d), lambda i: (0, 0)),
                  pl.BlockSpec((1, d), lambda i: (0, 0)),
                  pl.BlockSpec((d, d), lambda i: (0, 0), pipeline_mode=pl.Buffered(1)),
                  pl.BlockSpec((pd, d), lambda i: (0, 0), pipeline_mode=pl.Buffered(1))],
        out_specs=pl.BlockSpec((tm, d), lambda i: (i, 0)),
        out_shape=jax.ShapeDtypeStruct((n, d), F32),
        compiler_params=_cparams("parallel"),
        name="ple_final",
    )(h, p, norm3.reshape(1, d), final_norm.reshape(1, d), w_gate.astype(BF16), w_proj.astype(BF16))


QK_DIM = GDN_HEADS * GDN_DK
V_DIM = GDN_HEADS * GDN_DV
CONV_DIM = 2 * QK_DIM + V_DIM
MLP_DIM = MLP_GROUPS * MLP_GROUP_DIM
Z_COL = CONV_DIM
UV_COL = Z_COL + V_DIM
AB_COL = UV_COL + 2 * MLP_DIM
PROJ_COLS = AB_COL + LANES
PROJ_TN = 896


def _relayout_w_in(w_in):
    c0 = CONV_DIM
    c2 = c0 + 2 * GDN_HEADS
    c3 = c2 + V_DIM
    ab = jnp.pad(w_in[:, c0:c2], ((0, 0), (0, LANES - 2 * GDN_HEADS)))
    return jnp.concatenate([w_in[:, :c0], w_in[:, c2:c3], w_in[:, c3:], ab], axis=1)


def _pack_tables(peer_u, peer_v):
    ub = lax.bitcast_convert_type(peer_u.astype(BF16), jnp.uint16).astype(jnp.uint32)
    vb = lax.bitcast_convert_type(peer_v.astype(BF16), jnp.uint16).astype(jnp.uint32)
    e, d = peer_u.shape
    return (ub | (vb << 16)).reshape(e, d // LANES, LANES)


def _row_tile(n, want):
    t = min(n, want)
    assert n % t == 0
    return t


def _layer(x, p, s0, buf, w, final_norm):
    b, t, d = x.shape
    n = b * t
    x2 = x.reshape(n, d)
    tm = _row_tile(n, 1024)

    proj = _norm_matmul_resident(x2, w["norm1"], w["w_in"], tm=_row_tile(n, 256), name="in_proj")
    proj3 = proj.reshape(b, t, PROJ_COLS)
    new_buf = jnp.concatenate([buf, proj3[:, :, :CONV_DIM]], axis=1)[:, t:]

    buf8 = jnp.pad(buf, ((0, 0), (SUBLANES - (CONV_W - 1), 0), (0, 0)))
    q, k, v = _conv_qkv(proj3, buf8, w["conv_w"], tt=_row_tile(t, 512))

    tg = -(-t // GDN_CHUNK) * GDN_CHUNK
    ab3 = proj3[:, :, AB_COL:]
    if tg != t:
        padt = ((0, 0), (0, tg - t), (0, 0))
        q, k, v, ab3 = (jnp.pad(a, padt) for a in (q, k, v, ab3))
    abt = jnp.transpose(ab3[:, :, :2 * GDN_HEADS].reshape(b, tg // GDN_CHUNK, GDN_CHUNK, 2 * GDN_HEADS),
                        (0, 1, 3, 2))
    o, s_new = _gdn(q, k, v, ab3, abt, w["a_log"], w["dt_bias"], s0, tt=_row_tile(tg, 512), t_valid=t)
    o = o[:, :t]

    lc = min(t, MLP_CHUNK)
    y_b, vn = _mlp_branch(proj3, UV_COL // (2 * MLP_DIM), w["ln_g"], w["ln_b"], w["w_s"], w["b_s"],
                          rows=_row_tile(t, 512), lc=lc)

    h1 = _outproj(o.reshape(n, V_DIM), proj, Z_COL // V_DIM, y_b.reshape(n, MLP_DIM), x2,
                  w["gdn_norm"], w["w_out"], tm=_row_tile(n, 512))

    qp, xn2 = _norm_matmul_resident(h1, w["norm2"], w["w_q"], tm=_row_tile(n, 512), emit_xn=True, name="peer_query")
    tb = 128
    eidx_t, gate_t = _peer_topk(qp, w["sub_keys"], tk=_row_tile(n, 512))
    rows = PEER_HEADS * PEER_TOPK
    gate_blocks = jnp.transpose(gate_t.reshape(rows, n // tb, tb), (1, 0, 2))
    h2 = _peer_gather(jnp.transpose(eidx_t), xn2, gate_blocks, h1, w["table"], tb=tb)

    y = _ple_final(h2, p.reshape(n, -1), w["norm3"], final_norm, w["w_ple_gate"], w["w_ple_proj"],
                   tm=_row_tile(n, 512))
    return y.reshape(b, t, d), s_new, new_buf, vn


def kernel(x_prompt, x_sample, state_gdn, cache_conv, p_prompt, p_sample, norm1, w_in, conv_w, a_log, dt_bias,
           gdn_norm, ln_g, ln_b, w_s, b_s, w_out, norm2, w_q, sub_keys, peer_u, peer_v, norm3, w_ple_gate,
           w_ple_proj, final_norm):
    assert norm1.shape[0] == 1, "single layer"
    w = dict(norm1=norm1[0], w_in=_relayout_w_in(w_in[0]), conv_w=conv_w[0], a_log=a_log[0], dt_bias=dt_bias[0],
             gdn_norm=gdn_norm[0], ln_g=ln_g[0], ln_b=ln_b[0], w_s=w_s[0], b_s=b_s[0], w_out=w_out[0],
             norm2=norm2[0], w_q=w_q[0], sub_keys=sub_keys[0], table=_pack_tables(peer_u[0], peer_v[0]),
             norm3=norm3[0], w_ple_gate=w_ple_gate[0], w_ple_proj=w_ple_proj[0])
    bp = x_prompt.shape[0]
    s0p = jnp.zeros((bp, GDN_HEADS, GDN_DK, GDN_DV), F32)
    b0p = jnp.zeros((bp, CONV_W - 1, CONV_DIM), F32)
    y_s, s_s, b_s_new, v_s = _layer(x_sample, p_sample[0], state_gdn[0], cache_conv[0], w, final_norm)
    y_p, s_p, b_p, _ = _layer(x_prompt, p_prompt[0], s0p, b0p, w, final_norm)
    return (y_p, y_s, s_p[None], b_p[None], s_s[None], b_s_new[None], v_s[None])
```

```python
import functools
import math

import jax
import jax.numpy as jnp
from jax import lax
from jax.experimental import pallas as pl
from jax.experimental.pallas import tpu as pltpu

F32 = jnp.float32
BF16 = jnp.bfloat16

EPS = 1e-6
GDN_CHUNK = 64
GDN_HEADS = 8
GDN_CHUNKS_PER_ITER = 4
GDN_DK = 128
GDN_DV = 128
CONV_W = 4
MLP_CHUNK = 128
MLP_GROUPS = 8
MLP_GROUP_DIM = 128
PEER_HEADS = 8
PEER_QHALF = 128
N_KEYS = 128
PEER_TOPK = 16
PEER_TOPK_HEADS_PER_ITER = 8

LANES = 128
SUBLANES = 8
V7X_VMEM_BYTES = 64 * 1024 * 1024
VMEM_LIMIT_BYTES = V7X_VMEM_BYTES * 7 // 8


def _cparams(*sem):
    return pltpu.CompilerParams(dimension_semantics=sem, vmem_limit_bytes=VMEM_LIMIT_BYTES)


def _split3(x):
    hi = x.astype(BF16)
    r1 = x - hi.astype(F32)
    mid = r1.astype(BF16)
    lo = (r1 - mid.astype(F32)).astype(BF16)
    return hi, mid, lo


def _dg(a, b, dims):
    return lax.dot_general(a, b, (dims, ((), ())), preferred_element_type=F32)


_NN = ((1,), (0,))
_NT = ((1,), (1,))
_TN = ((0,), (0,))


def _dot1(a, b, dims=_NN):
    return _dg(a.astype(BF16), b.astype(BF16), dims)


def _dot3(a, b, dims=_NN):
    ah = a.astype(BF16)
    al = (a - ah.astype(F32)).astype(BF16)
    bh = b.astype(BF16)
    bl = (b - bh.astype(F32)).astype(BF16)
    return _dg(ah, bh, dims) + (_dg(ah, bl, dims) + _dg(al, bh, dims))


def _dot_exact_lhs(a_exact_bf16, b, dims=_NN):
    b0, b1, b2 = _split3(b)
    return _dg(a_exact_bf16, b0, dims) + (_dg(a_exact_bf16, b1, dims) + _dg(a_exact_bf16, b2, dims))


def _dot_exact_rhs(a, b_exact_bf16, dims=_NN):
    a0, a1, a2 = _split3(a)
    return _dg(a0, b_exact_bf16, dims) + (_dg(a1, b_exact_bf16, dims) + _dg(a2, b_exact_bf16, dims))


def _rms_rows(x, gain):
    ms = jnp.mean(x * x, axis=-1, keepdims=True)
    return x * lax.rsqrt(ms + EPS) * gain


def _sigmoid(x):
    return 1.0 / (1.0 + jnp.exp(-x))


def _silu(x):
    return x * _sigmoid(x)


def _gelu(x):
    return 0.5 * x * (1.0 + lax.erf(x * (1.0 / math.sqrt(2.0))))


def _softplus(x):
    return jnp.maximum(x, 0.0) + jnp.log1p(jnp.exp(-jnp.abs(x)))


def _norm_matmul_resident_kernel(x_ref, g_ref, w_ref, o_ref, *xn_out, emit_xn):
    xn = _rms_rows(x_ref[...], g_ref[...])
    if emit_xn:
        xn_out[0][...] = xn
    o_ref[...] = _dg(xn.astype(BF16), w_ref[...], _NN)


def _norm_matmul_resident(x, gain, w, *, tm, emit_xn=False, name):
    n, d = x.shape
    m = w.shape[1]
    assert n % tm == 0
    out_shape = [jax.ShapeDtypeStruct((n, m), F32)]
    out_specs = [pl.BlockSpec((tm, m), lambda i: (i, 0))]
    if emit_xn:
        out_shape.append(jax.ShapeDtypeStruct((n, d), F32))
        out_specs.append(pl.BlockSpec((tm, d), lambda i: (i, 0)))
    res = pl.pallas_call(
        functools.partial(_norm_matmul_resident_kernel, emit_xn=emit_xn),
        grid=(n // tm,),
        in_specs=[pl.BlockSpec((tm, d), lambda i: (i, 0)),
                  pl.BlockSpec((1, d), lambda i: (0, 0)),
                  pl.BlockSpec((d, m), lambda i: (0, 0), pipeline_mode=pl.Buffered(1))],
        out_specs=out_specs,
        out_shape=out_shape,
        compiler_params=_cparams("parallel"),
        name=name,
    )(x, gain.reshape(1, d), w.astype(BF16))
    return res if emit_xn else res[0]


def _conv_kernel(x_ref, buf_ref, w_ref, q_ref, k_ref, v_ref, xp_ref, *, tt):
    t = pl.program_id(1)
    halo = SUBLANES

    @pl.when(t == 0)
    def _():
        xp_ref[0:halo, :] = buf_ref[0]

    @pl.when(t > 0)
    def _():
        xp_ref[0:halo, :] = xp_ref[tt:tt + halo, :]

    xp_ref[halo:halo + tt, :] = x_ref[0]
    base = halo - (CONV_W - 1)
    y = xp_ref[base:base + tt, :] * w_ref[0:1, :]
    for j in range(1, CONV_W):
        y = y + xp_ref[base + j:base + j + tt, :] * w_ref[j:j + 1, :]
    y = _silu(y)
    qk_dim = GDN_HEADS * GDN_DK
    for h in range(GDN_HEADS):
        qh = y[:, h * GDN_DK:(h + 1) * GDN_DK]
        qn = qh * lax.rsqrt(jnp.sum(qh * qh, axis=-1, keepdims=True) + EPS)
        q_ref[0, :, h * GDN_DK:(h + 1) * GDN_DK] = qn * (GDN_DK ** -0.5)
        kh = y[:, qk_dim + h * GDN_DK:qk_dim + (h + 1) * GDN_DK]
        k_ref[0, :, h * GDN_DK:(h + 1) * GDN_DK] = kh * lax.rsqrt(jnp.sum(kh * kh, axis=-1, keepdims=True) + EPS)
    v_ref[0] = y[:, 2 * qk_dim:]


def _conv_qkv(proj3, buf8, conv_w, *, tt):
    b, t, _ = proj3.shape
    cdim = conv_w.shape[1]
    hd = GDN_HEADS * GDN_DK
    assert t % tt == 0
    shp = jax.ShapeDtypeStruct((b, t, hd), F32)
    ospec = pl.BlockSpec((1, tt, hd), lambda i, j: (i, j, 0))
    return pl.pallas_call(
        functools.partial(_conv_kernel, tt=tt),
        grid=(b, t // tt),
        in_specs=[pl.BlockSpec((1, tt, cdim), lambda i, j: (i, j, 0)),
                  pl.BlockSpec((1, SUBLANES, cdim), lambda i, j: (i, 0, 0)),
                  pl.BlockSpec((CONV_W, cdim), lambda i, j: (0, 0))],
        out_specs=[ospec, ospec, ospec],
        out_shape=[shp, shp, shp],
        scratch_shapes=[pltpu.VMEM((tt + 2 * SUBLANES, cdim), F32)],
        compiler_params=_cparams("parallel", "arbitrary"),
        name="conv_qkv",
    )(proj3, buf8, conv_w)


def _gdn_kernel(q_ref, k_ref, v_ref, ab_ref, abt_ref, prm_row_ref, prm_col_ref, s0_ref,
                o_ref, sout_ref, state_ref, *, tt, t_valid):
    tstep = pl.program_id(1)
    c = GDN_CHUNK

    @pl.when(tstep == 0)
    def _():
        state_ref[...] = s0_ref[0]

    ii = lax.broadcasted_iota(jnp.int32, (c, c), 0)
    jj = lax.broadcasted_iota(jnp.int32, (c, c), 1)
    incl = ii >= jj
    strict = ii > jj
    tri = incl.astype(BF16)
    tri_t = (jj >= ii).astype(BF16)
    eye = (ii == jj).astype(F32)

    alog_row = prm_row_ref[0:1, :]
    dtb_row = prm_row_ref[1:2, :]
    alog_col = prm_col_ref[:, 0:1]
    dtb_col = prm_col_ref[:, 1:2]

    def chunk_units(ci):
        r0 = pl.multiple_of(ci * c, c)
        tpos = tstep * tt + r0
        a_blk = ab_ref[0, pl.ds(r0, c), :]
        valid_col = (tpos + lax.broadcasted_iota(jnp.int32, (c, LANES), 0)) < t_valid
        g_col = jnp.where(valid_col, -jnp.exp(alog_row) * _softplus(a_blk + dtb_row), 0.0)
        beta_col = jnp.where(valid_col, _sigmoid(a_blk), 0.0)
        gc_col = _dot_exact_lhs(tri, g_col)
        at_blk = abt_ref[0, ci]
        valid_row = (tpos + lax.broadcasted_iota(jnp.int32, (2 * GDN_HEADS, c), 1)) < t_valid
        g_row = jnp.where(valid_row, -jnp.exp(alog_col) * _softplus(at_blk + dtb_col), 0.0)
        gc_row = _dot_exact_rhs(g_row, tri_t)
        units = []
        for h in range(GDN_HEADS):
            sl = slice(h * GDN_DK, (h + 1) * GDN_DK)
            qh = q_ref[0, pl.ds(r0, c), sl]
            kh = k_ref[0, pl.ds(r0, c), sl]
            vh = v_ref[0, pl.ds(r0, c), sl]
            gc = gc_col[:, h:h + 1]
            beta = beta_col[:, GDN_HEADS + h:GDN_HEADS + h + 1]
            gcr = gc_row[h:h + 1, :]
            gc_last = gc_col[c - 1:c, h:h + 1]
            decay = jnp.where(incl, jnp.exp(jnp.where(incl, gc - gcr, 0.0)), 0.0)
            egc = jnp.exp(gc)
            kb = kh * beta
            lmat = jnp.where(strict, _dot1(kb, kh, _NT) * decay, 0.0)
            units.append(dict(
                h=h, r0=r0, sl=sl, tinv=eye - lmat, pw=lmat,
                rhs=jnp.concatenate([vh * beta, kb * egc], axis=-1),
                attn=_dot1(qh, kh, _NT) * decay, q_dec=qh * egc,
                k_dec=kh * jnp.exp(gc_last - gc), bd=jnp.exp(gc_last)))
        return units

    def chunks_body(cp, carry):
        per_chunk = [chunk_units(cp * cpi + j) for j in range(cpi)]
        units = [u for us in per_chunk for u in us]
        for _ in range(5):
            for u in units:
                u["pw"] = _dot1(u["pw"], u["pw"])
            for u in units:
                u["tinv"] = u["tinv"] + _dot1(u["tinv"], u["pw"])
        for u in units:
            u["sol"] = _dot3(u["tinv"], u["rhs"])
        for us in per_chunk:
            for u in us:
                s = state_ref[u["h"]]
                u["s"] = s
                u["v_new"] = u["sol"][:, :GDN_DV] - _dot1(u["sol"][:, GDN_DV:], s)
            for u in us:
                o_ref[0, pl.ds(u["r0"], c), u["sl"]] = _dot1(u["q_dec"], u["s"]) + _dot1(u["attn"], u["v_new"])
                state_ref[u["h"]] = u["s"] * u["bd"] + _dot1(u["k_dec"], u["v_new"], _TN)
        return carry

    n_chunks = tt // c
    cpi = GDN_CHUNKS_PER_ITER if n_chunks % GDN_CHUNKS_PER_ITER == 0 else 1
    lax.fori_loop(0, n_chunks // cpi, chunks_body, 0)

    @pl.when(tstep == pl.num_programs(1) - 1)
    def _():
        sout_ref[0] = state_ref[...]


def _gdn(q, k, v, ab, abt, a_log, dt_bias, s0, *, tt, t_valid):
    b, t, hd = q.shape
    assert t % tt == 0 and tt % GDN_CHUNK == 0
    prm_row = jnp.zeros((SUBLANES, LANES), F32)
    prm_row = prm_row.at[0, :GDN_HEADS].set(a_log).at[1, :GDN_HEADS].set(dt_bias)
    prm_col = jnp.zeros((2 * GDN_HEADS, LANES), F32)
    prm_col = prm_col.at[:GDN_HEADS, 0].set(a_log).at[:GDN_HEADS, 1].set(dt_bias)
    seq = pl.BlockSpec((1, tt, hd), lambda i, j: (i, j, 0))
    st = pl.BlockSpec((1, GDN_HEADS, GDN_DK, GDN_DV), lambda i, j: (i, 0, 0, 0))
    return pl.pallas_call(
        functools.partial(_gdn_kernel, tt=tt, t_valid=t_valid),
        grid=(b, t // tt),
        in_specs=[seq, seq, seq,
                  pl.BlockSpec((1, tt, LANES), lambda i, j: (i, j, 0)),
                  pl.BlockSpec((1, tt // GDN_CHUNK, 2 * GDN_HEADS, GDN_CHUNK), lambda i, j: (i, j, 0, 0)),
                  pl.BlockSpec((SUBLANES, LANES), lambda i, j: (0, 0)),
                  pl.BlockSpec((2 * GDN_HEADS, LANES), lambda i, j: (0, 0)),
                  st],
        out_specs=[seq, st],
        out_shape=[jax.ShapeDtypeStruct((b, t, hd), F32),
                   jax.ShapeDtypeStruct((b, GDN_HEADS, GDN_DK, GDN_DV), F32)],
        scratch_shapes=[pltpu.VMEM((GDN_HEADS, GDN_DK, GDN_DV), F32)],
        compiler_params=_cparams("parallel", "arbitrary"),
        name="gdn_core",
    )(q, k, v, ab, abt, prm_row, prm_col, s0)


def _mlp_kernel(uv_ref, lng_ref, lnb_ref, ws_ref, bst_ref, y_ref, vn_ref, *, rows, lc):
    mdim = MLP_GROUPS * MLP_GROUP_DIM
    ii = lax.broadcasted_iota(jnp.int32, (lc, lc), 0)
    jj = lax.broadcasted_iota(jnp.int32, (lc, lc), 1)
    keep = ii >= jj
    for r in range(rows // lc):
        rs = slice(r * lc, (r + 1) * lc)
        g = _gelu(uv_ref[0, rs, :])
        u = g[:, :mdim]
        vv = g[:, mdim:]
        mu = jnp.mean(vv, axis=-1, keepdims=True)
        xc = vv - mu
        vn = xc * lax.rsqrt(jnp.mean(xc * xc, axis=-1, keepdims=True) + EPS) * lng_ref[...] + lnb_ref[...]
        vn_ref[0, rs, :] = vn
        for gi in range(MLP_GROUPS):
            gs = slice(gi * MLP_GROUP_DIM, (gi + 1) * MLP_GROUP_DIM)
            wm = jnp.where(keep, ws_ref[gi], 0.0)
            s = _dot1(wm, vn[:, gs]) + bst_ref[:, gi:gi + 1]
            y_ref[0, rs, gs] = u[:, gs] * s


def _mlp_branch(proj3, col_block, ln_g, ln_b, w_s, b_s, *, rows, lc):
    b, t, _ = proj3.shape
    mdim = MLP_GROUPS * MLP_GROUP_DIM
    assert t % rows == 0 and rows % lc == 0
    ws = w_s[:, :lc, :lc]
    bst = jnp.transpose(b_s[:, :lc])
    shp = jax.ShapeDtypeStruct((b, t, mdim), F32)
    ospec = pl.BlockSpec((1, rows, mdim), lambda i, j: (i, j, 0))
    return pl.pallas_call(
        functools.partial(_mlp_kernel, rows=rows, lc=lc),
        grid=(b, t // rows),
        in_specs=[pl.BlockSpec((1, rows, 2 * mdim), lambda i, j: (i, j, col_block)),
                  pl.BlockSpec((1, mdim), lambda i, j: (0, 0)),
                  pl.BlockSpec((1, mdim), lambda i, j: (0, 0)),
                  pl.BlockSpec((MLP_GROUPS, lc, lc), lambda i, j: (0, 0, 0)),
                  pl.BlockSpec((lc, MLP_GROUPS), lambda i, j: (0, 0))],
        out_specs=[ospec, ospec],
        out_shape=[shp, shp],
        compiler_params=_cparams("parallel", "parallel"),
        name="mlp_branch",
    )(proj3, ln_g.reshape(1, mdim), ln_b.reshape(1, mdim), ws, bst)


def _outproj_kernel(o_ref, z_ref, y_ref, h_ref, gn_ref, w_ref, out_ref):
    parts = []
    for hh in range(GDN_HEADS):
        sl = slice(hh * GDN_DV, (hh + 1) * GDN_DV)
        parts.append((_rms_rows(o_ref[:, sl], gn_ref[...]) * _silu(z_ref[:, sl])).astype(BF16))
    parts.append(y_ref[...].astype(BF16))
    cat = jnp.concatenate(parts, axis=-1)
    out_ref[...] = h_ref[...] + _dg(cat, w_ref[...], _NN)


def _outproj(o2, proj2, z_block, y2, h2, gdn_norm, w_out, *, tm):
    n, d = h2.shape
    vd = GDN_HEADS * GDN_DV
    assert n % tm == 0
    return pl.pallas_call(
        _outproj_kernel,
        grid=(n // tm,),
        in_specs=[pl.BlockSpec((tm, vd), lambda i: (i, 0)),
                  pl.BlockSpec((tm, vd), lambda i: (i, z_block)),
                  pl.BlockSpec((tm, vd), lambda i: (i, 0)),
                  pl.BlockSpec((tm, d), lambda i: (i, 0)),
                  pl.BlockSpec((1, GDN_DV), lambda i: (0, 0)),
                  pl.BlockSpec(w_out.shape, lambda i: (0, 0), pipeline_mode=pl.Buffered(1))],
        out_specs=pl.BlockSpec((tm, d), lambda i: (i, 0)),
        out_shape=jax.ShapeDtypeStruct((n, d), F32),
        compiler_params=_cparams("parallel"),
        name="out_proj",
    )(o2, proj2, y2, h2, gdn_norm.reshape(1, GDN_DV), w_out.astype(BF16))


def _topk_axis0(x, k, payloads=()):
    r = x.shape[0]
    iota = lax.broadcasted_iota(jnp.int32, x.shape, 0)
    vals, idxs = [], []
    outs = [[] for _ in payloads]
    for _ in range(k):
        m = jnp.max(x, axis=0, keepdims=True)
        i = jnp.min(jnp.where(x == m, iota, r), axis=0, keepdims=True)
        hit = iota == i
        vals.append(m)
        idxs.append(i)
        for p, acc in zip(payloads, outs):
            acc.append(jnp.sum(jnp.where(hit, p, 0), axis=0, keepdims=True))
        x = jnp.where(hit, -jnp.inf, x)
    cat = lambda parts: jnp.concatenate(parts, axis=0)
    return cat(vals), cat(idxs), [cat(acc) for acc in outs]


def _peer_topk_kernel(q_ref, keys_ref, eidx_ref, gate_ref):
    k = PEER_TOPK

    def one_head(h):
        sv, si = [], []
        for s in range(2):
            c0 = pl.multiple_of(h * (2 * PEER_QHALF) + s * PEER_QHALF, PEER_QHALF)
            qhs = q_ref[:, pl.ds(c0, PEER_QHALF)]
            sc_t = _dot3(keys_ref[s, h], qhs, _NT)
            v, i, _ = _topk_axis0(sc_t, k)
            sv.append(v)
            si.append(i)
        half = k // 2
        sub = lax.broadcasted_iota(jnp.int32, (half, sv[0].shape[1]), 0)
        cand_parts = [sv[0][0:1, :] + sv[1]]
        cidx_parts = [si[0][0:1, :] * N_KEYS + si[1]]
        for a in range(1, half):
            keep = sub < (k // (a + 1))
            cand_parts.append(jnp.where(keep, sv[0][a:a + 1, :] + sv[1][0:half, :], -jnp.inf))
            cidx_parts.append(si[0][a:a + 1, :] * N_KEYS + si[1][0:half, :])
        cand_parts.append(sv[0][half:, :] + sv[1][0:1, :])
        cidx_parts.append(si[0][half:, :] * N_KEYS + si[1][0:1, :])
        cand = jnp.concatenate(cand_parts, axis=0)
        cidx = jnp.concatenate(cidx_parts, axis=0)
        fv, _, (fe,) = _topk_axis0(cand, k, (cidx,))
        e = jnp.exp(fv - fv[0:1, :])
        gate = e / jnp.sum(e, axis=0, keepdims=True)
        r0 = pl.multiple_of(h * k, k)
        eidx_ref[pl.ds(r0, k), :] = fe
        gate_ref[pl.ds(r0, k), :] = gate

    def heads_body(hp, carry):
        for j in range(PEER_TOPK_HEADS_PER_ITER):
            one_head(hp * PEER_TOPK_HEADS_PER_ITER + j)
        return carry

    lax.fori_loop(0, PEER_HEADS // PEER_TOPK_HEADS_PER_ITER, heads_body, 0)


def _peer_topk(q, sub_keys, *, tk):
    n, qd = q.shape
    assert n % tk == 0
    rows = PEER_HEADS * PEER_TOPK
    return pl.pallas_call(
        _peer_topk_kernel,
        grid=(n // tk,),
        in_specs=[pl.BlockSpec((tk, qd), lambda i: (i, 0)),
                  pl.BlockSpec(sub_keys.shape, lambda i: (0, 0, 0, 0))],
        out_specs=[pl.BlockSpec((rows, tk), lambda i: (0, i)),
                   pl.BlockSpec((rows, tk), lambda i: (0, i))],
        out_shape=[jax.ShapeDtypeStruct((rows, n), jnp.int32),
                   jax.ShapeDtypeStruct((rows, n), F32)],
        compiler_params=_cparams("parallel"),
        name="peer_topk",
    )(q, sub_keys)


PEER_ROWS = PEER_HEADS * PEER_TOPK
PEER_SLOTS = 16
PEER_LOOK = PEER_SLOTS - 1
assert PEER_SLOTS % 2 == 0 and PEER_LOOK % 2 == 1
PEER_IDX_TAIL = 16
assert PEER_LOOK <= PEER_IDX_TAIL and (PEER_IDX_TAIL * PEER_ROWS) % 1024 == 0
PEER_ACCS = 4
PEER_DOT_COPIES = 5
PEER_MIX_COPIES = 3
assert (PEER_DOT_COPIES + PEER_MIX_COPIES) * (PEER_ROWS // SUBLANES) == PEER_ROWS


def _sublane_fold(parts):
    sub = lax.broadcasted_iota(jnp.int32, parts[0].shape, 0)
    for dist in (4, 2, 1):
        low = (sub & dist) == 0
        nxt = []
        for a in range(len(parts) // 2):
            lo_src, hi_src = parts[a], parts[a + len(parts) // 2]
            kept = jnp.where(low, lo_src, hi_src)
            if 2 * dist == SUBLANES:
                moved = pltpu.roll(jnp.where(low, hi_src, lo_src), dist, axis=0)
            else:
                moved = jnp.where(low, pltpu.roll(lo_src, SUBLANES - dist, axis=0), pltpu.roll(hi_src, dist, axis=0))
            nxt.append(kept + moved)
        parts = nxt
    return parts[0]


def _peer_gather_kernel(idx_ref, x_ref, gate_ref, h_ref, tab_ref, out_ref, buf_even, buf_odd, sem_ref, coef_ref, *, tb):
    rows = PEER_ROWS
    nc = x_ref.shape[1] // LANES
    half = nc // 2
    groups = rows // SUBLANES
    step = pl.program_id(0)
    hi_mask = jnp.uint32(0xFFFF0000)
    bufs = (buf_even, buf_odd)

    def slot_of(tok):
        return (tok // 2) % (PEER_SLOTS // 2)

    def start_rows(arr, tok, r0, count):
        for j in range(count):
            r = r0 + j
            pltpu.make_async_copy(tab_ref.at[idx_ref[tok * rows + r]], arr.at[slot_of(tok), r],
                                  sem_ref.at[tok % PEER_SLOTS]).start(priority=j % 2)

    def wait_token(arr, tok):
        pltpu.make_async_copy(tab_ref.at[pl.ds(0, rows)], arr.at[slot_of(tok)], sem_ref.at[tok % PEER_SLOTS]).wait()

    def load_x(t):
        xrow = x_ref[pl.ds(t, 1), :]
        x_lo = jnp.concatenate([xrow[:, c * LANES:(c + 1) * LANES] for c in range(half)], axis=0)
        x_hi = jnp.concatenate([xrow[:, c * LANES:(c + 1) * LANES] for c in range(half, nc)], axis=0)
        return x_lo, x_hi

    def dot_group(buf, x, g):
        parts = []
        for s in range(SUBLANES):
            r = g * SUBLANES + s
            u_lo = lax.bitcast_convert_type(buf[r, 0:half, :] << 16, F32)
            u_hi = lax.bitcast_convert_type(buf[r, half:nc, :] << 16, F32)
            parts.append(u_lo * x[0] + u_hi * x[1])
        return _sublane_fold(parts)

    def finish_dot(t, folded):
        act = jnp.sum(jnp.concatenate(folded, axis=0), axis=-1, keepdims=True)
        lane_tok = lax.broadcasted_iota(jnp.int32, (rows, tb), 1)
        gcol = jnp.sum(jnp.where(lane_tok == t, gate_ref[0], 0.0), axis=-1, keepdims=True)
        coef_ref[...] = jnp.broadcast_to(gcol * _gelu(act), (rows, LANES))

    def mix_group(buf, g, acc_lo, acc_hi):
        for s in range(SUBLANES):
            r = g * SUBLANES + s
            c = jnp.broadcast_to(coef_ref[r:r + 1, :], (half, LANES))
            a = r % PEER_ACCS
            acc_lo[a] = acc_lo[a] + lax.bitcast_convert_type(buf[r, 0:half, :] & hi_mask, F32) * c
            acc_hi[a] = acc_hi[a] + lax.bitcast_convert_type(buf[r, half:nc, :] & hi_mask, F32) * c

    def emit(t, acc_lo, acc_hi):
        mix_lo = (acc_lo[0] + acc_lo[1]) + (acc_lo[2] + acc_lo[3])
        mix_hi = (acc_hi[0] + acc_hi[1]) + (acc_hi[2] + acc_hi[3])
        mix_row = jnp.concatenate([mix_lo[c:c + 1, :] for c in range(half)]
                                  + [mix_hi[c:c + 1, :] for c in range(half)], axis=1)
        out_ref[pl.ds(t, 1), :] = h_ref[pl.ds(t, 1), :] + mix_row

    def token(t, parity):
        arr, arr_next = bufs[parity], bufs[(parity + PEER_LOOK) % 2]
        tok_next = t + PEER_LOOK
        wait_token(arr, t)
        buf = arr.at[slot_of(t)]
        x = load_x(t)
        folded = []
        for g in range(groups):
            start_rows(arr_next, tok_next, g * PEER_DOT_COPIES, PEER_DOT_COPIES)
            folded.append(dot_group(buf, x, g))
        finish_dot(t, folded)
        acc_lo = [jnp.zeros((half, LANES), F32) for _ in range(PEER_ACCS)]
        acc_hi = [jnp.zeros((half, LANES), F32) for _ in range(PEER_ACCS)]
        for g in range(groups):
            start_rows(arr_next, tok_next, groups * PEER_DOT_COPIES + g * PEER_MIX_COPIES, PEER_MIX_COPIES)
            mix_group(buf, g, acc_lo, acc_hi)
        emit(t, acc_lo, acc_hi)

    @pl.when(step == 0)
    def _():
        for s in range(PEER_LOOK):
            def prime(i, carry, s=s):
                start_rows(bufs[s % 2], s, i * SUBLANES, SUBLANES)
                return carry
            lax.fori_loop(0, groups, prime, 0)

    def pair(p, carry):
        token(2 * p, 0)
        token(2 * p + 1, 1)
        return carry

    lax.fori_loop(0, tb // 2, pair, 0)

    @pl.when(step == pl.num_programs(0) - 1)
    def _():
        for s in range(PEER_LOOK):
            wait_token(bufs[(tb + s) % 2], tb + s)


def _peer_gather(eidx, xn, gate_blocks, h, table, *, tb):
    n, d = h.shape
    rows = PEER_ROWS
    assert n % tb == 0 and tb % PEER_SLOTS == 0 and PEER_ACCS == 4
    nb = n // tb
    idx2 = eidx.reshape(nb, tb * rows)
    idx_ext = jnp.concatenate([idx2, jnp.roll(idx2[:, :PEER_IDX_TAIL * rows], -1, axis=0)], axis=1).reshape(-1)
    return pl.pallas_call(
        functools.partial(_peer_gather_kernel, tb=tb),
        grid=(nb,),
        in_specs=[pl.BlockSpec(((tb + PEER_IDX_TAIL) * rows,), lambda i: (i,), memory_space=pltpu.SMEM),
                  pl.BlockSpec((tb, d), lambda i: (i, 0)),
                  pl.BlockSpec((1, rows, tb), lambda i: (i, 0, 0)),
                  pl.BlockSpec((tb, d), lambda i: (i, 0)),
                  pl.BlockSpec(memory_space=pl.ANY)],
        out_specs=pl.BlockSpec((tb, d), lambda i: (i, 0)),
        out_shape=jax.ShapeDtypeStruct((n, d), F32),
        scratch_shapes=[pltpu.VMEM((PEER_SLOTS // 2, rows, d // LANES, LANES), jnp.uint32),
                        pltpu.VMEM((PEER_SLOTS // 2, rows, d // LANES, LANES), jnp.uint32),
                        pltpu.SemaphoreType.DMA((PEER_SLOTS,)),
                        pltpu.VMEM((rows, LANES), F32)],
        compiler_params=_cparams("arbitrary"),
        name="peer_gather",
    )(idx_ext, xn, gate_blocks, h, table)


def _ple_kernel(h_ref, p_ref, n3_ref, fn_ref, wg_ref, wp_ref, y_ref):
    h = h_ref[...]
    gate = _sigmoid(_dot1(_rms_rows(h, n3_ref[...]), wg_ref[...]))
    h = h + gate * _dot1(p_ref[...], wp_ref[...])
    y_ref[...] = _rms_rows(h, fn_ref[...])


def _ple_final(h, p, norm3, final_norm, w_gate, w_proj, *, tm):
    n, d = h.shape
    pd = p.shape[1]
    assert n % tm == 0
    return pl.pallas_call(
        _ple_kernel,
        grid=(n // tm,),
        in_specs=[pl.BlockSpec((tm, d), lambda i: (i, 0)),
                  pl.BlockSpec((tm, pd), lambda i: (i, 0)),
                  pl.BlockSpec((1, d), lambda i: (0, 0)),
                  pl.BlockSpec((1, d), lambda i: (0, 0)),
                  pl.BlockSpec((d, d), lambda i: (0, 0), pipeline_mode=pl.Buffered(1)),
                  pl.BlockSpec((pd, d), lambda i: (0, 0), pipeline_mode=pl.Buffered(1))],
        out_specs=pl.BlockSpec((tm, d), lambda i: (i, 0)),
        out_shape=jax.ShapeDtypeStruct((n, d), F32),
        compiler_params=_cparams("parallel"),
        name="ple_final",
    )(h, p, norm3.reshape(1, d), final_norm.reshape(1, d), w_gate.astype(BF16), w_proj.astype(BF16))


QK_DIM = GDN_HEADS * GDN_DK
V_DIM = GDN_HEADS * GDN_DV
CONV_DIM = 2 * QK_DIM + V_DIM
MLP_DIM = MLP_GROUPS * MLP_GROUP_DIM
Z_COL = CONV_DIM
UV_COL = Z_COL + V_DIM
AB_COL = UV_COL + 2 * MLP_DIM
PROJ_COLS = AB_COL + LANES


def _relayout_w_in(w_in):
    c0 = CONV_DIM
    c2 = c0 + 2 * GDN_HEADS
    c3 = c2 + V_DIM
    ab = jnp.pad(w_in[:, c0:c2], ((0, 0), (0, LANES - 2 * GDN_HEADS)))
    return jnp.concatenate([w_in[:, :c0], w_in[:, c2:c3], w_in[:, c3:], ab], axis=1)


def _pack_tables(peer_u, peer_v):
    ub = lax.bitcast_convert_type(peer_u.astype(BF16), jnp.uint16).astype(jnp.uint32)
    vb = lax.bitcast_convert_type(peer_v.astype(BF16), jnp.uint16).astype(jnp.uint32)
    e, d = peer_u.shape
    return (ub | (vb << 16)).reshape(e, d // LANES, LANES)


def _row_tile(n, want):
    t = min(n, want)
    assert n % t == 0
    return t


IN_PROJ_ROWS = 256
TOKEN_ROWS = 512
SEQ_ROWS = 512
PEER_TOKENS = 128


def _layer(x, p, s0, buf, w, final_norm):
    b, t, d = x.shape
    n = b * t
    x2 = x.reshape(n, d)
    tok_rows = _row_tile(n, TOKEN_ROWS)
    seq_rows = _row_tile(t, SEQ_ROWS)

    proj = _norm_matmul_resident(x2, w["norm1"], w["w_in"], tm=_row_tile(n, IN_PROJ_ROWS), name="in_proj")
    proj3 = proj.reshape(b, t, PROJ_COLS)
    new_buf = jnp.concatenate([buf, proj3[:, :, :CONV_DIM]], axis=1)[:, t:]

    buf8 = jnp.pad(buf, ((0, 0), (SUBLANES - (CONV_W - 1), 0), (0, 0)))
    q, k, v = _conv_qkv(proj3, buf8, w["conv_w"], tt=seq_rows)

    tg = -(-t // GDN_CHUNK) * GDN_CHUNK
    ab3 = proj3[:, :, AB_COL:]
    if tg != t:
        padt = ((0, 0), (0, tg - t), (0, 0))
        q, k, v, ab3 = (jnp.pad(a, padt) for a in (q, k, v, ab3))
    abt = jnp.transpose(ab3[:, :, :2 * GDN_HEADS].reshape(b, tg // GDN_CHUNK, GDN_CHUNK, 2 * GDN_HEADS),
                        (0, 1, 3, 2))
    o, s_new = _gdn(q, k, v, ab3, abt, w["a_log"], w["dt_bias"], s0, tt=_row_tile(tg, SEQ_ROWS), t_valid=t)
    o = o[:, :t]

    y_b, vn = _mlp_branch(proj3, UV_COL // (2 * MLP_DIM), w["ln_g"], w["ln_b"], w["w_s"], w["b_s"],
                          rows=seq_rows, lc=min(t, MLP_CHUNK))

    h1 = _outproj(o.reshape(n, V_DIM), proj, Z_COL // V_DIM, y_b.reshape(n, MLP_DIM), x2,
                  w["gdn_norm"], w["w_out"], tm=tok_rows)

    qp, xn2 = _norm_matmul_resident(h1, w["norm2"], w["w_q"], tm=tok_rows, emit_xn=True, name="peer_query")
    eidx_t, gate_t = _peer_topk(qp, w["sub_keys"], tk=tok_rows)
    gate_blocks = jnp.transpose(gate_t.reshape(PEER_ROWS, n // PEER_TOKENS, PEER_TOKENS), (1, 0, 2))
    h2 = _peer_gather(jnp.transpose(eidx_t), xn2, gate_blocks, h1, w["table"], tb=PEER_TOKENS)

    y = _ple_final(h2, p.reshape(n, -1), w["norm3"], final_norm, w["w_ple_gate"], w["w_ple_proj"], tm=tok_rows)
    return y.reshape(b, t, d), s_new, new_buf, vn


def kernel(x_prompt, x_sample, state_gdn, cache_conv, p_prompt, p_sample, norm1, w_in, conv_w, a_log, dt_bias,
           gdn_norm, ln_g, ln_b, w_s, b_s, w_out, norm2, w_q, sub_keys, peer_u, peer_v, norm3, w_ple_gate,
           w_ple_proj, final_norm):
    assert norm1.shape[0] == 1, "single layer"
    w = dict(norm1=norm1[0], w_in=_relayout_w_in(w_in[0]), conv_w=conv_w[0], a_log=a_log[0], dt_bias=dt_bias[0],
             gdn_norm=gdn_norm[0], ln_g=ln_g[0], ln_b=ln_b[0], w_s=w_s[0], b_s=b_s[0], w_out=w_out[0],
             norm2=norm2[0], w_q=w_q[0], sub_keys=sub_keys[0], table=_pack_tables(peer_u[0], peer_v[0]),
             norm3=norm3[0], w_ple_gate=w_ple_gate[0], w_ple_proj=w_ple_proj[0])
    bp = x_prompt.shape[0]
    s0p = jnp.zeros((bp, GDN_HEADS, GDN_DK, GDN_DV), F32)
    b0p = jnp.zeros((bp, CONV_W - 1, CONV_DIM), F32)
    y_s, s_s, b_s_new, v_s = _layer(x_sample, p_sample[0], state_gdn[0], cache_conv[0], w, final_norm)
    y_p, s_p, b_p, _ = _layer(x_prompt, p_prompt[0], s0p, b0p, w, final_norm)
    return (y_p, y_s, s_p[None], b_p[None], s_s[None], b_s_new[None], v_s[None])
```

```python
import functools
import math

import jax
import jax.numpy as jnp
from jax import lax
from jax.experimental import pallas as pl
from jax.experimental.pallas import tpu as pltpu

F32 = jnp.float32
BF16 = jnp.bfloat16

EPS = 1e-6
GDN_CHUNK = 64
GDN_HEADS = 8
GDN_CHUNKS_PER_ITER = 4
GDN_DK = 128
GDN_DV = 128
CONV_W = 4
MLP_CHUNK = 128
MLP_GROUPS = 8
MLP_GROUP_DIM = 128
PEER_HEADS = 8
PEER_QHALF = 128
N_KEYS = 128
PEER_TOPK = 16
PEER_TOPK_HEADS_PER_ITER = 8

LANES = 128
SUBLANES = 8
V7X_VMEM_BYTES = 64 * 1024 * 1024
VMEM_LIMIT_BYTES = V7X_VMEM_BYTES * 7 // 8


def _cparams(*sem):
    return pltpu.CompilerParams(dimension_semantics=sem, vmem_limit_bytes=VMEM_LIMIT_BYTES)


def _split3(x):
    hi = x.astype(BF16)
    r1 = x - hi.astype(F32)
    mid = r1.astype(BF16)
    lo = (r1 - mid.astype(F32)).astype(BF16)
    return hi, mid, lo


def _dg(a, b, dims):
    return lax.dot_general(a, b, (dims, ((), ())), preferred_element_type=F32)


_NN = ((1,), (0,))
_NT = ((1,), (1,))
_TN = ((0,), (0,))


def _dot1(a, b, dims=_NN):
    return _dg(a.astype(BF16), b.astype(BF16), dims)


def _dot3(a, b, dims=_NN):
    ah = a.astype(BF16)
    al = (a - ah.astype(F32)).astype(BF16)
    bh = b.astype(BF16)
    bl = (b - bh.astype(F32)).astype(BF16)
    return _dg(ah, bh, dims) + (_dg(ah, bl, dims) + _dg(al, bh, dims))


def _dot_exact_lhs(a_exact_bf16, b, dims=_NN):
    b0, b1, b2 = _split3(b)
    return _dg(a_exact_bf16, b0, dims) + (_dg(a_exact_bf16, b1, dims) + _dg(a_exact_bf16, b2, dims))


def _dot_exact_rhs(a, b_exact_bf16, dims=_NN):
    a0, a1, a2 = _split3(a)
    return _dg(a0, b_exact_bf16, dims) + (_dg(a1, b_exact_bf16, dims) + _dg(a2, b_exact_bf16, dims))


def _rms_rows(x, gain):
    ms = jnp.mean(x * x, axis=-1, keepdims=True)
    return x * lax.rsqrt(ms + EPS) * gain


def _sigmoid(x):
    return 1.0 / (1.0 + jnp.exp(-x))


def _silu(x):
    return x * _sigmoid(x)


def _gelu(x):
    return 0.5 * x * (1.0 + lax.erf(x * (1.0 / math.sqrt(2.0))))


def _softplus(x):
    return jnp.maximum(x, 0.0) + jnp.log1p(jnp.exp(-jnp.abs(x)))


def _norm_matmul_resident_kernel(x_ref, g_ref, w_ref, o_ref, *xn_out, emit_xn):
    xn = _rms_rows(x_ref[...], g_ref[...])
    if emit_xn:
        xn_out[0][...] = xn
    o_ref[...] = _dg(xn.astype(BF16), w_ref[...], _NN)


def _norm_matmul_resident(x, gain, w, *, tm, emit_xn=False, name):
    n, d = x.shape
    m = w.shape[1]
    assert n % tm == 0
    out_shape = [jax.ShapeDtypeStruct((n, m), F32)]
    out_specs = [pl.BlockSpec((tm, m), lambda i: (i, 0))]
    if emit_xn:
        out_shape.append(jax.ShapeDtypeStruct((n, d), F32))
        out_specs.append(pl.BlockSpec((tm, d), lambda i: (i, 0)))
    res = pl.pallas_call(
        functools.partial(_norm_matmul_resident_kernel, emit_xn=emit_xn),
        grid=(n // tm,),
        in_specs=[pl.BlockSpec((tm, d), lambda i: (i, 0)),
                  pl.BlockSpec((1, d), lambda i: (0, 0)),
                  pl.BlockSpec((d, m), lambda i: (0, 0), pipeline_mode=pl.Buffered(1))],
        out_specs=out_specs,
        out_shape=out_shape,
        compiler_params=_cparams("parallel"),
        name=name,
    )(x, gain.reshape(1, d), w.astype(BF16))
    return res if emit_xn else res[0]


def _conv_kernel(x_ref, buf_ref, w_ref, q_ref, k_ref, v_ref, xp_ref, *, tt):
    t = pl.program_id(1)
    halo = SUBLANES

    @pl.when(t == 0)
    def _():
        xp_ref[0:halo, :] = buf_ref[0]

    @pl.when(t > 0)
    def _():
        xp_ref[0:halo, :] = xp_ref[tt:tt + halo, :]

    xp_ref[halo:halo + tt, :] = x_ref[0]
    base = halo - (CONV_W - 1)
    y = xp_ref[base:base + tt, :] * w_ref[0:1, :]
    for j in range(1, CONV_W):
        y = y + xp_ref[base + j:base + j + tt, :] * w_ref[j:j + 1, :]
    y = _silu(y)
    qk_dim = GDN_HEADS * GDN_DK
    for h in range(GDN_HEADS):
        qh = y[:, h * GDN_DK:(h + 1) * GDN_DK]
        qn = qh * lax.rsqrt(jnp.sum(qh * qh, axis=-1, keepdims=True) + EPS)
        q_ref[0, :, h * GDN_DK:(h + 1) * GDN_DK] = qn * (GDN_DK ** -0.5)
        kh = y[:, qk_dim + h * GDN_DK:qk_dim + (h + 1) * GDN_DK]
        k_ref[0, :, h * GDN_DK:(h + 1) * GDN_DK] = kh * lax.rsqrt(jnp.sum(kh * kh, axis=-1, keepdims=True) + EPS)
    v_ref[0] = y[:, 2 * qk_dim:]


def _conv_qkv(proj3, buf8, conv_w, *, tt):
    b, t, _ = proj3.shape
    cdim = conv_w.shape[1]
    hd = GDN_HEADS * GDN_DK
    assert t % tt == 0
    shp = jax.ShapeDtypeStruct((b, t, hd), F32)
    ospec = pl.BlockSpec((1, tt, hd), lambda i, j: (i, j, 0))
    return pl.pallas_call(
        functools.partial(_conv_kernel, tt=tt),
        grid=(b, t // tt),
        in_specs=[pl.BlockSpec((1, tt, cdim), lambda i, j: (i, j, 0)),
                  pl.BlockSpec((1, SUBLANES, cdim), lambda i, j: (i, 0, 0)),
                  pl.BlockSpec((CONV_W, cdim), lambda i, j: (0, 0))],
        out_specs=[ospec, ospec, ospec],
        out_shape=[shp, shp, shp],
        scratch_shapes=[pltpu.VMEM((tt + 2 * SUBLANES, cdim), F32)],
        compiler_params=_cparams("parallel", "arbitrary"),
        name="conv_qkv",
    )(proj3, buf8, conv_w)


def _gdn_kernel(q_ref, k_ref, v_ref, ab_ref, abt_ref, prm_row_ref, prm_col_ref, s0_ref,
                o_ref, sout_ref, state_ref, *, tt, t_valid):
    tstep = pl.program_id(1)
    c = GDN_CHUNK

    @pl.when(tstep == 0)
    def _():
        state_ref[...] = s0_ref[0]

    ii = lax.broadcasted_iota(jnp.int32, (c, c), 0)
    jj = lax.broadcasted_iota(jnp.int32, (c, c), 1)
    incl = ii >= jj
    strict = ii > jj
    tri = incl.astype(BF16)
    tri_t = (jj >= ii).astype(BF16)
    eye = (ii == jj).astype(F32)

    alog_row = prm_row_ref[0:1, :]
    dtb_row = prm_row_ref[1:2, :]
    alog_col = prm_col_ref[:, 0:1]
    dtb_col = prm_col_ref[:, 1:2]

    def chunk_units(ci):
        r0 = pl.multiple_of(ci * c, c)
        tpos = tstep * tt + r0
        a_blk = ab_ref[0, pl.ds(r0, c), :]
        valid_col = (tpos + lax.broadcasted_iota(jnp.int32, (c, LANES), 0)) < t_valid
        g_col = jnp.where(valid_col, -jnp.exp(alog_row) * _softplus(a_blk + dtb_row), 0.0)
        beta_col = jnp.where(valid_col, _sigmoid(a_blk), 0.0)
        gc_col = _dot_exact_lhs(tri, g_col)
        at_blk = abt_ref[0, ci]
        valid_row = (tpos + lax.broadcasted_iota(jnp.int32, (2 * GDN_HEADS, c), 1)) < t_valid
        g_row = jnp.where(valid_row, -jnp.exp(alog_col) * _softplus(at_blk + dtb_col), 0.0)
        gc_row = _dot_exact_rhs(g_row, tri_t)
        units = []
        for h in range(GDN_HEADS):
            sl = slice(h * GDN_DK, (h + 1) * GDN_DK)
            qh = q_ref[0, pl.ds(r0, c), sl]
            kh = k_ref[0, pl.ds(r0, c), sl]
            vh = v_ref[0, pl.ds(r0, c), sl]
            gc = gc_col[:, h:h + 1]
            beta = beta_col[:, GDN_HEADS + h:GDN_HEADS + h + 1]
            gcr = gc_row[h:h + 1, :]
            gc_last = gc_col[c - 1:c, h:h + 1]
            decay = jnp.where(incl, jnp.exp(jnp.where(incl, gc - gcr, 0.0)), 0.0)
            egc = jnp.exp(gc)
            kb = kh * beta
            lmat = jnp.where(strict, _dot1(kb, kh, _NT) * decay, 0.0)
            units.append(dict(
                h=h, r0=r0, sl=sl, tinv=eye - lmat, pw=lmat,
                rhs=jnp.concatenate([vh * beta, kb * egc], axis=-1),
                attn=_dot1(qh, kh, _NT) * decay, q_dec=qh * egc,
                k_dec=kh * jnp.exp(gc_last - gc), bd=jnp.exp(gc_last)))
        return units

    def chunks_body(cp, carry):
        per_chunk = [chunk_units(cp * cpi + j) for j in range(cpi)]
        units = [u for us in per_chunk for u in us]
        for _ in range(5):
            for u in units:
                u["pw"] = _dot1(u["pw"], u["pw"])
            for u in units:
                u["tinv"] = u["tinv"] + _dot1(u["tinv"], u["pw"])
        for u in units:
            u["sol"] = _dot3(u["tinv"], u["rhs"])
        for us in per_chunk:
            for u in us:
                s = state_ref[u["h"]]
                u["s"] = s
                u["v_new"] = u["sol"][:, :GDN_DV] - _dot1(u["sol"][:, GDN_DV:], s)
            for u in us:
                o_ref[0, pl.ds(u["r0"], c), u["sl"]] = _dot1(u["q_dec"], u["s"]) + _dot1(u["attn"], u["v_new"])
                state_ref[u["h"]] = u["s"] * u["bd"] + _dot1(u["k_dec"], u["v_new"], _TN)
        return carry

    n_chunks = tt // c
    cpi = GDN_CHUNKS_PER_ITER if n_chunks % GDN_CHUNKS_PER_ITER == 0 else 1
    lax.fori_loop(0, n_chunks // cpi, chunks_body, 0)

    @pl.when(tstep == pl.num_programs(1) - 1)
    def _():
        sout_ref[0] = state_ref[...]


def _gdn(q, k, v, ab, abt, a_log, dt_bias, s0, *, tt, t_valid):
    b, t, hd = q.shape
    assert t % tt == 0 and tt % GDN_CHUNK == 0
    prm_row = jnp.zeros((SUBLANES, LANES), F32)
    prm_row = prm_row.at[0, :GDN_HEADS].set(a_log).at[1, :GDN_HEADS].set(dt_bias)
    prm_col = jnp.zeros((2 * GDN_HEADS, LANES), F32)
    prm_col = prm_col.at[:GDN_HEADS, 0].set(a_log).at[:GDN_HEADS, 1].set(dt_bias)
    seq = pl.BlockSpec((1, tt, hd), lambda i, j: (i, j, 0))
    st = pl.BlockSpec((1, GDN_HEADS, GDN_DK, GDN_DV), lambda i, j: (i, 0, 0, 0))
    return pl.pallas_call(
        functools.partial(_gdn_kernel, tt=tt, t_valid=t_valid),
        grid=(b, t // tt),
        in_specs=[seq, seq, seq,
                  pl.BlockSpec((1, tt, LANES), lambda i, j: (i, j, 0)),
                  pl.BlockSpec((1, tt // GDN_CHUNK, 2 * GDN_HEADS, GDN_CHUNK), lambda i, j: (i, j, 0, 0)),
                  pl.BlockSpec((SUBLANES, LANES), lambda i, j: (0, 0)),
                  pl.BlockSpec((2 * GDN_HEADS, LANES), lambda i, j: (0, 0)),
                  st],
        out_specs=[seq, st],
        out_shape=[jax.ShapeDtypeStruct((b, t, hd), F32),
                   jax.ShapeDtypeStruct((b, GDN_HEADS, GDN_DK, GDN_DV), F32)],
        scratch_shapes=[pltpu.VMEM((GDN_HEADS, GDN_DK, GDN_DV), F32)],
        compiler_params=_cparams("parallel", "arbitrary"),
        name="gdn_core",
    )(q, k, v, ab, abt, prm_row, prm_col, s0)


def _mlp_kernel(uv_ref, lng_ref, lnb_ref, ws_ref, bst_ref, y_ref, vn_ref, *, rows, lc):
    mdim = MLP_GROUPS * MLP_GROUP_DIM
    ii = lax.broadcasted_iota(jnp.int32, (lc, lc), 0)
    jj = lax.broadcasted_iota(jnp.int32, (lc, lc), 1)
    keep = ii >= jj
    for r in range(rows // lc):
        rs = slice(r * lc, (r + 1) * lc)
        g = _gelu(uv_ref[0, rs, :])
        u = g[:, :mdim]
        vv = g[:, mdim:]
        mu = jnp.mean(vv, axis=-1, keepdims=True)
        xc = vv - mu
        vn = xc * lax.rsqrt(jnp.mean(xc * xc, axis=-1, keepdims=True) + EPS) * lng_ref[...] + lnb_ref[...]
        vn_ref[0, rs, :] = vn
        for gi in range(MLP_GROUPS):
            gs = slice(gi * MLP_GROUP_DIM, (gi + 1) * MLP_GROUP_DIM)
            wm = jnp.where(keep, ws_ref[gi], 0.0)
            s = _dot1(wm, vn[:, gs]) + bst_ref[:, gi:gi + 1]
            y_ref[0, rs, gs] = u[:, gs] * s


def _mlp_branch(proj3, col_block, ln_g, ln_b, w_s, b_s, *, rows, lc):
    b, t, _ = proj3.shape
    mdim = MLP_GROUPS * MLP_GROUP_DIM
    assert t % rows == 0 and rows % lc == 0
    ws = w_s[:, :lc, :lc]
    bst = jnp.transpose(b_s[:, :lc])
    shp = jax.ShapeDtypeStruct((b, t, mdim), F32)
    ospec = pl.BlockSpec((1, rows, mdim), lambda i, j: (i, j, 0))
    return pl.pallas_call(
        functools.partial(_mlp_kernel, rows=rows, lc=lc),
        grid=(b, t // rows),
        in_specs=[pl.BlockSpec((1, rows, 2 * mdim), lambda i, j: (i, j, col_block)),
                  pl.BlockSpec((1, mdim), lambda i, j: (0, 0)),
                  pl.BlockSpec((1, mdim), lambda i, j: (0, 0)),
                  pl.BlockSpec((MLP_GROUPS, lc, lc), lambda i, j: (0, 0, 0)),
                  pl.BlockSpec((lc, MLP_GROUPS), lambda i, j: (0, 0))],
        out_specs=[ospec, ospec],
        out_shape=[shp, shp],
        compiler_params=_cparams("parallel", "parallel"),
        name="mlp_branch",
    )(proj3, ln_g.reshape(1, mdim), ln_b.reshape(1, mdim), ws, bst)


def _outproj_kernel(o_ref, z_ref, y_ref, h_ref, gn_ref, w_ref, out_ref):
    parts = []
    for hh in range(GDN_HEADS):
        sl = slice(hh * GDN_DV, (hh + 1) * GDN_DV)
        parts.append((_rms_rows(o_ref[:, sl], gn_ref[...]) * _silu(z_ref[:, sl])).astype(BF16))
    parts.append(y_ref[...].astype(BF16))
    cat = jnp.concatenate(parts, axis=-1)
    out_ref[...] = h_ref[...] + _dg(cat, w_ref[...], _NN)


def _outproj(o2, proj2, z_block, y2, h2, gdn_norm, w_out, *, tm):
    n, d = h2.shape
    vd = GDN_HEADS * GDN_DV
    assert n % tm == 0
    return pl.pallas_call(
        _outproj_kernel,
        grid=(n // tm,),
        in_specs=[pl.BlockSpec((tm, vd), lambda i: (i, 0)),
                  pl.BlockSpec((tm, vd), lambda i: (i, z_block)),
                  pl.BlockSpec((tm, vd), lambda i: (i, 0)),
                  pl.BlockSpec((tm, d), lambda i: (i, 0)),
                  pl.BlockSpec((1, GDN_DV), lambda i: (0, 0)),
                  pl.BlockSpec(w_out.shape, lambda i: (0, 0), pipeline_mode=pl.Buffered(1))],
        out_specs=pl.BlockSpec((tm, d), lambda i: (i, 0)),
        out_shape=jax.ShapeDtypeStruct((n, d), F32),
        compiler_params=_cparams("parallel"),
        name="out_proj",
    )(o2, proj2, y2, h2, gdn_norm.reshape(1, GDN_DV), w_out.astype(BF16))


def _topk_axis0(x, k, payloads=()):
    r = x.shape[0]
    iota = lax.broadcasted_iota(jnp.int32, x.shape, 0)
    vals, idxs = [], []
    outs = [[] for _ in payloads]
    for _ in range(k):
        m = jnp.max(x, axis=0, keepdims=True)
        i = jnp.min(jnp.where(x == m, iota, r), axis=0, keepdims=True)
        hit = iota == i
        vals.append(m)
        idxs.append(i)
        for p, acc in zip(payloads, outs):
            acc.append(jnp.sum(jnp.where(hit, p, 0), axis=0, keepdims=True))
        x = jnp.where(hit, -jnp.inf, x)
    cat = lambda parts: jnp.concatenate(parts, axis=0)
    return cat(vals), cat(idxs), [cat(acc) for acc in outs]


def _peer_topk_kernel(q_ref, keys_ref, eidx_ref, gate_ref):
    k = PEER_TOPK

    def one_head(h):
        sv, si = [], []
        for s in range(2):
            c0 = pl.multiple_of(h * (2 * PEER_QHALF) + s * PEER_QHALF, PEER_QHALF)
            qhs = q_ref[:, pl.ds(c0, PEER_QHALF)]
            sc_t = _dot1(keys_ref[s, h], qhs, _NT)
            v, i, _ = _topk_axis0(sc_t, k)
            sv.append(v)
            si.append(i)
        half = k // 2
        sub = lax.broadcasted_iota(jnp.int32, (half, sv[0].shape[1]), 0)
        cand_parts = [sv[0][0:1, :] + sv[1]]
        cidx_parts = [si[0][0:1, :] * N_KEYS + si[1]]
        for a in range(1, half):
            keep = sub < (k // (a + 1))
            cand_parts.append(jnp.where(keep, sv[0][a:a + 1, :] + sv[1][0:half, :], -jnp.inf))
            cidx_parts.append(si[0][a:a + 1, :] * N_KEYS + si[1][0:half, :])
        cand_parts.append(sv[0][half:, :] + sv[1][0:1, :])
        cidx_parts.append(si[0][half:, :] * N_KEYS + si[1][0:1, :])
        cand = jnp.concatenate(cand_parts, axis=0)
        cidx = jnp.concatenate(cidx_parts, axis=0)
        fv, _, (fe,) = _topk_axis0(cand, k, (cidx,))
        e = jnp.exp(fv - fv[0:1, :])
        gate = e / jnp.sum(e, axis=0, keepdims=True)
        r0 = pl.multiple_of(h * k, k)
        eidx_ref[pl.ds(r0, k), :] = fe
        gate_ref[pl.ds(r0, k), :] = gate

    def heads_body(hp, carry):
        for j in range(PEER_TOPK_HEADS_PER_ITER):
            one_head(hp * PEER_TOPK_HEADS_PER_ITER + j)
        return carry

    lax.fori_loop(0, PEER_HEADS // PEER_TOPK_HEADS_PER_ITER, heads_body, 0)


def _peer_topk(q, sub_keys, *, tk):
    n, qd = q.shape
    assert n % tk == 0
    rows = PEER_HEADS * PEER_TOPK
    return pl.pallas_call(
        _peer_topk_kernel,
        grid=(n // tk,),
        in_specs=[pl.BlockSpec((tk, qd), lambda i: (i, 0)),
                  pl.BlockSpec(sub_keys.shape, lambda i: (0, 0, 0, 0))],
        out_specs=[pl.BlockSpec((rows, tk), lambda i: (0, i)),
                   pl.BlockSpec((rows, tk), lambda i: (0, i))],
        out_shape=[jax.ShapeDtypeStruct((rows, n), jnp.int32),
                   jax.ShapeDtypeStruct((rows, n), F32)],
        compiler_params=_cparams("parallel"),
        name="peer_topk",
    )(q, sub_keys)


PEER_ROWS = PEER_HEADS * PEER_TOPK
PEER_SLOTS = 16
PEER_LOOK = PEER_SLOTS - 1
assert PEER_SLOTS % 2 == 0 and PEER_LOOK % 2 == 1
PEER_IDX_TAIL = 16
assert PEER_LOOK <= PEER_IDX_TAIL and (PEER_IDX_TAIL * PEER_ROWS) % 1024 == 0
PEER_ACCS = 4
PEER_DOT_COPIES = 5
PEER_MIX_COPIES = 3
assert (PEER_DOT_COPIES + PEER_MIX_COPIES) * (PEER_ROWS // SUBLANES) == PEER_ROWS


def _sublane_fold(parts):
    sub = lax.broadcasted_iota(jnp.int32, parts[0].shape, 0)
    for dist in (4, 2, 1):
        low = (sub & dist) == 0
        nxt = []
        for a in range(len(parts) // 2):
            lo_src, hi_src = parts[a], parts[a + len(parts) // 2]
            kept = jnp.where(low, lo_src, hi_src)
            if 2 * dist == SUBLANES:
                moved = pltpu.roll(jnp.where(low, hi_src, lo_src), dist, axis=0)
            else:
                moved = jnp.where(low, pltpu.roll(lo_src, SUBLANES - dist, axis=0), pltpu.roll(hi_src, dist, axis=0))
            nxt.append(kept + moved)
        parts = nxt
    return parts[0]


def _peer_gather_kernel(idx_ref, x_ref, gate_ref, h_ref, tab_ref, out_ref, buf_even, buf_odd, sem_ref, coef_ref, *, tb):
    rows = PEER_ROWS
    nc = x_ref.shape[1] // LANES
    half = nc // 2
    groups = rows // SUBLANES
    step = pl.program_id(0)
    hi_mask = jnp.uint32(0xFFFF0000)
    bufs = (buf_even, buf_odd)

    def slot_of(tok):
        return (tok // 2) % (PEER_SLOTS // 2)

    def start_rows(arr, tok, r0, count):
        for j in range(count):
            r = r0 + j
            pltpu.make_async_copy(tab_ref.at[idx_ref[tok * rows + r]], arr.at[slot_of(tok), r],
                                  sem_ref.at[tok % PEER_SLOTS]).start(priority=j % 2)

    def wait_token(arr, tok):
        pltpu.make_async_copy(tab_ref.at[pl.ds(0, rows)], arr.at[slot_of(tok)], sem_ref.at[tok % PEER_SLOTS]).wait()

    def load_x(t):
        xrow = x_ref[pl.ds(t, 1), :]
        x_lo = jnp.concatenate([xrow[:, c * LANES:(c + 1) * LANES] for c in range(half)], axis=0)
        x_hi = jnp.concatenate([xrow[:, c * LANES:(c + 1) * LANES] for c in range(half, nc)], axis=0)
        return x_lo, x_hi

    def dot_group(buf, x, g):
        parts = []
        for s in range(SUBLANES):
            r = g * SUBLANES + s
            u_lo = lax.bitcast_convert_type(buf[r, 0:half, :] << 16, F32)
            u_hi = lax.bitcast_convert_type(buf[r, half:nc, :] << 16, F32)
            parts.append(u_lo * x[0] + u_hi * x[1])
        return _sublane_fold(parts)

    def finish_dot(t, folded):
        act = jnp.sum(jnp.concatenate(folded, axis=0), axis=-1, keepdims=True)
        lane_tok = lax.broadcasted_iota(jnp.int32, (rows, tb), 1)
        gcol = jnp.sum(jnp.where(lane_tok == t, gate_ref[0], 0.0), axis=-1, keepdims=True)
        coef_ref[...] = jnp.broadcast_to(gcol * _gelu(act), (rows, LANES))

    def mix_group(buf, g, acc_lo, acc_hi):
        for s in range(SUBLANES):
            r = g * SUBLANES + s
            c = jnp.broadcast_to(coef_ref[r:r + 1, :], (half, LANES))
            a = r % PEER_ACCS
            acc_lo[a] = acc_lo[a] + lax.bitcast_convert_type(buf[r, 0:half, :] & hi_mask, F32) * c
            acc_hi[a] = acc_hi[a] + lax.bitcast_convert_type(buf[r, half:nc, :] & hi_mask, F32) * c

    def emit(t, acc_lo, acc_hi):
        mix_lo = (acc_lo[0] + acc_lo[1]) + (acc_lo[2] + acc_lo[3])
        mix_hi = (acc_hi[0] + acc_hi[1]) + (acc_hi[2] + acc_hi[3])
        mix_row = jnp.concatenate([mix_lo[c:c + 1, :] for c in range(half)]
                                  + [mix_hi[c:c + 1, :] for c in range(half)], axis=1)
        out_ref[pl.ds(t, 1), :] = h_ref[pl.ds(t, 1), :] + mix_row

    def token(t, parity):
        arr, arr_next = bufs[parity], bufs[(parity + PEER_LOOK) % 2]
        tok_next = t + PEER_LOOK
        wait_token(arr, t)
        buf = arr.at[slot_of(t)]
        x = load_x(t)
        folded = []
        for g in range(groups):
            start_rows(arr_next, tok_next, g * PEER_DOT_COPIES, PEER_DOT_COPIES)
            folded.append(dot_group(buf, x, g))
        finish_dot(t, folded)
        acc_lo = [jnp.zeros((half, LANES), F32) for _ in range(PEER_ACCS)]
        acc_hi = [jnp.zeros((half, LANES), F32) for _ in range(PEER_ACCS)]
        for g in range(groups):
            start_rows(arr_next, tok_next, groups * PEER_DOT_COPIES + g * PEER_MIX_COPIES, PEER_MIX_COPIES)
            mix_group(buf, g, acc_lo, acc_hi)
        emit(t, acc_lo, acc_hi)

    @pl.when(step == 0)
    def _():
        for s in range(PEER_LOOK):
            def prime(i, carry, s=s):
                start_rows(bufs[s % 2], s, i * SUBLANES, SUBLANES)
                return carry
            lax.fori_loop(0, groups, prime, 0)

    def pair(p, carry):
        token(2 * p, 0)
        token(2 * p + 1, 1)
        return carry

    lax.fori_loop(0, tb // 2, pair, 0)

    @pl.when(step == pl.num_programs(0) - 1)
    def _():
        for s in range(PEER_LOOK):
            wait_token(bufs[(tb + s) % 2], tb + s)


def _peer_gather(eidx, xn, gate_blocks, h, table, *, tb):
    n, d = h.shape
    rows = PEER_ROWS
    assert n % tb == 0 and tb % PEER_SLOTS == 0 and PEER_ACCS == 4
    nb = n // tb
    idx2 = eidx.reshape(nb, tb * rows)
    idx_ext = jnp.concatenate([idx2, jnp.roll(idx2[:, :PEER_IDX_TAIL * rows], -1, axis=0)], axis=1).reshape(-1)
    return pl.pallas_call(
        functools.partial(_peer_gather_kernel, tb=tb),
        grid=(nb,),
        in_specs=[pl.BlockSpec(((tb + PEER_IDX_TAIL) * rows,), lambda i: (i,), memory_space=pltpu.SMEM),
                  pl.BlockSpec((tb, d), lambda i: (i, 0)),
                  pl.BlockSpec((1, rows, tb), lambda i: (i, 0, 0)),
                  pl.BlockSpec((tb, d), lambda i: (i, 0)),
                  pl.BlockSpec(memory_space=pl.ANY)],
        out_specs=pl.BlockSpec((tb, d), lambda i: (i, 0)),
        out_shape=jax.ShapeDtypeStruct((n, d), F32),
        scratch_shapes=[pltpu.VMEM((PEER_SLOTS // 2, rows, d // LANES, LANES), jnp.uint32),
                        pltpu.VMEM((PEER_SLOTS // 2, rows, d // LANES, LANES), jnp.uint32),
                        pltpu.SemaphoreType.DMA((PEER_SLOTS,)),
                        pltpu.VMEM((rows, LANES), F32)],
        compiler_params=_cparams("arbitrary"),
        name="peer_gather",
    )(idx_ext, xn, gate_blocks, h, table)


def _ple_kernel(h_ref, p_ref, n3_ref, fn_ref, wg_ref, wp_ref, y_ref):
    h = h_ref[...]
    gate = _sigmoid(_dot1(_rms_rows(h, n3_ref[...]), wg_ref[...]))
    h = h + gate * _dot1(p_ref[...], wp_ref[...])
    y_ref[...] = _rms_rows(h, fn_ref[...])


def _ple_final(h, p, norm3, final_norm, w_gate, w_proj, *, tm):
    n, d = h.shape
    pd = p.shape[1]
    assert n % tm == 0
    return pl.pallas_call(
        _ple_kernel,
        grid=(n // tm,),
        in_specs=[pl.BlockSpec((tm, d), lambda i: (i, 0)),
                  pl.BlockSpec((tm, pd), lambda i: (i, 0)),
                  pl.BlockSpec((1, d), lambda i: (0, 0)),
                  pl.BlockSpec((1, d), lambda i: (0, 0)),
                  pl.BlockSpec((d, d), lambda i: (0, 0), pipeline_mode=pl.Buffered(1)),
                  pl.BlockSpec((pd, d), lambda i: (0, 0), pipeline_mode=pl.Buffered(1))],
        out_specs=pl.BlockSpec((tm, d), lambda i: (i, 0)),
        out_shape=jax.ShapeDtypeStruct((n, d), F32),
        compiler_params=_cparams("parallel"),
        name="ple_final",
    )(h, p, norm3.reshape(1, d), final_norm.reshape(1, d), w_gate.astype(BF16), w_proj.astype(BF16))


QK_DIM = GDN_HEADS * GDN_DK
V_DIM = GDN_HEADS * GDN_DV
CONV_DIM = 2 * QK_DIM + V_DIM
MLP_DIM = MLP_GROUPS * MLP_GROUP_DIM
Z_COL = CONV_DIM
UV_COL = Z_COL + V_DIM
AB_COL = UV_COL + 2 * MLP_DIM
PROJ_COLS = AB_COL + LANES


def _relayout_w_in(w_in):
    c0 = CONV_DIM
    c2 = c0 + 2 * GDN_HEADS
    c3 = c2 + V_DIM
    ab = jnp.pad(w_in[:, c0:c2], ((0, 0), (0, LANES - 2 * GDN_HEADS)))
    return jnp.concatenate([w_in[:, :c0], w_in[:, c2:c3], w_in[:, c3:], ab], axis=1)


def _pack_tables(peer_u, peer_v):
    ub = lax.bitcast_convert_type(peer_u.astype(BF16), jnp.uint16).astype(jnp.uint32)
    vb = lax.bitcast_convert_type(peer_v.astype(BF16), jnp.uint16).astype(jnp.uint32)
    e, d = peer_u.shape
    return (ub | (vb << 16)).reshape(e, d // LANES, LANES)


def _row_tile(n, want):
    t = min(n, want)
    assert n % t == 0
    return t


IN_PROJ_ROWS = 256
TOKEN_ROWS = 512
SEQ_ROWS = 512
PEER_TOKENS = 128


def _layer(x, p, s0, buf, w, final_norm):
    b, t, d = x.shape
    n = b * t
    x2 = x.reshape(n, d)
    tok_rows = _row_tile(n, TOKEN_ROWS)
    seq_rows = _row_tile(t, SEQ_ROWS)

    proj = _norm_matmul_resident(x2, w["norm1"], w["w_in"], tm=_row_tile(n, IN_PROJ_ROWS), name="in_proj")
    proj3 = proj.reshape(b, t, PROJ_COLS)
    new_buf = jnp.concatenate([buf, proj3[:, :, :CONV_DIM]], axis=1)[:, t:]

    buf8 = jnp.pad(buf, ((0, 0), (SUBLANES - (CONV_W - 1), 0), (0, 0)))
    q, k, v = _conv_qkv(proj3, buf8, w["conv_w"], tt=seq_rows)

    tg = -(-t // GDN_CHUNK) * GDN_CHUNK
    ab3 = proj3[:, :, AB_COL:]
    if tg != t:
        padt = ((0, 0), (0, tg - t), (0, 0))
        q, k, v, ab3 = (jnp.pad(a, padt) for a in (q, k, v, ab3))
    abt = jnp.transpose(ab3[:, :, :2 * GDN_HEADS].reshape(b, tg // GDN_CHUNK, GDN_CHUNK, 2 * GDN_HEADS),
                        (0, 1, 3, 2))
    o, s_new = _gdn(q, k, v, ab3, abt, w["a_log"], w["dt_bias"], s0, tt=_row_tile(tg, SEQ_ROWS), t_valid=t)
    o = o[:, :t]

    y_b, vn = _mlp_branch(proj3, UV_COL // (2 * MLP_DIM), w["ln_g"], w["ln_b"], w["w_s"], w["b_s"],
                          rows=seq_rows, lc=min(t, MLP_CHUNK))

    h1 = _outproj(o.reshape(n, V_DIM), proj, Z_COL // V_DIM, y_b.reshape(n, MLP_DIM), x2,
                  w["gdn_norm"], w["w_out"], tm=tok_rows)

    qp, xn2 = _norm_matmul_resident(h1, w["norm2"], w["w_q"], tm=tok_rows, emit_xn=True, name="peer_query")
    eidx_t, gate_t = _peer_topk(qp, w["sub_keys"], tk=tok_rows)
    gate_blocks = jnp.transpose(gate_t.reshape(PEER_ROWS, n // PEER_TOKENS, PEER_TOKENS), (1, 0, 2))
    h2 = _peer_gather(jnp.transpose(eidx_t), xn2, gate_blocks, h1, w["table"], tb=PEER_TOKENS)

    y = _ple_final(h2, p.reshape(n, -1), w["norm3"], final_norm, w["w_ple_gate"], w["w_ple_proj"], tm=tok_rows)
    return y.reshape(b, t, d), s_new, new_buf, vn


def kernel(x_prompt, x_sample, state_gdn, cache_conv, p_prompt, p_sample, norm1, w_in, conv_w, a_log, dt_bias,
           gdn_norm, ln_g, ln_b, w_s, b_s, w_out, norm2, w_q, sub_keys, peer_u, peer_v, norm3, w_ple_gate,
           w_ple_proj, final_norm):
    assert norm1.shape[0] == 1, "single layer"
    w = dict(norm1=norm1[0], w_in=_relayout_w_in(w_in[0]), conv_w=conv_w[0], a_log=a_log[0], dt_bias=dt_bias[0],
             gdn_norm=gdn_norm[0], ln_g=ln_g[0], ln_b=ln_b[0], w_s=w_s[0], b_s=b_s[0], w_out=w_out[0],
             norm2=norm2[0], w_q=w_q[0], sub_keys=sub_keys[0], table=_pack_tables(peer_u[0], peer_v[0]),
             norm3=norm3[0], w_ple_gate=w_ple_gate[0], w_ple_proj=w_ple_proj[0])
    bp = x_prompt.shape[0]
    s0p = jnp.zeros((bp, GDN_HEADS, GDN_DK, GDN_DV), F32)
    b0p = jnp.zeros((bp, CONV_W - 1, CONV_DIM), F32)
    y_s, s_s, b_s_new, v_s = _layer(x_sample, p_sample[0], state_gdn[0], cache_conv[0], w, final_norm)
    y_p, s_p, b_p, _ = _layer(x_prompt, p_prompt[0], s0p, b0p, w, final_norm)
    return (y_p, y_s, s_p[None], b_p[None], s_s[None], b_s_new[None], v_s[None])
```

```python
import functools
import math

import jax
import jax.numpy as jnp
from jax import lax
from jax.experimental import pallas as pl
from jax.experimental.pallas import tpu as pltpu

F32 = jnp.float32
BF16 = jnp.bfloat16

EPS = 1e-6
GDN_CHUNK = 64
GDN_HEADS = 8
GDN_CHUNKS_PER_ITER = 4
GDN_DK = 128
GDN_DV = 128
CONV_W = 4
MLP_CHUNK = 128
MLP_GROUPS = 8
MLP_GROUP_DIM = 128
PEER_HEADS = 8
PEER_QHALF = 128
N_KEYS = 128
PEER_TOPK = 16
PEER_TOPK_HEADS_PER_ITER = 8

LANES = 128
SUBLANES = 8
V7X_VMEM_BYTES = 64 * 1024 * 1024
VMEM_LIMIT_BYTES = V7X_VMEM_BYTES * 7 // 8


def _cparams(*sem):
    return pltpu.CompilerParams(dimension_semantics=sem, vmem_limit_bytes=VMEM_LIMIT_BYTES)


def _split3(x):
    hi = x.astype(BF16)
    r1 = x - hi.astype(F32)
    mid = r1.astype(BF16)
    lo = (r1 - mid.astype(F32)).astype(BF16)
    return hi, mid, lo


def _dg(a, b, dims):
    return lax.dot_general(a, b, (dims, ((), ())), preferred_element_type=F32)


_NN = ((1,), (0,))
_NT = ((1,), (1,))
_TN = ((0,), (0,))


def _dot1(a, b, dims=_NN):
    return _dg(a.astype(BF16), b.astype(BF16), dims)


def _dot3(a, b, dims=_NN):
    ah = a.astype(BF16)
    al = (a - ah.astype(F32)).astype(BF16)
    bh = b.astype(BF16)
    bl = (b - bh.astype(F32)).astype(BF16)
    return _dg(ah, bh, dims) + (_dg(ah, bl, dims) + _dg(al, bh, dims))


def _dot_exact_lhs(a_exact_bf16, b, dims=_NN):
    b0, b1, b2 = _split3(b)
    return _dg(a_exact_bf16, b0, dims) + (_dg(a_exact_bf16, b1, dims) + _dg(a_exact_bf16, b2, dims))


def _dot_exact_rhs(a, b_exact_bf16, dims=_NN):
    a0, a1, a2 = _split3(a)
    return _dg(a0, b_exact_bf16, dims) + (_dg(a1, b_exact_bf16, dims) + _dg(a2, b_exact_bf16, dims))


def _rms_rows(x, gain):
    ms = jnp.mean(x * x, axis=-1, keepdims=True)
    return x * lax.rsqrt(ms + EPS) * gain


def _sigmoid(x):
    return 1.0 / (1.0 + jnp.exp(-x))


def _silu(x):
    return x * _sigmoid(x)


def _gelu(x):
    return 0.5 * x * (1.0 + lax.erf(x * (1.0 / math.sqrt(2.0))))


def _softplus(x):
    return jnp.maximum(x, 0.0) + jnp.log1p(jnp.exp(-jnp.abs(x)))


def _norm_matmul_resident_kernel(x_ref, g_ref, w_ref, o_ref, *xn_out, emit_xn):
    xn = _rms_rows(x_ref[...], g_ref[...])
    if emit_xn:
        xn_out[0][...] = xn
    o_ref[...] = _dg(xn.astype(BF16), w_ref[...], _NN)


def _norm_matmul_resident(x, gain, w, *, tm, emit_xn=False, name):
    n, d = x.shape
    m = w.shape[1]
    assert n % tm == 0
    out_shape = [jax.ShapeDtypeStruct((n, m), F32)]
    out_specs = [pl.BlockSpec((tm, m), lambda i: (i, 0))]
    if emit_xn:
        out_shape.append(jax.ShapeDtypeStruct((n, d), F32))
        out_specs.append(pl.BlockSpec((tm, d), lambda i: (i, 0)))
    res = pl.pallas_call(
        functools.partial(_norm_matmul_resident_kernel, emit_xn=emit_xn),
        grid=(n // tm,),
        in_specs=[pl.BlockSpec((tm, d), lambda i: (i, 0)),
                  pl.BlockSpec((1, d), lambda i: (0, 0)),
                  pl.BlockSpec((d, m), lambda i: (0, 0), pipeline_mode=pl.Buffered(1))],
        out_specs=out_specs,
        out_shape=out_shape,
        compiler_params=_cparams("parallel"),
        name=name,
    )(x, gain.reshape(1, d), w.astype(BF16))
    return res if emit_xn else res[0]


def _conv_kernel(x_ref, buf_ref, w_ref, q_ref, k_ref, v_ref, xp_ref, *, tt):
    t = pl.program_id(1)
    halo = SUBLANES

    @pl.when(t == 0)
    def _():
        xp_ref[0:halo, :] = buf_ref[0]

    @pl.when(t > 0)
    def _():
        xp_ref[0:halo, :] = xp_ref[tt:tt + halo, :]

    xp_ref[halo:halo + tt, :] = x_ref[0]
    base = halo - (CONV_W - 1)
    y = xp_ref[base:base + tt, :] * w_ref[0:1, :]
    for j in range(1, CONV_W):
        y = y + xp_ref[base + j:base + j + tt, :] * w_ref[j:j + 1, :]
    y = _silu(y)
    qk_dim = GDN_HEADS * GDN_DK
    for h in range(GDN_HEADS):
        qh = y[:, h * GDN_DK:(h + 1) * GDN_DK]
        qn = qh * lax.rsqrt(jnp.sum(qh * qh, axis=-1, keepdims=True) + EPS)
        q_ref[0, :, h * GDN_DK:(h + 1) * GDN_DK] = qn * (GDN_DK ** -0.5)
        kh = y[:, qk_dim + h * GDN_DK:qk_dim + (h + 1) * GDN_DK]
        k_ref[0, :, h * GDN_DK:(h + 1) * GDN_DK] = kh * lax.rsqrt(jnp.sum(kh * kh, axis=-1, keepdims=True) + EPS)
    v_ref[0] = y[:, 2 * qk_dim:]


def _conv_qkv(proj3, buf8, conv_w, *, tt):
    b, t, _ = proj3.shape
    cdim = conv_w.shape[1]
    hd = GDN_HEADS * GDN_DK
    assert t % tt == 0
    shp = jax.ShapeDtypeStruct((b, t, hd), F32)
    ospec = pl.BlockSpec((1, tt, hd), lambda i, j: (i, j, 0))
    return pl.pallas_call(
        functools.partial(_conv_kernel, tt=tt),
        grid=(b, t // tt),
        in_specs=[pl.BlockSpec((1, tt, cdim), lambda i, j: (i, j, 0)),
                  pl.BlockSpec((1, SUBLANES, cdim), lambda i, j: (i, 0, 0)),
                  pl.BlockSpec((CONV_W, cdim), lambda i, j: (0, 0))],
        out_specs=[ospec, ospec, ospec],
        out_shape=[shp, shp, shp],
        scratch_shapes=[pltpu.VMEM((tt + 2 * SUBLANES, cdim), F32)],
        compiler_params=_cparams("parallel", "arbitrary"),
        name="conv_qkv",
    )(proj3, buf8, conv_w)


def _gdn_kernel(q_ref, k_ref, v_ref, ab_ref, abt_ref, prm_row_ref, prm_col_ref, s0_ref,
                o_ref, sout_ref, state_ref, *, tt, t_valid):
    tstep = pl.program_id(1)
    c = GDN_CHUNK

    @pl.when(tstep == 0)
    def _():
        state_ref[...] = s0_ref[0]

    ii = lax.broadcasted_iota(jnp.int32, (c, c), 0)
    jj = lax.broadcasted_iota(jnp.int32, (c, c), 1)
    incl = ii >= jj
    strict = ii > jj
    tri = incl.astype(BF16)
    tri_t = (jj >= ii).astype(BF16)
    eye = (ii == jj).astype(F32)

    alog_row = prm_row_ref[0:1, :]
    dtb_row = prm_row_ref[1:2, :]
    alog_col = prm_col_ref[:, 0:1]
    dtb_col = prm_col_ref[:, 1:2]

    def chunk_units(ci):
        r0 = pl.multiple_of(ci * c, c)
        tpos = tstep * tt + r0
        a_blk = ab_ref[0, pl.ds(r0, c), :]
        valid_col = (tpos + lax.broadcasted_iota(jnp.int32, (c, LANES), 0)) < t_valid
        g_col = jnp.where(valid_col, -jnp.exp(alog_row) * _softplus(a_blk + dtb_row), 0.0)
        beta_col = jnp.where(valid_col, _sigmoid(a_blk), 0.0)
        gc_col = _dot_exact_lhs(tri, g_col)
        at_blk = abt_ref[0, ci]
        valid_row = (tpos + lax.broadcasted_iota(jnp.int32, (2 * GDN_HEADS, c), 1)) < t_valid
        g_row = jnp.where(valid_row, -jnp.exp(alog_col) * _softplus(at_blk + dtb_col), 0.0)
        gc_row = _dot_exact_rhs(g_row, tri_t)
        units = []
        for h in range(GDN_HEADS):
            sl = slice(h * GDN_DK, (h + 1) * GDN_DK)
            qh = q_ref[0, pl.ds(r0, c), sl]
            kh = k_ref[0, pl.ds(r0, c), sl]
            vh = v_ref[0, pl.ds(r0, c), sl]
            gc = gc_col[:, h:h + 1]
            beta = beta_col[:, GDN_HEADS + h:GDN_HEADS + h + 1]
            gcr = gc_row[h:h + 1, :]
            gc_last = gc_col[c - 1:c, h:h + 1]
            decay = jnp.where(incl, jnp.exp(jnp.where(incl, gc - gcr, 0.0)), 0.0)
            egc = jnp.exp(gc)
            kb = kh * beta
            lmat = jnp.where(strict, _dot1(kb, kh, _NT) * decay, 0.0)
            units.append(dict(
                h=h, r0=r0, sl=sl, tinv=eye - lmat, pw=lmat,
                rhs=jnp.concatenate([vh * beta, kb * egc], axis=-1),
                attn=_dot1(qh, kh, _NT) * decay, q_dec=qh * egc,
                k_dec=kh * jnp.exp(gc_last - gc), bd=jnp.exp(gc_last)))
        return units

    def chunks_body(cp, carry):
        per_chunk = [chunk_units(cp * cpi + j) for j in range(cpi)]
        units = [u for us in per_chunk for u in us]
        for _ in range(5):
            for u in units:
                u["pw"] = _dot1(u["pw"], u["pw"])
            for u in units:
                u["tinv"] = u["tinv"] + _dot1(u["tinv"], u["pw"])
        for u in units:
            u["sol"] = _dot3(u["tinv"], u["rhs"])
        for us in per_chunk:
            for u in us:
                s = state_ref[u["h"]]
                u["s"] = s
                u["v_new"] = u["sol"][:, :GDN_DV] - _dot1(u["sol"][:, GDN_DV:], s)
            for u in us:
                o_ref[0, pl.ds(u["r0"], c), u["sl"]] = _dot1(u["q_dec"], u["s"]) + _dot1(u["attn"], u["v_new"])
                state_ref[u["h"]] = u["s"] * u["bd"] + _dot1(u["k_dec"], u["v_new"], _TN)
        return carry

    n_chunks = tt // c
    cpi = GDN_CHUNKS_PER_ITER if n_chunks % GDN_CHUNKS_PER_ITER == 0 else 1
    lax.fori_loop(0, n_chunks // cpi, chunks_body, 0)

    @pl.when(tstep == pl.num_programs(1) - 1)
    def _():
        sout_ref[0] = state_ref[...]


def _gdn(q, k, v, ab, abt, a_log, dt_bias, s0, *, tt, t_valid):
    b, t, hd = q.shape
    assert t % tt == 0 and tt % GDN_CHUNK == 0
    prm_row = jnp.zeros((SUBLANES, LANES), F32)
    prm_row = prm_row.at[0, :GDN_HEADS].set(a_log).at[1, :GDN_HEADS].set(dt_bias)
    prm_col = jnp.zeros((2 * GDN_HEADS, LANES), F32)
    prm_col = prm_col.at[:GDN_HEADS, 0].set(a_log).at[:GDN_HEADS, 1].set(dt_bias)
    seq = pl.BlockSpec((1, tt, hd), lambda i, j: (i, j, 0))
    st = pl.BlockSpec((1, GDN_HEADS, GDN_DK, GDN_DV), lambda i, j: (i, 0, 0, 0))
    return pl.pallas_call(
        functools.partial(_gdn_kernel, tt=tt, t_valid=t_valid),
        grid=(b, t // tt),
        in_specs=[seq, seq, seq,
                  pl.BlockSpec((1, tt, LANES), lambda i, j: (i, j, 0)),
                  pl.BlockSpec((1, tt // GDN_CHUNK, 2 * GDN_HEADS, GDN_CHUNK), lambda i, j: (i, j, 0, 0)),
                  pl.BlockSpec((SUBLANES, LANES), lambda i, j: (0, 0)),
                  pl.BlockSpec((2 * GDN_HEADS, LANES), lambda i, j: (0, 0)),
                  st],
        out_specs=[seq, st],
        out_shape=[jax.ShapeDtypeStruct((b, t, hd), F32),
                   jax.ShapeDtypeStruct((b, GDN_HEADS, GDN_DK, GDN_DV), F32)],
        scratch_shapes=[pltpu.VMEM((GDN_HEADS, GDN_DK, GDN_DV), F32)],
        compiler_params=_cparams("parallel", "arbitrary"),
        name="gdn_core",
    )(q, k, v, ab, abt, prm_row, prm_col, s0)


def _mlp_kernel(uv_ref, lng_ref, lnb_ref, ws_ref, bst_ref, y_ref, vn_ref, *, rows, lc):
    mdim = MLP_GROUPS * MLP_GROUP_DIM
    ii = lax.broadcasted_iota(jnp.int32, (lc, lc), 0)
    jj = lax.broadcasted_iota(jnp.int32, (lc, lc), 1)
    keep = ii >= jj
    for r in range(rows // lc):
        rs = slice(r * lc, (r + 1) * lc)
        g = _gelu(uv_ref[0, rs, :])
        u = g[:, :mdim]
        vv = g[:, mdim:]
        mu = jnp.mean(vv, axis=-1, keepdims=True)
        xc = vv - mu
        vn = xc * lax.rsqrt(jnp.mean(xc * xc, axis=-1, keepdims=True) + EPS) * lng_ref[...] + lnb_ref[...]
        vn_ref[0, rs, :] = vn
        for gi in range(MLP_GROUPS):
            gs = slice(gi * MLP_GROUP_DIM, (gi + 1) * MLP_GROUP_DIM)
            wm = jnp.where(keep, ws_ref[gi], 0.0)
            s = _dot1(wm, vn[:, gs]) + bst_ref[:, gi:gi + 1]
            y_ref[0, rs, gs] = u[:, gs] * s


def _mlp_branch(proj3, col_block, ln_g, ln_b, w_s, b_s, *, rows, lc):
    b, t, _ = proj3.shape
    mdim = MLP_GROUPS * MLP_GROUP_DIM
    assert t % rows == 0 and rows % lc == 0
    ws = w_s[:, :lc, :lc]
    bst = jnp.transpose(b_s[:, :lc])
    shp = jax.ShapeDtypeStruct((b, t, mdim), F32)
    ospec = pl.BlockSpec((1, rows, mdim), lambda i, j: (i, j, 0))
    return pl.pallas_call(
        functools.partial(_mlp_kernel, rows=rows, lc=lc),
        grid=(b, t // rows),
        in_specs=[pl.BlockSpec((1, rows, 2 * mdim), lambda i, j: (i, j, col_block)),
                  pl.BlockSpec((1, mdim), lambda i, j: (0, 0)),
                  pl.BlockSpec((1, mdim), lambda i, j: (0, 0)),
                  pl.BlockSpec((MLP_GROUPS, lc, lc), lambda i, j: (0, 0, 0)),
                  pl.BlockSpec((lc, MLP_GROUPS), lambda i, j: (0, 0))],
        out_specs=[ospec, ospec],
        out_shape=[shp, shp],
        compiler_params=_cparams("parallel", "parallel"),
        name="mlp_branch",
    )(proj3, ln_g.reshape(1, mdim), ln_b.reshape(1, mdim), ws, bst)


def _outproj_kernel(o_ref, z_ref, y_ref, h_ref, gn_ref, w_ref, out_ref):
    parts = []
    for hh in range(GDN_HEADS):
        sl = slice(hh * GDN_DV, (hh + 1) * GDN_DV)
        parts.append((_rms_rows(o_ref[:, sl], gn_ref[...]) * _silu(z_ref[:, sl])).astype(BF16))
    parts.append(y_ref[...].astype(BF16))
    cat = jnp.concatenate(parts, axis=-1)
    out_ref[...] = h_ref[...] + _dg(cat, w_ref[...], _NN)


def _outproj(o2, proj2, z_block, y2, h2, gdn_norm, w_out, *, tm):
    n, d = h2.shape
    vd = GDN_HEADS * GDN_DV
    assert n % tm == 0
    return pl.pallas_call(
        _outproj_kernel,
        grid=(n // tm,),
        in_specs=[pl.BlockSpec((tm, vd), lambda i: (i, 0)),
                  pl.BlockSpec((tm, vd), lambda i: (i, z_block)),
                  pl.BlockSpec((tm, vd), lambda i: (i, 0)),
                  pl.BlockSpec((tm, d), lambda i: (i, 0)),
                  pl.BlockSpec((1, GDN_DV), lambda i: (0, 0)),
                  pl.BlockSpec(w_out.shape, lambda i: (0, 0), pipeline_mode=pl.Buffered(1))],
        out_specs=pl.BlockSpec((tm, d), lambda i: (i, 0)),
        out_shape=jax.ShapeDtypeStruct((n, d), F32),
        compiler_params=_cparams("parallel"),
        name="out_proj",
    )(o2, proj2, y2, h2, gdn_norm.reshape(1, GDN_DV), w_out.astype(BF16))


def _topk_axis0(x, k, payloads=()):
    r = x.shape[0]
    iota = lax.broadcasted_iota(jnp.int32, x.shape, 0)
    vals, idxs = [], []
    outs = [[] for _ in payloads]
    for _ in range(k):
        m = jnp.max(x, axis=0, keepdims=True)
        i = jnp.min(jnp.where(x == m, iota, r), axis=0, keepdims=True)
        hit = iota == i
        vals.append(m)
        idxs.append(i)
        for p, acc in zip(payloads, outs):
            acc.append(jnp.sum(jnp.where(hit, p, 0), axis=0, keepdims=True))
        x = jnp.where(hit, -jnp.inf, x)
    cat = lambda parts: jnp.concatenate(parts, axis=0)
    return cat(vals), cat(idxs), [cat(acc) for acc in outs]


def _peer_topk_kernel(q_ref, keys_ref, eidx_ref, gate_ref):
    k = PEER_TOPK

    def one_head(h):
        sv, si = [], []
        for s in range(2):
            c0 = pl.multiple_of(h * (2 * PEER_QHALF) + s * PEER_QHALF, PEER_QHALF)
            qhs = q_ref[:, pl.ds(c0, PEER_QHALF)]
            sc_t = _dot1(keys_ref[s, h], qhs, _NT)
            v, i, _ = _topk_axis0(sc_t, k)
            sv.append(v)
            si.append(i)
        half = k // 2
        sub = lax.broadcasted_iota(jnp.int32, (half, sv[0].shape[1]), 0)
        cand_parts = [sv[0][0:1, :] + sv[1]]
        cidx_parts = [si[0][0:1, :] * N_KEYS + si[1]]
        for a in range(1, half):
            keep = sub < (k // (a + 1))
            cand_parts.append(jnp.where(keep, sv[0][a:a + 1, :] + sv[1][0:half, :], -jnp.inf))
            cidx_parts.append(si[0][a:a + 1, :] * N_KEYS + si[1][0:half, :])
        cand_parts.append(sv[0][half:, :] + sv[1][0:1, :])
        cidx_parts.append(si[0][half:, :] * N_KEYS + si[1][0:1, :])
        cand = jnp.concatenate(cand_parts, axis=0)
        cidx = jnp.concatenate(cidx_parts, axis=0)
        fv, _, (fe,) = _topk_axis0(cand, k, (cidx,))
        e = jnp.exp(fv - fv[0:1, :])
        gate = e / jnp.sum(e, axis=0, keepdims=True)
        r0 = pl.multiple_of(h * k, k)
        eidx_ref[pl.ds(r0, k), :] = fe
        gate_ref[pl.ds(r0, k), :] = gate

    def heads_body(hp, carry):
        for j in range(PEER_TOPK_HEADS_PER_ITER):
            one_head(hp * PEER_TOPK_HEADS_PER_ITER + j)
        return carry

    lax.fori_loop(0, PEER_HEADS // PEER_TOPK_HEADS_PER_ITER, heads_body, 0)


def _peer_topk(q, sub_keys, *, tk):
    n, qd = q.shape
    assert n % tk == 0
    rows = PEER_HEADS * PEER_TOPK
    return pl.pallas_call(
        _peer_topk_kernel,
        grid=(n // tk,),
        in_specs=[pl.BlockSpec((tk, qd), lambda i: (i, 0)),
                  pl.BlockSpec(sub_keys.shape, lambda i: (0, 0, 0, 0))],
        out_specs=[pl.BlockSpec((rows, tk), lambda i: (0, i)),
                   pl.BlockSpec((rows, tk), lambda i: (0, i))],
        out_shape=[jax.ShapeDtypeStruct((rows, n), jnp.int32),
                   jax.ShapeDtypeStruct((rows, n), F32)],
        compiler_params=_cparams("parallel"),
        name="peer_topk",
    )(q, sub_keys)


PEER_ROWS = PEER_HEADS * PEER_TOPK
PEER_SLOTS = 16
PEER_LOOK = PEER_SLOTS - 1
assert PEER_SLOTS % 2 == 0 and PEER_LOOK % 2 == 1
PEER_IDX_TAIL = 16
assert PEER_LOOK <= PEER_IDX_TAIL and (PEER_IDX_TAIL * PEER_ROWS) % 1024 == 0
PEER_ACCS = 4
PEER_DOT_COPIES = 5
PEER_MIX_COPIES = 3
assert (PEER_DOT_COPIES + PEER_MIX_COPIES) * (PEER_ROWS // SUBLANES) == PEER_ROWS


def _sublane_fold(parts):
    sub = lax.broadcasted_iota(jnp.int32, parts[0].shape, 0)
    for dist in (4, 2, 1):
        low = (sub & dist) == 0
        nxt = []
        for a in range(len(parts) // 2):
            lo_src, hi_src = parts[a], parts[a + len(parts) // 2]
            kept = jnp.where(low, lo_src, hi_src)
            if 2 * dist == SUBLANES:
                moved = pltpu.roll(jnp.where(low, hi_src, lo_src), dist, axis=0)
            else:
                moved = jnp.where(low, pltpu.roll(lo_src, SUBLANES - dist, axis=0), pltpu.roll(hi_src, dist, axis=0))
            nxt.append(kept + moved)
        parts = nxt
    return parts[0]


def _peer_gather_kernel(idx_ref, x_ref, gate_ref, h_ref, tab_ref, out_ref, buf_even, buf_odd, sem_ref, coef_ref, *, tb):
    rows = PEER_ROWS
    nc = x_ref.shape[1] // LANES
    half = nc // 2
    groups = rows // SUBLANES
    step = pl.program_id(0)
    hi_mask = jnp.uint32(0xFFFF0000)
    bufs = (buf_even, buf_odd)

    def slot_of(tok):
        return (tok // 2) % (PEER_SLOTS // 2)

    def start_rows(arr, tok, r0, count):
        for j in range(count):
            r = r0 + j
            pltpu.make_async_copy(tab_ref.at[idx_ref[tok * rows + r]], arr.at[slot_of(tok), r],
                                  sem_ref.at[tok % PEER_SLOTS]).start(priority=j % 2)

    def wait_token(arr, tok):
        pltpu.make_async_copy(tab_ref.at[pl.ds(0, rows)], arr.at[slot_of(tok)], sem_ref.at[tok % PEER_SLOTS]).wait()

    def load_x(t):
        xrow = x_ref[pl.ds(t, 1), :]
        x_lo = jnp.concatenate([xrow[:, c * LANES:(c + 1) * LANES] for c in range(half)], axis=0)
        x_hi = jnp.concatenate([xrow[:, c * LANES:(c + 1) * LANES] for c in range(half, nc)], axis=0)
        return x_lo, x_hi

    def dot_group(buf, x, g):
        parts = []
        for s in range(SUBLANES):
            r = g * SUBLANES + s
            u_lo = lax.bitcast_convert_type(buf[r, 0:half, :] << 16, F32)
            u_hi = lax.bitcast_convert_type(buf[r, half:nc, :] << 16, F32)
            parts.append(u_lo * x[0] + u_hi * x[1])
        return _sublane_fold(parts)

    def finish_dot(t, folded):
        act = jnp.sum(jnp.concatenate(folded, axis=0), axis=-1, keepdims=True)
        lane_tok = lax.broadcasted_iota(jnp.int32, (rows, tb), 1)
        gcol = jnp.sum(jnp.where(lane_tok == t, gate_ref[0], 0.0), axis=-1, keepdims=True)
        coef_ref[...] = jnp.broadcast_to(gcol * _gelu(act), (rows, LANES))

    def mix_group(buf, g, acc_lo, acc_hi):
        for s in range(SUBLANES):
            r = g * SUBLANES + s
            c = jnp.broadcast_to(coef_ref[r:r + 1, :], (half, LANES))
            a = r % PEER_ACCS
            acc_lo[a] = acc_lo[a] + lax.bitcast_convert_type(buf[r, 0:half, :] & hi_mask, F32) * c
            acc_hi[a] = acc_hi[a] + lax.bitcast_convert_type(buf[r, half:nc, :] & hi_mask, F32) * c

    def emit(t, acc_lo, acc_hi):
        mix_lo = (acc_lo[0] + acc_lo[1]) + (acc_lo[2] + acc_lo[3])
        mix_hi = (acc_hi[0] + acc_hi[1]) + (acc_hi[2] + acc_hi[3])
        mix_row = jnp.concatenate([mix_lo[c:c + 1, :] for c in range(half)]
                                  + [mix_hi[c:c + 1, :] for c in range(half)], axis=1)
        out_ref[pl.ds(t, 1), :] = h_ref[pl.ds(t, 1), :] + mix_row

    def token(t, parity):
        arr, arr_next = bufs[parity], bufs[(parity + PEER_LOOK) % 2]
        tok_next = t + PEER_LOOK
        wait_token(arr, t)
        buf = arr.at[slot_of(t)]
        x = load_x(t)
        folded = []
        for g in range(groups):
            folded.append(dot_group(buf, x, g))
            start_rows(arr_next, tok_next, g * PEER_DOT_COPIES, PEER_DOT_COPIES)
        finish_dot(t, folded)
        acc_lo = [jnp.zeros((half, LANES), F32) for _ in range(PEER_ACCS)]
        acc_hi = [jnp.zeros((half, LANES), F32) for _ in range(PEER_ACCS)]
        for g in range(groups):
            mix_group(buf, g, acc_lo, acc_hi)
            start_rows(arr_next, tok_next, groups * PEER_DOT_COPIES + g * PEER_MIX_COPIES, PEER_MIX_COPIES)
        emit(t, acc_lo, acc_hi)

    @pl.when(step == 0)
    def _():
        for s in range(PEER_LOOK):
            def prime(i, carry, s=s):
                start_rows(bufs[s % 2], s, i * SUBLANES, SUBLANES)
                return carry
            lax.fori_loop(0, groups, prime, 0)

    def pair(p, carry):
        token(2 * p, 0)
        token(2 * p + 1, 1)
        return carry

    lax.fori_loop(0, tb // 2, pair, 0)

    @pl.when(step == pl.num_programs(0) - 1)
    def _():
        for s in range(PEER_LOOK):
            wait_token(bufs[(tb + s) % 2], tb + s)


def _peer_gather(eidx, xn, gate_blocks, h, table, *, tb):
    n, d = h.shape
    rows = PEER_ROWS
    assert n % tb == 0 and tb % PEER_SLOTS == 0 and PEER_ACCS == 4
    nb = n // tb
    idx2 = eidx.reshape(nb, tb * rows)
    idx_ext = jnp.concatenate([idx2, jnp.roll(idx2[:, :PEER_IDX_TAIL * rows], -1, axis=0)], axis=1).reshape(-1)
    return pl.pallas_call(
        functools.partial(_peer_gather_kernel, tb=tb),
        grid=(nb,),
        in_specs=[pl.BlockSpec(((tb + PEER_IDX_TAIL) * rows,), lambda i: (i,), memory_space=pltpu.SMEM),
                  pl.BlockSpec((tb, d), lambda i: (i, 0)),
                  pl.BlockSpec((1, rows, tb), lambda i: (i, 0, 0)),
                  pl.BlockSpec((tb, d), lambda i: (i, 0)),
                  pl.BlockSpec(memory_space=pl.ANY)],
        out_specs=pl.BlockSpec((tb, d), lambda i: (i, 0)),
        out_shape=jax.ShapeDtypeStruct((n, d), F32),
        scratch_shapes=[pltpu.VMEM((PEER_SLOTS // 2, rows, d // LANES, LANES), jnp.uint32),
                        pltpu.VMEM((PEER_SLOTS // 2, rows, d // LANES, LANES), jnp.uint32),
                        pltpu.SemaphoreType.DMA((PEER_SLOTS,)),
                        pltpu.VMEM((rows, LANES), F32)],
        compiler_params=_cparams("arbitrary"),
        name="peer_gather",
    )(idx_ext, xn, gate_blocks, h, table)


def _ple_kernel(h_ref, p_ref, n3_ref, fn_ref, wg_ref, wp_ref, y_ref):
    h = h_ref[...]
    gate = _sigmoid(_dot1(_rms_rows(h, n3_ref[...]), wg_ref[...]))
    h = h + gate * _dot1(p_ref[...], wp_ref[...])
    y_ref[...] = _rms_rows(h, fn_ref[...])


def _ple_final(h, p, norm3, final_norm, w_gate, w_proj, *, tm):
    n, d = h.shape
    pd = p.shape[1]
    assert n % tm == 0
    return pl.pallas_call(
        _ple_kernel,
        grid=(n // tm,),
        in_specs=[pl.BlockSpec((tm, d), lambda i: (i, 0)),
                  pl.BlockSpec((tm, pd), lambda i: (i, 0)),
                  pl.BlockSpec((1, d), lambda i: (0, 0)),
                  pl.BlockSpec((1, d), lambda i: (0, 0)),
                  pl.BlockSpec((d, d), lambda i: (0, 0), pipeline_mode=pl.Buffered(1)),
                  pl.BlockSpec((pd, d), lambda i: (0, 0), pipeline_mode=pl.Buffered(1))],
        out_specs=pl.BlockSpec((tm, d), lambda i: (i, 0)),
        out_shape=jax.ShapeDtypeStruct((n, d), F32),
        compiler_params=_cparams("parallel"),
        name="ple_final",
    )(h, p, norm3.reshape(1, d), final_norm.reshape(1, d), w_gate.astype(BF16), w_proj.astype(BF16))


QK_DIM = GDN_HEADS * GDN_DK
V_DIM = GDN_HEADS * GDN_DV
CONV_DIM = 2 * QK_DIM + V_DIM
MLP_DIM = MLP_GROUPS * MLP_GROUP_DIM
Z_COL = CONV_DIM
UV_COL = Z_COL + V_DIM
AB_COL = UV_COL + 2 * MLP_DIM
PROJ_COLS = AB_COL + LANES


def _relayout_w_in(w_in):
    c0 = CONV_DIM
    c2 = c0 + 2 * GDN_HEADS
    c3 = c2 + V_DIM
    ab = jnp.pad(w_in[:, c0:c2], ((0, 0), (0, LANES - 2 * GDN_HEADS)))
    return jnp.concatenate([w_in[:, :c0], w_in[:, c2:c3], w_in[:, c3:], ab], axis=1)


def _pack_tables(peer_u, peer_v):
    ub = lax.bitcast_convert_type(peer_u.astype(BF16), jnp.uint16).astype(jnp.uint32)
    vb = lax.bitcast_convert_type(peer_v.astype(BF16), jnp.uint16).astype(jnp.uint32)
    e, d = peer_u.shape
    return (ub | (vb << 16)).reshape(e, d // LANES, LANES)


def _row_tile(n, want):
    t = min(n, want)
    assert n % t == 0
    return t


IN_PROJ_ROWS = 256
TOKEN_ROWS = 512
SEQ_ROWS = 512
PEER_TOKENS = 128


def _layer(x, p, s0, buf, w, final_norm):
    b, t, d = x.shape
    n = b * t
    x2 = x.reshape(n, d)
    tok_rows = _row_tile(n, TOKEN_ROWS)
    seq_rows = _row_tile(t, SEQ_ROWS)

    proj = _norm_matmul_resident(x2, w["norm1"], w["w_in"], tm=_row_tile(n, IN_PROJ_ROWS), name="in_proj")
    proj3 = proj.reshape(b, t, PROJ_COLS)
    new_buf = jnp.concatenate([buf, proj3[:, :, :CONV_DIM]], axis=1)[:, t:]

    buf8 = jnp.pad(buf, ((0, 0), (SUBLANES - (CONV_W - 1), 0), (0, 0)))
    q, k, v = _conv_qkv(proj3, buf8, w["conv_w"], tt=seq_rows)

    tg = -(-t // GDN_CHUNK) * GDN_CHUNK
    ab3 = proj3[:, :, AB_COL:]
    if tg != t:
        padt = ((0, 0), (0, tg - t), (0, 0))
        q, k, v, ab3 = (jnp.pad(a, padt) for a in (q, k, v, ab3))
    abt = jnp.transpose(ab3[:, :, :2 * GDN_HEADS].reshape(b, tg // GDN_CHUNK, GDN_CHUNK, 2 * GDN_HEADS),
                        (0, 1, 3, 2))
    o, s_new = _gdn(q, k, v, ab3, abt, w["a_log"], w["dt_bias"], s0, tt=_row_tile(tg, SEQ_ROWS), t_valid=t)
    o = o[:, :t]

    y_b, vn = _mlp_branch(proj3, UV_COL // (2 * MLP_DIM), w["ln_g"], w["ln_b"], w["w_s"], w["b_s"],
                          rows=seq_rows, lc=min(t, MLP_CHUNK))

    h1 = _outproj(o.reshape(n, V_DIM), proj, Z_COL // V_DIM, y_b.reshape(n, MLP_DIM), x2,
                  w["gdn_norm"], w["w_out"], tm=tok_rows)

    qp, xn2 = _norm_matmul_resident(h1, w["norm2"], w["w_q"], tm=tok_rows, emit_xn=True, name="peer_query")
    eidx_t, gate_t = _peer_topk(qp, w["sub_keys"], tk=tok_rows)
    gate_blocks = jnp.transpose(gate_t.reshape(PEER_ROWS, n // PEER_TOKENS, PEER_TOKENS), (1, 0, 2))
    h2 = _peer_gather(jnp.transpose(eidx_t), xn2, gate_blocks, h1, w["table"], tb=PEER_TOKENS)

    y = _ple_final(h2, p.reshape(n, -1), w["norm3"], final_norm, w["w_ple_gate"], w["w_ple_proj"], tm=tok_rows)
    return y.reshape(b, t, d), s_new, new_buf, vn


def kernel(x_prompt, x_sample, state_gdn, cache_conv, p_prompt, p_sample, norm1, w_in, conv_w, a_log, dt_bias,
           gdn_norm, ln_g, ln_b, w_s, b_s, w_out, norm2, w_q, sub_keys, peer_u, peer_v, norm3, w_ple_gate,
           w_ple_proj, final_norm):
    assert norm1.shape[0] == 1, "single layer"
    w = dict(norm1=norm1[0], w_in=_relayout_w_in(w_in[0]), conv_w=conv_w[0], a_log=a_log[0], dt_bias=dt_bias[0],
             gdn_norm=gdn_norm[0], ln_g=ln_g[0], ln_b=ln_b[0], w_s=w_s[0], b_s=b_s[0], w_out=w_out[0],
             norm2=norm2[0], w_q=w_q[0], sub_keys=sub_keys[0], table=_pack_tables(peer_u[0], peer_v[0]),
             norm3=norm3[0], w_ple_gate=w_ple_gate[0], w_ple_proj=w_ple_proj[0])
    bp = x_prompt.shape[0]
    s0p = jnp.zeros((bp, GDN_HEADS, GDN_DK, GDN_DV), F32)
    b0p = jnp.zeros((bp, CONV_W - 1, CONV_DIM), F32)
    y_s, s_s, b_s_new, v_s = _layer(x_sample, p_sample[0], state_gdn[0], cache_conv[0], w, final_norm)
    y_p, s_p, b_p, _ = _layer(x_prompt, p_prompt[0], s0p, b0p, w, final_norm)
    return (y_p, y_s, s_p[None], b_p[None], s_s[None], b_s_new[None], v_s[None])
```
